```python
import math
import jax
import jax.numpy as jnp
from jax import lax
import numpy as np


D_MODEL = 1024
BATCH = 2
SEQ = 16384
DEPTH = 1

DN_HEADS = 4
DN_HEAD_DIM = 128
DN_WIDTH = DN_HEADS * DN_HEAD_DIM
DN_CHUNK = 64
CONV_WIDTH = 4
SG_GROUPS = 4
SG_GROUP_DIM = 128
SG_WIDTH = SG_GROUPS * SG_GROUP_DIM
SG_CHUNK = 128
IN_SPLITS = [3 * DN_WIDTH, 4 * DN_WIDTH, 4 * DN_WIDTH + DN_HEADS, 4 * DN_WIDTH + 2 * DN_HEADS, 4 * DN_WIDTH + 2 * DN_HEADS + SG_WIDTH]
IN_COLS = 4 * DN_WIDTH + 2 * DN_HEADS + 2 * SG_WIDTH
N_EXPERTS = 256
N_EXPERT_GROUPS = 8
TOPK_GROUPS = 4
TOP_K = 8
EXPERT_HIDDEN = 256
SHARED_HIDDEN = 256
ROUTED_SCALE = 2.5
MOE_BLOCK = 128
EPS = 1e-6

kernel_name = 'hybrid_gdn_sgmlp_moe_block'


def _rmsnorm(x, w):
    xf = x.astype(jnp.float32)
    y = xf * lax.rsqrt(jnp.mean(xf * xf, axis=-1, keepdims=True) + EPS)
    return (y * w.astype(jnp.float32)).astype(x.dtype)


def _layernorm(x, w, b):
    xf = x.astype(jnp.float32)
    mu = jnp.mean(xf, axis=-1, keepdims=True)
    var = jnp.mean(jnp.square(xf - mu), axis=-1, keepdims=True)
    y = (xf - mu) * lax.rsqrt(var + EPS) * w.astype(jnp.float32) + b.astype(jnp.float32)
    return y.astype(x.dtype)


def _l2norm(x):
    return x * lax.rsqrt(jnp.sum(x * x, axis=-1, keepdims=True) + EPS)


def _causal_conv_silu(x, w):
    s = x.shape[1]
    xp = jnp.pad(x, ((0, 0), (CONV_WIDTH - 1, 0), (0, 0)))
    y = xp[:, 0:s, :] * w[0]
    for i in range(1, CONV_WIDTH):
        y = y + xp[:, i:i + s, :] * w[i]
    return jax.nn.silu(y)


def _gated_delta_rule(q, k, v, g, beta):
    b_, s, h, dk = q.shape
    dv = v.shape[-1]
    c = DN_CHUNK
    n = s // c
    q = _l2norm(q) * (dk ** -0.5)
    k = _l2norm(k)
    q = q.reshape(b_, n, c, h, dk)
    k = k.reshape(b_, n, c, h, dk)
    v = v.reshape(b_, n, c, h, dv)
    g = g.reshape(b_, n, c, h)
    beta = beta.reshape(b_, n, c, h)
    gc = jnp.cumsum(g, axis=2)
    gch = jnp.swapaxes(gc, 2, 3)
    diff = gch[..., :, None] - gch[..., None, :]
    causal = jnp.tril(jnp.ones((c, c), dtype=bool))
    strict = jnp.tril(jnp.ones((c, c), dtype=bool), -1)
    decay = jnp.where(causal, jnp.exp(jnp.where(causal, diff, 0.0)), 0.0)
    kb = k * beta[..., None]
    lmat = jnp.where(strict, jnp.einsum('bnihk,bnjhk->bnhij', kb, k) * decay, 0.0)
    rhs = jnp.concatenate([v * beta[..., None], kb * jnp.exp(gc)[..., None]], axis=-1)
    rhs = jnp.transpose(rhs, (0, 1, 3, 2, 4))
    sol = lax.linalg.triangular_solve(lmat + jnp.eye(c, dtype=jnp.float32), rhs, left_side=True, lower=True, unit_diagonal=True)
    u_c = sol[..., :dv]
    w_c = sol[..., dv:]
    qk = jnp.einsum('bnihk,bnjhk->bnhij', q, k) * decay
    q_dec = jnp.transpose(q * jnp.exp(gc)[..., None], (0, 1, 3, 2, 4))
    g_last = gc[:, :, -1, :]
    k_dec = jnp.transpose(k * jnp.exp(g_last[:, :, None, :] - gc)[..., None], (0, 1, 3, 2, 4))

    def step(state, inp):
        u_n, w_n, q_n, k_n, qk_n, gl_n = inp
        v_new = u_n - jnp.einsum('bhck,bhkv->bhcv', w_n, state)
        o = jnp.einsum('bhck,bhkv->bhcv', q_n, state) + jnp.einsum('bhij,bhjv->bhiv', qk_n, v_new)
        state = state * jnp.exp(gl_n)[..., None, None] + jnp.einsum('bhck,bhcv->bhkv', k_n, v_new)
        return state, o

    xs = tuple(jnp.moveaxis(t, 1, 0) for t in (u_c, w_c, q_dec, k_dec, qk, g_last))
    s0 = jnp.zeros((b_, h, dk, dv), jnp.float32)
    _, o = lax.scan(step, s0, xs)
    return jnp.transpose(o, (1, 0, 3, 2, 4)).reshape(b_, s, h, dv)


def _token_mixer(h, w_in, conv_w, a_log, dt_bias, dn_norm_w, sg_ln_w, sg_ln_b, sg_w, sg_b, w_branch_gate, b_branch_gate, w_proj_dn, w_proj_sg, w_out):
    b_, s, _ = h.shape
    qkv, z, a, bl, u, vg = jnp.split(h @ w_in, IN_SPLITS, axis=-1)
    qkv = _causal_conv_silu(qkv, conv_w).astype(jnp.float32).reshape(b_, s, 3, DN_HEADS, DN_HEAD_DIM)
    q = qkv[:, :, 0]
    k = qkv[:, :, 1]
    v = qkv[:, :, 2]
    beta = jax.nn.sigmoid(bl.astype(jnp.float32))
    g = -jnp.exp(a_log.astype(jnp.float32)) * jax.nn.softplus(a.astype(jnp.float32) + dt_bias.astype(jnp.float32))
    o = _gated_delta_rule(q, k, v, g, beta)
    zf = z.astype(jnp.float32).reshape(b_, s, DN_HEADS, DN_HEAD_DIM)
    y_dn = (_rmsnorm(o, dn_norm_w) * jax.nn.silu(zf)).reshape(b_, s, DN_WIDTH).astype(h.dtype)
    u = jax.nn.gelu(u, approximate=False)
    vg = _layernorm(jax.nn.gelu(vg, approximate=False), sg_ln_w, sg_ln_b)
    vg = vg.reshape(b_, s // SG_CHUNK, SG_CHUNK, SG_GROUPS, SG_GROUP_DIM)
    ws = jnp.where(jnp.tril(jnp.ones((SG_CHUNK, SG_CHUNK), dtype=bool)), sg_w, 0.0)
    mixed = jnp.einsum('gts,bnsgc->bntgc', ws.astype(vg.dtype), vg) + sg_b.T[:, :, None].astype(vg.dtype)
    y_sg = u * mixed.reshape(b_, s, SG_WIDTH)
    gate_dn, gate_sg = jnp.split(jax.nn.sigmoid(h @ w_branch_gate + b_branch_gate), 2, axis=-1)
    merged = gate_dn * (y_dn @ w_proj_dn) + gate_sg * (y_sg @ w_proj_sg)
    return merged @ w_out


def _moe(h, w_router, router_bias, w_gate, w_up, w_down, ws_gate, ws_up, ws_down):
    b_, s, d = h.shape
    t = b_ * s
    ht = h.reshape(t, d)
    scores = jax.nn.sigmoid((ht @ w_router).astype(jnp.float32))
    sel = scores + router_bias.astype(jnp.float32)
    per_group = N_EXPERTS // N_EXPERT_GROUPS
    grp_score = jnp.sum(lax.top_k(sel.reshape(t, N_EXPERT_GROUPS, per_group), 2)[0], axis=-1)
    _, top_grp = lax.top_k(grp_score, TOPK_GROUPS)
    grp_mask = jnp.sum(jax.nn.one_hot(top_grp, N_EXPERT_GROUPS, dtype=jnp.float32), axis=1) > 0
    exp_mask = jnp.repeat(grp_mask, per_group, axis=1)
    _, idx = lax.top_k(jnp.where(exp_mask, sel, -jnp.inf), TOP_K)
    wts = jnp.take_along_axis(scores, idx, axis=1)
    wts = wts / jnp.sum(wts, axis=-1, keepdims=True) * ROUTED_SCALE
    a = t * TOP_K
    flat_e = idx.reshape(a)
    flat_tok = jnp.repeat(jnp.arange(t, dtype=jnp.int32), TOP_K)
    flat_w = wts.reshape(a)
    order = jnp.argsort(flat_e)
    sorted_e = flat_e[order]
    counts = jnp.bincount(flat_e, length=N_EXPERTS)
    padded = (counts + MOE_BLOCK - 1) // MOE_BLOCK * MOE_BLOCK
    starts = jnp.cumsum(counts) - counts
    pends = jnp.cumsum(padded)
    pstarts = pends - padded
    dest = pstarts[sorted_e] + jnp.arange(a, dtype=jnp.int32) - starts[sorted_e]
    nb = -(-a // MOE_BLOCK) + N_EXPERTS
    buf_tok = jnp.full((nb * MOE_BLOCK,), t, jnp.int32).at[dest].set(flat_tok[order])
    buf_w = jnp.zeros((nb * MOE_BLOCK,), jnp.float32).at[dest].set(flat_w[order])
    blk_e = jnp.minimum(jnp.searchsorted(pends, jnp.arange(nb, dtype=jnp.int32) * MOE_BLOCK, side='right'), N_EXPERTS - 1)
    x_pad = jnp.concatenate([ht, jnp.zeros((1, d), ht.dtype)], axis=0)

    def body(acc, blk):
        tok, wt, e = blk
        xb = x_pad[tok]
        hb = jax.nn.silu(xb @ w_gate[e]) * (xb @ w_up[e])
        yb = (hb @ w_down[e]).astype(jnp.float32) * wt[:, None]
        return acc.at[tok].add(yb), None

    acc, _ = lax.scan(body, jnp.zeros((t + 1, d), jnp.float32), (buf_tok.reshape(nb, MOE_BLOCK), buf_w.reshape(nb, MOE_BLOCK), blk_e))
    shared = (jax.nn.silu(ht @ ws_gate) * (ht @ ws_up)) @ ws_down
    return (acc[:t].astype(h.dtype) + shared).reshape(b_, s, d)


def setup_inputs(seed: int = 0):
    key = jax.random.key(seed)
    ks = iter(jax.random.split(key, 40))
    f32 = jnp.float32
    L = DEPTH
    D = D_MODEL

    def nrm(shape, scale):
        return jax.random.normal(next(ks), shape, f32) * scale

    def gain(shape):
        return 1.0 + nrm(shape, 0.05)

    x = nrm((BATCH, SEQ, D), 1.0)
    c = nrm((BATCH, D), 1.0)
    dt = jnp.exp(jax.random.uniform(next(ks), (L, DN_HEADS), f32, math.log(1e-3), math.log(1e-1)))
    dt_bias = dt + jnp.log(-jnp.expm1(-dt))
    a_log = jnp.log(jax.random.uniform(next(ks), (L, DN_HEADS), f32, 1.0, 16.0))
    return {
        'x': x,
        'c': c,
        'ada_w': nrm((L, D, 6 * D), 0.3 * D ** -0.5),
        'ada_b': nrm((L, 6 * D), 0.01),
        'mix_pre_norm': gain((L, D)),
        'mix_post_norm': gain((L, D)),
        'w_in': nrm((L, D, IN_COLS), D ** -0.5),
        'conv_w': nrm((L, CONV_WIDTH, 3 * DN_WIDTH), CONV_WIDTH ** -0.5),
        'a_log': a_log,
        'dt_bias': dt_bias,
        'dn_norm_w': gain((L, DN_HEAD_DIM)),
        'sg_ln_w': gain((L, SG_WIDTH)),
        'sg_ln_b': nrm((L, SG_WIDTH), 0.01),
        'sg_w': nrm((L, SG_GROUPS, SG_CHUNK, SG_CHUNK), SG_CHUNK ** -0.5),
        'sg_b': gain((L, SG_GROUPS, SG_CHUNK)),
        'w_branch_gate': nrm((L, D, 2 * D), D ** -0.5),
        'b_branch_gate': nrm((L, 2 * D), 0.01),
        'w_proj_dn': nrm((L, DN_WIDTH, D), DN_WIDTH ** -0.5),
        'w_proj_sg': nrm((L, SG_WIDTH, D), SG_WIDTH ** -0.5),
        'w_out': nrm((L, D, D), D ** -0.5),
        'ffn_pre_norm': gain((L, D)),
        'ffn_post_norm': gain((L, D)),
        'w_router': nrm((L, D, N_EXPERTS), D ** -0.5),
        'router_bias': nrm((L, N_EXPERTS), 0.01),
        'w_exp_gate': nrm((L, N_EXPERTS, D, EXPERT_HIDDEN), D ** -0.5),
        'w_exp_up': nrm((L, N_EXPERTS, D, EXPERT_HIDDEN), D ** -0.5),
        'w_exp_down': nrm((L, N_EXPERTS, EXPERT_HIDDEN, D), EXPERT_HIDDEN ** -0.5),
        'w_sh_gate': nrm((L, D, SHARED_HIDDEN), D ** -0.5),
        'w_sh_up': nrm((L, D, SHARED_HIDDEN), D ** -0.5),
        'w_sh_down': nrm((L, SHARED_HIDDEN, D), SHARED_HIDDEN ** -0.5),
    }


def reference(x, c, ada_w, ada_b, mix_pre_norm, mix_post_norm, w_in, conv_w, a_log, dt_bias, dn_norm_w, sg_ln_w, sg_ln_b, sg_w, sg_b, w_branch_gate, b_branch_gate, w_proj_dn, w_proj_sg, w_out, ffn_pre_norm, ffn_post_norm, w_router, router_bias, w_exp_gate, w_exp_up, w_exp_down, w_sh_gate, w_sh_up, w_sh_down):
    cs = jax.nn.silu(c)
    for l in range(DEPTH):
        mod = (cs @ ada_w[l] + ada_b[l])[:, None, :]
        sh1, sc1, g1, sh2, sc2, g2 = jnp.split(mod, 6, axis=-1)
        hm = _rmsnorm(x, mix_pre_norm[l]) * (1.0 + sc1) + sh1
        y = _token_mixer(hm, w_in[l], conv_w[l], a_log[l], dt_bias[l], dn_norm_w[l], sg_ln_w[l], sg_ln_b[l], sg_w[l], sg_b[l], w_branch_gate[l], b_branch_gate[l], w_proj_dn[l], w_proj_sg[l], w_out[l])
        x = x + g1 * _rmsnorm(y, mix_post_norm[l])
        hf = _rmsnorm(x, ffn_pre_norm[l]) * (1.0 + sc2) + sh2
        y = _moe(hf, w_router[l], router_bias[l], w_exp_gate[l], w_exp_up[l], w_exp_down[l], w_sh_gate[l], w_sh_up[l], w_sh_down[l])
        x = x + g2 * _rmsnorm(y, ffn_post_norm[l])
    return x
```

```python
import functools

import jax
import jax.numpy as jnp
from jax import lax
from jax.experimental import pallas as pl
from jax.experimental.pallas import tpu as pltpu

F32 = jnp.float32
BF16 = jnp.bfloat16
I32 = jnp.int32

D_MODEL = 1024
DN_HEADS = 4
DN_HEAD_DIM = 128
DN_WIDTH = DN_HEADS * DN_HEAD_DIM
DN_CHUNK = 64
CONV_WIDTH = 4
SG_GROUPS = 4
SG_GROUP_DIM = 128
SG_WIDTH = SG_GROUPS * SG_GROUP_DIM
SG_CHUNK = 128
N_EXPERTS = 256
N_EXPERT_GROUPS = 8
GROUP_SIZE = N_EXPERTS // N_EXPERT_GROUPS
TOPK_GROUPS = 4
TOP_K = 8
EXPERT_HIDDEN = 256
ROUTED_SCALE = 2.5
MOE_BLOCK = 128
EPS = 1e-6

LANES = 128
SUBLANES = 8
VMEM_LIMIT = 56 * 1024 * 1024
NEG_INF = float("-inf")


def _dot(a, b):
    return jnp.dot(a, b, preferred_element_type=F32)


def _dot_nt(a, b):
    return lax.dot_general(a, b, (((1,), (1,)), ((), ())), preferred_element_type=F32)


def _dot_tn(a, b):
    return lax.dot_general(a, b, (((0,), (0,)), ((), ())), preferred_element_type=F32)


def _split2(x):
    hi = x.astype(BF16)
    lo = (x - hi.astype(F32)).astype(BF16)
    return hi, lo


def _split3(x):
    hi = x.astype(BF16)
    r = x - hi.astype(F32)
    mid = r.astype(BF16)
    lo = (r - mid.astype(F32)).astype(BF16)
    return hi, mid, lo


def _dot_hp(a, b, dot=_dot):
    ah, al = _split2(a)
    bh, bl = _split2(b)
    return dot(ah, bh) + (dot(ah, bl) + dot(al, bh))


def _sigmoid(x):
    return 1.0 / (1.0 + jnp.exp(-x))


def _silu(x):
    return x * _sigmoid(x)


def _gelu(x):
    return 0.5 * x * (1.0 + lax.erf(x * (2.0 ** -0.5)))


def _softplus(x):
    return jnp.maximum(x, 0.0) + jnp.log1p(jnp.exp(-jnp.abs(x)))


def _rms(x):
    return x * lax.rsqrt(jnp.mean(x * x, axis=-1, keepdims=True) + EPS)


def _params(*sem):
    return pltpu.CompilerParams(dimension_semantics=sem, vmem_limit_bytes=VMEM_LIMIT)


def _const_spec(shape):
    nd = len(shape)
    return pl.BlockSpec(shape, lambda *_: (0,) * nd)


def _ada_kernel(c_ref, w_ref, b_ref, o_ref):
    cs = _silu(c_ref[...])
    o_ref[...] = _dot_hp(cs, w_ref[...]) + b_ref[...]


def _ada(c, ada_w, ada_b):
    b, d = c.shape
    n = ada_w.shape[1]
    tn = d
    return pl.pallas_call(
        _ada_kernel,
        grid=(n // tn,),
        in_specs=[_const_spec((b, d)), pl.BlockSpec((d, tn), lambda j: (0, j)), pl.BlockSpec((1, tn), lambda j: (0, j))],
        out_specs=pl.BlockSpec((b, tn), lambda j: (0, j)),
        out_shape=jax.ShapeDtypeStruct((b, n), F32),
        compiler_params=_params("arbitrary"),
        name="ada",
    )(c, ada_w, ada_b.reshape(1, n))


W1_COLS = 4 * DN_WIDTH + LANES


def _in_kernel(x_ref, mod_ref, pn_ref, w1_ref, wuv_ref, lnw_ref, lnb_ref, sgw_ref, sgbt_ref,
               qkv_ref, z_ref, ab_ref, ysg_ref):
    tm = x_ref.shape[0]
    mod = mod_ref[0]
    hm = _rms(x_ref[...]) * pn_ref[...] * (1.0 + mod[1:2, :]) + mod[0:1, :]
    hb = hm.astype(BF16)
    p1 = _dot(hb, w1_ref[...])
    qkv_ref[...] = p1[:, :3 * DN_WIDTH]
    z_ref[...] = p1[:, 3 * DN_WIDTH:4 * DN_WIDTH]
    ab_ref[...] = p1[:, 4 * DN_WIDTH:]
    uv = _dot(hb, wuv_ref[...])
    u = _gelu(uv[:, :SG_WIDTH])
    vg = _gelu(uv[:, SG_WIDTH:])
    mu = jnp.mean(vg, axis=-1, keepdims=True)
    dv = vg - mu
    var = jnp.mean(dv * dv, axis=-1, keepdims=True)
    vgn = (dv * lax.rsqrt(var + EPS) * lnw_ref[...] + lnb_ref[...]).astype(BF16)
    row = lax.broadcasted_iota(I32, (SG_CHUNK, SG_CHUNK), 0)
    col = lax.broadcasted_iota(I32, (SG_CHUNK, SG_CHUNK), 1)
    tril = row >= col
    for g in range(SG_GROUPS):
        wg = jnp.where(tril, sgw_ref[g], 0.0).astype(BF16)
        bg = sgbt_ref[:, g:g + 1]
        cs = slice(g * SG_GROUP_DIM, (g + 1) * SG_GROUP_DIM)
        for n in range(tm // SG_CHUNK):
            rs = slice(n * SG_CHUNK, (n + 1) * SG_CHUNK)
            mixed = _dot(wg, vgn[rs, cs]) + bg
            ysg_ref[rs, cs] = (u[rs, cs] * mixed).astype(BF16)


def _in_proj(x2, mod3, pre_norm, w1, wuv, sg_ln_w, sg_ln_b, sg_w, sg_bt, seq, tm):
    t, d = x2.shape
    tiles_per_batch = seq // tm
    tok = lambda i: (i, 0)
    return pl.pallas_call(
        _in_kernel,
        grid=(t // tm,),
        in_specs=[
            pl.BlockSpec((tm, d), tok),
            pl.BlockSpec((1, 6, d), lambda i: (i // tiles_per_batch, 0, 0)),
            _const_spec((1, d)),
            _const_spec(w1.shape),
            _const_spec(wuv.shape),
            _const_spec((1, SG_WIDTH)),
            _const_spec((1, SG_WIDTH)),
            _const_spec(sg_w.shape),
            _const_spec(sg_bt.shape),
        ],
        out_specs=[
            pl.BlockSpec((tm, 3 * DN_WIDTH), tok),
            pl.BlockSpec((tm, DN_WIDTH), tok),
            pl.BlockSpec((tm, LANES), tok),
            pl.BlockSpec((tm, SG_WIDTH), tok),
        ],
        out_shape=[
            jax.ShapeDtypeStruct((t, 3 * DN_WIDTH), F32),
            jax.ShapeDtypeStruct((t, DN_WIDTH), F32),
            jax.ShapeDtypeStruct((t, LANES), F32),
            jax.ShapeDtypeStruct((t, SG_WIDTH), BF16),
        ],
        compiler_params=_params("arbitrary"),
        name="in_proj",
    )(x2, mod3, pre_norm, w1, wuv, sg_ln_w, sg_ln_b, sg_w, sg_bt)


def _unit_lower_inverse(a):
    c = a.shape[0]
    i = lax.broadcasted_iota(I32, (c, c), 0)
    j = lax.broadcasted_iota(I32, (c, c), 1)
    eye = (i == j).astype(F32)
    d = eye - jnp.where((i == j + 1) & ((i & 1) == 1), a, 0.0)
    b = 2
    while b < c:
        shift = b.bit_length()
        off = ((i >> shift) == (j >> shift)) & ((i & b) != 0) & ((j & b) == 0)
        a_off = jnp.where(off, a, 0.0)
        d = d - _dot_hp(_dot_hp(d, a_off), d)
        b *= 2
    return d


def _dn_kernel(qkv_ref, z_ref, ab_ref, convw_ref, alog_ref, dtb_ref, nw_ref, y_ref, carry_ref, state_ref):
    nb, c = qkv_ref.shape[0], qkv_ref.shape[1]
    hd = DN_HEAD_DIM

    @pl.when(pl.program_id(0) == 0)
    def _():
        carry_ref[...] = jnp.zeros_like(carry_ref)
        state_ref[...] = jnp.zeros_like(state_ref)

    ab = ab_ref[...].reshape(nb * c, LANES)
    g = -jnp.exp(alog_ref[...]) * _softplus(ab + dtb_ref[...])
    beta_all = _sigmoid(ab)
    ri = lax.broadcasted_iota(I32, (nb * c, nb * c), 0)
    ci = lax.broadcasted_iota(I32, (nb * c, nb * c), 1)
    tri = ((ri >= ci) & ((ri // c) == (ci // c))).astype(BF16)
    gh, gm, gl = _split3(g)
    gc = _dot(tri, gh) + (_dot(tri, gm) + _dot(tri, gl))
    gct = gc.T

    i = lax.broadcasted_iota(I32, (c, c), 0)
    j = lax.broadcasted_iota(I32, (c, c), 1)
    causal = i >= j
    strict = i > j
    row8 = lax.broadcasted_iota(I32, (SUBLANES, 3 * DN_WIDTH), 0)

    for b in range(nb):
        xc = qkv_ref[b]
        prev = carry_ref[b]
        acc = xc * convw_ref[CONV_WIDTH - 1:CONV_WIDTH, :]
        for s in range(1, CONV_WIDTH):
            rolled = pltpu.roll(xc, s, axis=0)
            top = jnp.where(row8 < s, pltpu.roll(prev, s, axis=0), rolled[:SUBLANES])
            shifted = jnp.concatenate([top, rolled[SUBLANES:]], axis=0)
            acc = acc + shifted * convw_ref[CONV_WIDTH - 1 - s:CONV_WIDTH - s, :]
        carry_ref[b] = xc[c - SUBLANES:]
        act = _silu(acc)

        rs = slice(b * c, (b + 1) * c)
        gcb = gc[rs]
        g_last = gcb[c - 1:c, :]
        e_gc = jnp.exp(gcb)
        e_rem = jnp.exp(g_last - gcb)
        e_last = jnp.exp(g_last)
        beta_b = beta_all[rs]
        zb = z_ref[b]
        for h in range(DN_HEADS):
            q = act[:, h * hd:(h + 1) * hd]
            k = act[:, DN_WIDTH + h * hd:DN_WIDTH + (h + 1) * hd]
            v = act[:, 2 * DN_WIDTH + h * hd:2 * DN_WIDTH + (h + 1) * hd]
            qn = q * lax.rsqrt(jnp.sum(q * q, axis=-1, keepdims=True) + EPS) * (hd ** -0.5)
            kn = k * lax.rsqrt(jnp.sum(k * k, axis=-1, keepdims=True) + EPS)
            beta = beta_b[:, DN_HEADS + h:DN_HEADS + h + 1]
            diff = gcb[:, h:h + 1] - gct[h:h + 1, rs]
            decay = jnp.where(causal, jnp.exp(jnp.where(causal, diff, 0.0)), 0.0)
            kb = kn * beta
            knb = kn.astype(BF16)
            lmat = jnp.where(strict, _dot_nt(kb.astype(BF16), knb) * decay, 0.0)
            tinv = _unit_lower_inverse(lmat)
            rhs = jnp.concatenate([v * beta, kb * e_gc[:, h:h + 1]], axis=1)
            sol = _dot_hp(tinv, rhs)
            u_c = sol[:, :hd]
            w_c = sol[:, hd:]
            qk = jnp.where(causal, _dot_nt(qn.astype(BF16), knb) * decay, 0.0)
            q_dec = qn * e_gc[:, h:h + 1]
            k_dec = kn * e_rem[:, h:h + 1]
            state = state_ref[b * DN_HEADS + h]
            sb = state.astype(BF16)
            ws = _dot(jnp.concatenate([w_c, q_dec], axis=0).astype(BF16), sb)
            v_new = u_c - ws[:c]
            vnb = v_new.astype(BF16)
            o = ws[c:] + _dot(qk.astype(BF16), vnb)
            state_ref[b * DN_HEADS + h] = state * e_last[:, h:h + 1] + _dot_tn(k_dec.astype(BF16), vnb)
            zh = zb[:, h * hd:(h + 1) * hd]
            y_ref[b, :, h * hd:(h + 1) * hd] = (_rms(o) * nw_ref[...] * _silu(zh)).astype(BF16)


def _delta_net(qkv3, z3, ab3, conv_w, alog_row, dtb_row, dn_norm_w):
    nb, s, _ = qkv3.shape
    c = DN_CHUNK
    blk = lambda n: (0, n, 0)
    return pl.pallas_call(
        _dn_kernel,
        grid=(s // c,),
        in_specs=[
            pl.BlockSpec((nb, c, 3 * DN_WIDTH), blk),
            pl.BlockSpec((nb, c, DN_WIDTH), blk),
            pl.BlockSpec((nb, c, LANES), blk),
            _const_spec(conv_w.shape),
            _const_spec((1, LANES)),
            _const_spec((1, LANES)),
            _const_spec((1, DN_HEAD_DIM)),
        ],
        out_specs=pl.BlockSpec((nb, c, DN_WIDTH), blk),
        out_shape=jax.ShapeDtypeStruct((nb, s, DN_WIDTH), BF16),
        scratch_shapes=[
            pltpu.VMEM((nb, SUBLANES, 3 * DN_WIDTH), F32),
            pltpu.VMEM((nb * DN_HEADS, DN_HEAD_DIM, DN_HEAD_DIM), F32),
        ],
        compiler_params=_params("arbitrary"),
        name="delta_net",
    )(qkv3, z3, ab3, conv_w, alog_row, dtb_row, dn_norm_w)


def _mix_kernel(x_ref, ydn_ref, ysg_ref, mod_ref, pn_ref, postn_ref, fpn_ref,
                wbg_ref, bbg_ref, wpd_ref, wps_ref, wout_ref, wrh_ref, wrl_ref, wsg_ref, wsu_ref, wsd_ref,
                x1_ref, hf_ref, sct_ref, sh_ref):
    d = x_ref.shape[1]
    x = x_ref[...]
    mod = mod_ref[0]
    hm = _rms(x) * pn_ref[...] * (1.0 + mod[1:2, :]) + mod[0:1, :]
    gates = _sigmoid(_dot(hm.astype(BF16), wbg_ref[...]) + bbg_ref[...])
    merged = gates[:, :d] * _dot(ydn_ref[...], wpd_ref[...]) + gates[:, d:] * _dot(ysg_ref[...], wps_ref[...])
    y = _dot(merged.astype(BF16), wout_ref[...])
    x1 = x + mod[2:3, :] * (_rms(y) * postn_ref[...])
    x1_ref[...] = x1
    hf = _rms(x1) * fpn_ref[...] * (1.0 + mod[4:5, :]) + mod[3:4, :]
    hf_ref[...] = hf
    hh, hl = _split2(hf)
    logits_t = _dot_nt(wrh_ref[...], hh) + (_dot_nt(wrh_ref[...], hl) + _dot_nt(wrl_ref[...], hh))
    sct_ref[...] = _sigmoid(logits_t)
    hid = _silu(_dot(hh, wsg_ref[...])) * _dot(hh, wsu_ref[...])
    sh_ref[...] = _dot(hid.astype(BF16), wsd_ref[...])


def _mix(x2, ydn, ysg, mod3, pre_norm, post_norm, ffn_pre_norm, wbg, bbg, wpd, wps, wout, wrh, wrl, wsg, wsu, wsd, seq, tm):
    t, d = x2.shape
    tiles_per_batch = seq // tm
    tok = lambda i: (i, 0)
    consts = [pre_norm, post_norm, ffn_pre_norm, wbg, bbg, wpd, wps, wout, wrh, wrl, wsg, wsu, wsd]
    return pl.pallas_call(
        _mix_kernel,
        grid=(t // tm,),
        in_specs=[
            pl.BlockSpec((tm, d), tok),
            pl.BlockSpec((tm, DN_WIDTH), tok),
            pl.BlockSpec((tm, SG_WIDTH), tok),
            pl.BlockSpec((1, 6, d), lambda i: (i // tiles_per_batch, 0, 0)),
        ] + [_const_spec(a.shape) for a in consts],
        out_specs=[
            pl.BlockSpec((tm, d), tok),
            pl.BlockSpec((tm, d), tok),
            pl.BlockSpec((N_EXPERTS, tm), lambda i: (0, i)),
            pl.BlockSpec((tm, d), tok),
        ],
        out_shape=[
            jax.ShapeDtypeStruct((t, d), F32),
            jax.ShapeDtypeStruct((t, d), F32),
            jax.ShapeDtypeStruct((N_EXPERTS, t), F32),
            jax.ShapeDtypeStruct((t, d), F32),
        ],
        compiler_params=_params("arbitrary"),
        name="mix",
    )(x2, ydn, ysg, mod3, *consts)


def _first_argmax(vals, idx):
    m = jnp.max(vals, axis=0, keepdims=True)
    first = jnp.min(jnp.where(vals == m, idx, jnp.int32(2 ** 30)), axis=0, keepdims=True)
    return m, first


def _route_kernel(sct_ref, bias_ref, idx_ref, wtt_ref, rank_ref, cnt_ref, carry_ref):
    tm = sct_ref.shape[1]

    @pl.when(pl.program_id(0) == 0)
    def _():
        carry_ref[...] = jnp.zeros_like(carry_ref)

    scores = sct_ref[...]
    sel = scores + bias_ref[...]
    erow = lax.broadcasted_iota(I32, (N_EXPERTS, tm), 0)
    grow = lax.broadcasted_iota(I32, (GROUP_SIZE, tm), 0)

    gs = []
    for gidx in range(N_EXPERT_GROUPS):
        sg = sel[gidx * GROUP_SIZE:(gidx + 1) * GROUP_SIZE]
        m1, first = _first_argmax(sg, grow)
        m2 = jnp.max(jnp.where(grow == first, NEG_INF, sg), axis=0, keepdims=True)
        gs.append(m1 + m2)
    gsc = jnp.concatenate(gs, axis=0)
    giota = lax.broadcasted_iota(I32, (N_EXPERT_GROUPS, tm), 0)
    gmask = jnp.zeros((N_EXPERT_GROUPS, tm), F32)
    cur = gsc
    for _ in range(TOPK_GROUPS):
        _, gi = _first_argmax(cur, giota)
        pick = giota == gi
        gmask = jnp.where(pick, 1.0, gmask)
        cur = jnp.where(pick, NEG_INF, cur)
    masked = jnp.concatenate(
        [jnp.where(gmask[gidx:gidx + 1, :] > 0.5, sel[gidx * GROUP_SIZE:(gidx + 1) * GROUP_SIZE], NEG_INF)
         for gidx in range(N_EXPERT_GROUPS)], axis=0)

    cur = masked
    idxs, wts = [], []
    onehot = jnp.zeros((N_EXPERTS, tm), F32)
    for _ in range(TOP_K):
        _, ei = _first_argmax(cur, erow)
        pick = erow == ei
        idxs.append(ei)
        wts.append(jnp.sum(jnp.where(pick, scores, 0.0), axis=0, keepdims=True))
        onehot = jnp.where(pick, 1.0, onehot)
        cur = jnp.where(pick, NEG_INF, cur)
    idx = jnp.concatenate(idxs, axis=0)
    wt = jnp.concatenate(wts, axis=0)
    wt = wt / jnp.sum(wt, axis=0, keepdims=True) * ROUTED_SCALE
    idx_ref[...] = idx
    wpad = jnp.concatenate([wt, jnp.zeros((LANES - TOP_K, tm), F32)], axis=0)
    wtt_ref[...] = wpad.T

    ti = lax.broadcasted_iota(I32, (tm, tm), 0)
    tj = lax.broadcasted_iota(I32, (tm, tm), 1)
    upper = (ti < tj).astype(BF16)
    before = _dot(onehot.astype(BF16), upper) + carry_ref[...]
    rank_ref[...] = jnp.concatenate(
        [jnp.sum(jnp.where(erow == idxs[kk], before, 0.0), axis=0, keepdims=True) for kk in range(TOP_K)],
        axis=0).astype(I32)
    total = carry_ref[...] + jnp.sum(onehot, axis=1, keepdims=True)
    carry_ref[...] = total
    cnt_ref[...] = total


def _route(sct, bias_col, tm):
    e, t = sct.shape
    tile = lambda i: (0, i)
    return pl.pallas_call(
        _route_kernel,
        grid=(t // tm,),
        in_specs=[pl.BlockSpec((e, tm), tile), _const_spec((e, 1))],
        out_specs=[
            pl.BlockSpec((TOP_K, tm), tile),
            pl.BlockSpec((tm, LANES), lambda i: (i, 0)),
            pl.BlockSpec((TOP_K, tm), tile),
            _const_spec((e, 1)),
        ],
        out_shape=[
            jax.ShapeDtypeStruct((TOP_K, t), I32),
            jax.ShapeDtypeStruct((t, LANES), F32),
            jax.ShapeDtypeStruct((TOP_K, t), I32),
            jax.ShapeDtypeStruct((e, 1), F32),
        ],
        scratch_shapes=[pltpu.VMEM((e, 1), F32)],
        compiler_params=_params("arbitrary"),
        name="route",
    )(sct, bias_col)


def _dest_kernel(cnt_ref, idx_ref, rank_ref, dest_ref, blke_ref):
    tm = idx_ref.shape[1]
    nblk = blke_ref.shape[1]
    e = N_EXPERTS
    cnt = cnt_ref[...]
    padded = jnp.floor((cnt + (MOE_BLOCK - 1)) * (1.0 / MOE_BLOCK)) * MOE_BLOCK
    pw = jnp.broadcast_to(padded, (e, LANES))
    ri = lax.broadcasted_iota(I32, (e, e), 0)
    ci = lax.broadcasted_iota(I32, (e, e), 1)
    lower = (ri >= ci).astype(BF16)
    ph, pm, plo = _split3(pw)
    pends = _dot(lower, ph) + (_dot(lower, pm) + _dot(lower, plo))
    pstart = (pends - pw)[:, 0:1]
    erow = lax.broadcasted_iota(I32, (e, tm), 0)
    idx = idx_ref[...]
    rows = [jnp.sum(jnp.where(erow == idx[kk:kk + 1, :], pstart, 0.0), axis=0, keepdims=True) for kk in range(TOP_K)]
    dest_ref[...] = jnp.concatenate(rows, axis=0).astype(I32) + rank_ref[...]
    bstart = (lax.broadcasted_iota(I32, (e, nblk), 1) * MOE_BLOCK).astype(F32)
    below = jnp.sum(jnp.where(pends[:, 0:1] <= bstart, 1.0, 0.0), axis=0, keepdims=True)
    blke_ref[...] = jnp.minimum(below, float(e - 1)).astype(I32)


def _dest(cnt, idx, rank, nblk, tm):
    e = cnt.shape[0]
    t = idx.shape[1]
    tile = lambda i: (0, i)
    return pl.pallas_call(
        _dest_kernel,
        grid=(t // tm,),
        in_specs=[_const_spec((e, 1)), pl.BlockSpec((TOP_K, tm), tile), pl.BlockSpec((TOP_K, tm), tile)],
        out_specs=[pl.BlockSpec((TOP_K, tm), tile), _const_spec((1, nblk))],
        out_shape=[jax.ShapeDtypeStruct((TOP_K, t), I32), jax.ShapeDtypeStruct((1, nblk), I32)],
        compiler_params=_params("arbitrary"),
        name="dest",
    )(cnt, idx, rank)


def _row_copy(src, dst, sem):
    return pltpu.make_async_copy(src, dst, sem)


def _scatter_kernel(dest_hbm, hf_ref, xs_in, xs_out, dest_smem, sem_idx, sem_rows):
    del xs_in
    tm = hf_ref.shape[0]
    i = pl.program_id(0)
    idx_copy = pltpu.make_async_copy(dest_hbm.at[i], dest_smem, sem_idx)
    idx_copy.start()
    idx_copy.wait()

    def issue(tok, carry):
        for kk in range(TOP_K):
            slot = dest_smem[kk * tm + tok]
            _row_copy(hf_ref.at[pl.ds(tok, 1), :], xs_out.at[pl.ds(slot, 1), :], sem_rows).start()
        return carry

    lax.fori_loop(0, tm, issue, 0)
    for kk in range(TOP_K):
        _row_copy(hf_ref, xs_out.at[pl.ds(0, tm), :], sem_rows).wait()


def _scatter(dest_tiles, hf, xs_zero, tm):
    t, d = hf.shape
    n_tiles = t // tm
    return pl.pallas_call(
        _scatter_kernel,
        grid=(n_tiles,),
        in_specs=[
            pl.BlockSpec(memory_space=pl.ANY),
            pl.BlockSpec((tm, d), lambda i: (i, 0)),
            pl.BlockSpec(memory_space=pl.ANY),
        ],
        out_specs=pl.BlockSpec(memory_space=pl.ANY),
        out_shape=jax.ShapeDtypeStruct(xs_zero.shape, xs_zero.dtype),
        scratch_shapes=[pltpu.SMEM((TOP_K * tm,), I32), pltpu.SemaphoreType.DMA, pltpu.SemaphoreType.DMA],
        input_output_aliases={2: 0},
        compiler_params=_params("arbitrary"),
        name="scatter",
    )(dest_tiles, hf, xs_zero)


def _moe_kernel(blke_ref, xs_ref, wg_ref, wu_ref, wd_ref, ys_ref, wgb_ref, wub_ref, wdb_ref):
    jb = pl.program_id(0)
    changed = jnp.logical_or(jb == 0, blke_ref[jb] != blke_ref[jnp.maximum(jb - 1, 0)])

    @pl.when(changed)
    def _():
        wgb_ref[...] = wg_ref[0].astype(BF16)
        wub_ref[...] = wu_ref[0].astype(BF16)
        wdb_ref[...] = wd_ref[0].astype(BF16)

    xb = xs_ref[...].astype(BF16)
    hid = _silu(_dot(xb, wgb_ref[...])) * _dot(xb, wub_ref[...])
    ys_ref[...] = _dot(hid.astype(BF16), wdb_ref[...])


def _moe(blk_e, xs, w_gate, w_up, w_down):
    ns, d = xs.shape
    nblk = ns // MOE_BLOCK
    eh = w_gate.shape[2]
    return pl.pallas_call(
        _moe_kernel,
        grid_spec=pltpu.PrefetchScalarGridSpec(
            num_scalar_prefetch=1,
            grid=(nblk,),
            in_specs=[
                pl.BlockSpec((MOE_BLOCK, d), lambda j, be: (j, 0)),
                pl.BlockSpec((1, d, eh), lambda j, be: (be[j], 0, 0)),
                pl.BlockSpec((1, d, eh), lambda j, be: (be[j], 0, 0)),
                pl.BlockSpec((1, eh, d), lambda j, be: (be[j], 0, 0)),
            ],
            out_specs=pl.BlockSpec((MOE_BLOCK, d), lambda j, be: (j, 0)),
            scratch_shapes=[pltpu.VMEM((d, eh), BF16), pltpu.VMEM((d, eh), BF16), pltpu.VMEM((eh, d), BF16)],
        ),
        out_shape=jax.ShapeDtypeStruct((ns, d), F32),
        compiler_params=_params("arbitrary"),
        name="moe",
    )(blk_e, xs, w_gate, w_up, w_down)


def _combine_kernel(dest_hbm, ys_hbm, wtt_ref, x1_ref, sh_ref, mod_ref, postn_ref, o_ref, dest_smem, buf_ref, sem_idx, sem_rows):
    tc = x1_ref.shape[0]
    i = pl.program_id(0)
    idx_copy = pltpu.make_async_copy(dest_hbm.at[i], dest_smem, sem_idx)
    idx_copy.start()
    idx_copy.wait()

    def issue(tok, carry):
        for kk in range(TOP_K):
            slot = dest_smem[kk * tc + tok]
            _row_copy(ys_hbm.at[pl.ds(slot, 1), :], buf_ref.at[kk, pl.ds(tok, 1), :], sem_rows).start()
        return carry

    lax.fori_loop(0, tc, issue, 0)
    for kk in range(TOP_K):
        _row_copy(ys_hbm.at[pl.ds(0, tc), :], buf_ref.at[kk], sem_rows).wait()

    wtt = wtt_ref[...]
    acc = buf_ref[0] * wtt[:, 0:1]
    for kk in range(1, TOP_K):
        acc = acc + buf_ref[kk] * wtt[:, kk:kk + 1]
    y = acc + sh_ref[...]
    mod = mod_ref[0]
    o_ref[...] = x1_ref[...] + mod[5:6, :] * (_rms(y) * postn_ref[...])


def _combine(dest_tiles, ys, wtt, x1, shared, mod3, ffn_post_norm, seq, tc):
    t, d = x1.shape
    tiles_per_batch = seq // tc
    tok = lambda i: (i, 0)
    return pl.pallas_call(
        _combine_kernel,
        grid=(t // tc,),
        in_specs=[
            pl.BlockSpec(memory_space=pl.ANY),
            pl.BlockSpec(memory_space=pl.ANY),
            pl.BlockSpec((tc, LANES), tok),
            pl.BlockSpec((tc, d), tok),
            pl.BlockSpec((tc, d), tok),
            pl.BlockSpec((1, 6, d), lambda i: (i // tiles_per_batch, 0, 0)),
            _const_spec((1, d)),
        ],
        out_specs=pl.BlockSpec((tc, d), tok),
        out_shape=jax.ShapeDtypeStruct((t, d), F32),
        scratch_shapes=[
            pltpu.SMEM((TOP_K * tc,), I32),
            pltpu.VMEM((TOP_K, tc, d), F32),
            pltpu.SemaphoreType.DMA,
            pltpu.SemaphoreType.DMA,
        ],
        compiler_params=_params("arbitrary"),
        name="combine",
    )(dest_tiles, ys, wtt, x1, shared, mod3, ffn_post_norm)


def _tile(n, want):
    t = min(n, want)
    assert n % t == 0
    return t


def _dest_tiles(dest, tile):
    k, t = dest.shape
    return dest.reshape(k, t // tile, tile).transpose(1, 0, 2).reshape(t // tile, k * tile)


def kernel(x, c, ada_w, ada_b, mix_pre_norm, mix_post_norm, w_in, conv_w, a_log, dt_bias, dn_norm_w, sg_ln_w, sg_ln_b, sg_w, sg_b, w_branch_gate, b_branch_gate, w_proj_dn, w_proj_sg, w_out, ffn_pre_norm, ffn_post_norm, w_router, router_bias, w_exp_gate, w_exp_up, w_exp_down, w_sh_gate, w_sh_up, w_sh_down):
    nb, seq, d = x.shape
    depth = ada_w.shape[0]
    t = nb * seq
    tm = _tile(seq, 512)
    tr = _tile(t, 512)
    tsc = _tile(t, 512)
    tcm = _tile(seq, 256)
    nblk = -(-t * TOP_K // MOE_BLOCK) + N_EXPERTS
    row = lambda v: v.reshape(1, -1)
    pad_lanes = lambda v: jnp.pad(v.astype(F32), (0, LANES - v.shape[0])).reshape(1, LANES)

    x2 = x.reshape(t, d)
    for l in range(depth):
        mod3 = _ada(c, ada_w[l], ada_b[l]).reshape(nb, 6, d)

        wi = w_in[l]
        qkvz, ab_cols, uv = wi[:, :4 * DN_WIDTH], wi[:, 4 * DN_WIDTH:4 * DN_WIDTH + 2 * DN_HEADS], wi[:, 4 * DN_WIDTH + 2 * DN_HEADS:]
        w1 = jnp.concatenate([qkvz, ab_cols, jnp.zeros((d, LANES - 2 * DN_HEADS), wi.dtype)], axis=1).astype(BF16)
        qkv, z, ab, ysg = _in_proj(x2, mod3, row(mix_pre_norm[l]), w1, uv.astype(BF16), row(sg_ln_w[l]), row(sg_ln_b[l]),
                                   sg_w[l], sg_b[l].T, seq, tm)

        ydn = _delta_net(qkv.reshape(nb, seq, -1), z.reshape(nb, seq, -1), ab.reshape(nb, seq, -1), conv_w[l],
                         pad_lanes(a_log[l]), pad_lanes(dt_bias[l]), row(dn_norm_w[l]))

        wr_t = w_router[l].T
        wrh = wr_t.astype(BF16)
        wrl = (wr_t - wrh.astype(F32)).astype(BF16)
        x1, hf, sct, shared = _mix(
            x2, ydn.reshape(t, -1), ysg, mod3, row(mix_pre_norm[l]), row(mix_post_norm[l]), row(ffn_pre_norm[l]),
            w_branch_gate[l].astype(BF16), row(b_branch_gate[l]), w_proj_dn[l].astype(BF16), w_proj_sg[l].astype(BF16),
            w_out[l].astype(BF16), wrh, wrl, w_sh_gate[l].astype(BF16), w_sh_up[l].astype(BF16), w_sh_down[l].astype(BF16),
            seq, tm)

        idx, wtt, rank, cnt = _route(sct, router_bias[l].reshape(-1, 1), tr)
        dest, blk_e = _dest(cnt, idx, rank, nblk, tr)
        xs = _scatter(_dest_tiles(dest, tsc), hf, jnp.zeros((nblk * MOE_BLOCK, d), F32), tsc)
        ys = _moe(blk_e.reshape(nblk), xs, w_exp_gate[l], w_exp_up[l], w_exp_down[l])
        x2 = _combine(_dest_tiles(dest, tcm), ys, wtt, x1, shared, mod3, row(ffn_post_norm[l]), seq, tcm)
    return x2.reshape(nb, seq, d)
```

```python
import functools

import jax
import jax.numpy as jnp
from jax import lax
from jax.experimental import pallas as pl
from jax.experimental.pallas import tpu as pltpu

F32 = jnp.float32
BF16 = jnp.bfloat16
I32 = jnp.int32

D_MODEL = 1024
DN_HEADS = 4
DN_HEAD_DIM = 128
DN_WIDTH = DN_HEADS * DN_HEAD_DIM
DN_CHUNK = 64
CONV_WIDTH = 4
SG_GROUPS = 4
SG_GROUP_DIM = 128
SG_WIDTH = SG_GROUPS * SG_GROUP_DIM
SG_CHUNK = 128
N_EXPERTS = 256
N_EXPERT_GROUPS = 8
GROUP_SIZE = N_EXPERTS // N_EXPERT_GROUPS
TOPK_GROUPS = 4
TOP_K = 8
EXPERT_HIDDEN = 256
ROUTED_SCALE = 2.5
MOE_BLOCK = 128
EPS = 1e-6

LANES = 128
SUBLANES = 8
VMEM_LIMIT = 56 * 1024 * 1024
NEG_INF = float("-inf")


def _dot(a, b):
    return jnp.dot(a, b, preferred_element_type=F32)


def _dot_nt(a, b):
    return lax.dot_general(a, b, (((1,), (1,)), ((), ())), preferred_element_type=F32)


def _dot_tn(a, b):
    return lax.dot_general(a, b, (((0,), (0,)), ((), ())), preferred_element_type=F32)


def _split2(x):
    hi = x.astype(BF16)
    lo = (x - hi.astype(F32)).astype(BF16)
    return hi, lo


def _split3(x):
    hi = x.astype(BF16)
    r = x - hi.astype(F32)
    mid = r.astype(BF16)
    lo = (r - mid.astype(F32)).astype(BF16)
    return hi, mid, lo


def _dot_hp(a, b, dot=_dot):
    ah, al = _split2(a)
    bh, bl = _split2(b)
    return dot(ah, bh) + (dot(ah, bl) + dot(al, bh))


def _sigmoid(x):
    return 1.0 / (1.0 + jnp.exp(-x))


def _silu(x):
    return x * _sigmoid(x)


def _gelu(x):
    return 0.5 * x * (1.0 + lax.erf(x * (2.0 ** -0.5)))


def _softplus(x):
    return jnp.maximum(x, 0.0) + jnp.log1p(jnp.exp(-jnp.abs(x)))


def _rms(x):
    return x * lax.rsqrt(jnp.mean(x * x, axis=-1, keepdims=True) + EPS)


def _params(*sem):
    return pltpu.CompilerParams(dimension_semantics=sem, vmem_limit_bytes=VMEM_LIMIT)


def _const_spec(shape):
    nd = len(shape)
    return pl.BlockSpec(shape, lambda *_: (0,) * nd)


def _ada_kernel(c_ref, w_ref, b_ref, o_ref):
    cs = _silu(c_ref[...])
    o_ref[...] = _dot_hp(cs, w_ref[...]) + b_ref[...]


def _ada(c, ada_w, ada_b):
    b, d = c.shape
    n = ada_w.shape[1]
    tn = d
    return pl.pallas_call(
        _ada_kernel,
        grid=(n // tn,),
        in_specs=[_const_spec((b, d)), pl.BlockSpec((d, tn), lambda j: (0, j)), pl.BlockSpec((1, tn), lambda j: (0, j))],
        out_specs=pl.BlockSpec((b, tn), lambda j: (0, j)),
        out_shape=jax.ShapeDtypeStruct((b, n), F32),
        compiler_params=_params("arbitrary"),
        name="ada",
    )(c, ada_w, ada_b.reshape(1, n))


W1_COLS = 4 * DN_WIDTH + LANES


def _in_kernel(x_ref, mod_ref, pn_ref, w1_ref, wuv_ref, lnw_ref, lnb_ref, sgw_ref, sgbt_ref,
               qkv_ref, z_ref, ab_ref, ysg_ref):
    tm = x_ref.shape[0]
    mod = mod_ref[0]
    hm = _rms(x_ref[...]) * pn_ref[...] * (1.0 + mod[1:2, :]) + mod[0:1, :]
    hb = hm.astype(BF16)
    p1 = _dot(hb, w1_ref[...])
    qkv_ref[...] = p1[:, :3 * DN_WIDTH]
    z_ref[...] = p1[:, 3 * DN_WIDTH:4 * DN_WIDTH]
    ab_ref[...] = p1[:, 4 * DN_WIDTH:]
    uv = _dot(hb, wuv_ref[...])
    u = _gelu(uv[:, :SG_WIDTH])
    vg = _gelu(uv[:, SG_WIDTH:])
    mu = jnp.mean(vg, axis=-1, keepdims=True)
    dv = vg - mu
    var = jnp.mean(dv * dv, axis=-1, keepdims=True)
    vgn = (dv * lax.rsqrt(var + EPS) * lnw_ref[...] + lnb_ref[...]).astype(BF16)
    row = lax.broadcasted_iota(I32, (SG_CHUNK, SG_CHUNK), 0)
    col = lax.broadcasted_iota(I32, (SG_CHUNK, SG_CHUNK), 1)
    tril = row >= col
    for g in range(SG_GROUPS):
        wg = jnp.where(tril, sgw_ref[g], 0.0).astype(BF16)
        bg = sgbt_ref[:, g:g + 1]
        cs = slice(g * SG_GROUP_DIM, (g + 1) * SG_GROUP_DIM)
        for n in range(tm // SG_CHUNK):
            rs = slice(n * SG_CHUNK, (n + 1) * SG_CHUNK)
            mixed = _dot(wg, vgn[rs, cs]) + bg
            ysg_ref[rs, cs] = (u[rs, cs] * mixed).astype(BF16)


def _in_proj(x2, mod3, pre_norm, w1, wuv, sg_ln_w, sg_ln_b, sg_w, sg_bt, seq, tm):
    t, d = x2.shape
    tiles_per_batch = seq // tm
    tok = lambda i: (i, 0)
    return pl.pallas_call(
        _in_kernel,
        grid=(t // tm,),
        in_specs=[
            pl.BlockSpec((tm, d), tok),
            pl.BlockSpec((1, 6, d), lambda i: (i // tiles_per_batch, 0, 0)),
            _const_spec((1, d)),
            _const_spec(w1.shape),
            _const_spec(wuv.shape),
            _const_spec((1, SG_WIDTH)),
            _const_spec((1, SG_WIDTH)),
            _const_spec(sg_w.shape),
            _const_spec(sg_bt.shape),
        ],
        out_specs=[
            pl.BlockSpec((tm, 3 * DN_WIDTH), tok),
            pl.BlockSpec((tm, DN_WIDTH), tok),
            pl.BlockSpec((tm, LANES), tok),
            pl.BlockSpec((tm, SG_WIDTH), tok),
        ],
        out_shape=[
            jax.ShapeDtypeStruct((t, 3 * DN_WIDTH), F32),
            jax.ShapeDtypeStruct((t, DN_WIDTH), F32),
            jax.ShapeDtypeStruct((t, LANES), F32),
            jax.ShapeDtypeStruct((t, SG_WIDTH), BF16),
        ],
        compiler_params=_params("arbitrary"),
        name="in_proj",
    )(x2, mod3, pre_norm, w1, wuv, sg_ln_w, sg_ln_b, sg_w, sg_bt)


def _hp_parts(ap, bp, dot=_dot):
    return dot(ap[0], bp[0]) + (dot(ap[0], bp[1]) + dot(ap[1], bp[0]))


def _unit_lower_inverses(a_list):
    c = a_list[0].shape[0]
    i = lax.broadcasted_iota(I32, (c, c), 0)
    j = lax.broadcasted_iota(I32, (c, c), 1)
    eye = (i == j).astype(F32)
    first = (i == j + 1) & ((i & 1) == 1)
    d_list = [eye - jnp.where(first, a, 0.0) for a in a_list]
    b = 2
    while b < c:
        shift = b.bit_length()
        off = ((i >> shift) == (j >> shift)) & ((i & b) != 0) & ((j & b) == 0)
        a_parts = [_split2(jnp.where(off, a, 0.0)) for a in a_list]
        d_parts = [_split2(d) for d in d_list]
        t_list = [_hp_parts(dp, ap) for dp, ap in zip(d_parts, a_parts)]
        t_parts = [_split2(t) for t in t_list]
        d_list = [d - _hp_parts(tp, dp) for d, tp, dp in zip(d_list, t_parts, d_parts)]
        b *= 2
    return d_list


def _dn_kernel(qkv_ref, z_ref, ab_ref, convw_ref, alog_ref, dtb_ref, nw_ref, y_ref, carry_ref, state_ref):
    nb, c = qkv_ref.shape[0], qkv_ref.shape[1]
    hd = DN_HEAD_DIM
    probs = [(b, h) for b in range(nb) for h in range(DN_HEADS)]

    @pl.when(pl.program_id(0) == 0)
    def _():
        carry_ref[...] = jnp.zeros_like(carry_ref)
        state_ref[...] = jnp.zeros_like(state_ref)

    ab = ab_ref[...].reshape(nb * c, LANES)
    g = -jnp.exp(alog_ref[...]) * _softplus(ab + dtb_ref[...])
    beta_all = _sigmoid(ab)
    ri = lax.broadcasted_iota(I32, (nb * c, nb * c), 0)
    ci = lax.broadcasted_iota(I32, (nb * c, nb * c), 1)
    tri = ((ri >= ci) & ((ri // c) == (ci // c))).astype(BF16)
    gh, gm, gl = _split3(g)
    gc = _dot(tri, gh) + (_dot(tri, gm) + _dot(tri, gl))
    gct = gc.T

    i = lax.broadcasted_iota(I32, (c, c), 0)
    j = lax.broadcasted_iota(I32, (c, c), 1)
    causal = i >= j
    strict = i > j
    row8 = lax.broadcasted_iota(I32, (SUBLANES, 3 * DN_WIDTH), 0)

    acts, gcs, e_gcs, e_rems, e_lasts = [], [], [], [], []
    for b in range(nb):
        xc = qkv_ref[b]
        prev = carry_ref[b]
        acc = xc * convw_ref[CONV_WIDTH - 1:CONV_WIDTH, :]
        for s in range(1, CONV_WIDTH):
            rolled = pltpu.roll(xc, s, axis=0)
            top = jnp.where(row8 < s, pltpu.roll(prev, s, axis=0), rolled[:SUBLANES])
            shifted = jnp.concatenate([top, rolled[SUBLANES:]], axis=0)
            acc = acc + shifted * convw_ref[CONV_WIDTH - 1 - s:CONV_WIDTH - s, :]
        carry_ref[b] = xc[c - SUBLANES:]
        acts.append(_silu(acc))
        gcb = gc[b * c:(b + 1) * c]
        g_last = gcb[c - 1:c, :]
        gcs.append(gcb)
        e_gcs.append(jnp.exp(gcb))
        e_rems.append(jnp.exp(g_last - gcb))
        e_lasts.append(jnp.exp(g_last))

    qn, kn, knb, kb, vb, decay = {}, {}, {}, {}, {}, {}
    for p in probs:
        b, h = p
        act = acts[b]
        q = act[:, h * hd:(h + 1) * hd]
        k = act[:, DN_WIDTH + h * hd:DN_WIDTH + (h + 1) * hd]
        v = act[:, 2 * DN_WIDTH + h * hd:2 * DN_WIDTH + (h + 1) * hd]
        qn[p] = q * lax.rsqrt(jnp.sum(q * q, axis=-1, keepdims=True) + EPS) * (hd ** -0.5)
        kn[p] = k * lax.rsqrt(jnp.sum(k * k, axis=-1, keepdims=True) + EPS)
        beta = beta_all[b * c:(b + 1) * c, DN_HEADS + h:DN_HEADS + h + 1]
        diff = gcs[b][:, h:h + 1] - gct[h:h + 1, b * c:(b + 1) * c]
        decay[p] = jnp.where(causal, jnp.exp(jnp.where(causal, diff, 0.0)), 0.0)
        kb[p] = kn[p] * beta
        vb[p] = v * beta
        knb[p] = kn[p].astype(BF16)

    kk = {p: _dot_nt(kb[p].astype(BF16), knb[p]) for p in probs}
    qk = {p: _dot_nt(qn[p].astype(BF16), knb[p]) for p in probs}
    tinv = dict(zip(probs, _unit_lower_inverses([jnp.where(strict, kk[p] * decay[p], 0.0) for p in probs])))
    t_parts = {p: _split2(tinv[p]) for p in probs}
    rhs = {p: jnp.concatenate([vb[p], kb[p] * e_gcs[p[0]][:, p[1]:p[1] + 1]], axis=1) for p in probs}
    r_parts = {p: _split2(rhs[p]) for p in probs}
    sol = {p: _hp_parts(t_parts[p], r_parts[p]) for p in probs}

    lhs = {p: jnp.concatenate([sol[p][:, hd:], qn[p] * e_gcs[p[0]][:, p[1]:p[1] + 1]], axis=0).astype(BF16) for p in probs}
    state = {p: state_ref[p[0] * DN_HEADS + p[1]] for p in probs}
    ws = {p: _dot(lhs[p], state[p].astype(BF16)) for p in probs}
    vnb = {p: (sol[p][:, :hd] - ws[p][:c]).astype(BF16) for p in probs}
    qkm = {p: jnp.where(causal, qk[p] * decay[p], 0.0).astype(BF16) for p in probs}
    o = {p: ws[p][c:] + _dot(qkm[p], vnb[p]) for p in probs}
    kd = {p: (kn[p] * e_rems[p[0]][:, p[1]:p[1] + 1]).astype(BF16) for p in probs}
    upd = {p: _dot_tn(kd[p], vnb[p]) for p in probs}
    for p in probs:
        b, h = p
        state_ref[b * DN_HEADS + h] = state[p] * e_lasts[b][:, h:h + 1] + upd[p]
        zh = z_ref[b, :, h * hd:(h + 1) * hd]
        y_ref[b, :, h * hd:(h + 1) * hd] = (_rms(o[p]) * nw_ref[...] * _silu(zh)).astype(BF16)


def _delta_net(qkv3, z3, ab3, conv_w, alog_row, dtb_row, dn_norm_w):
    nb, s, _ = qkv3.shape
    c = DN_CHUNK
    blk = lambda n: (0, n, 0)
    return pl.pallas_call(
        _dn_kernel,
        grid=(s // c,),
        in_specs=[
            pl.BlockSpec((nb, c, 3 * DN_WIDTH), blk),
            pl.BlockSpec((nb, c, DN_WIDTH), blk),
            pl.BlockSpec((nb, c, LANES), blk),
            _const_spec(conv_w.shape),
            _const_spec((1, LANES)),
            _const_spec((1, LANES)),
            _const_spec((1, DN_HEAD_DIM)),
        ],
        out_specs=pl.BlockSpec((nb, c, DN_WIDTH), blk),
        out_shape=jax.ShapeDtypeStruct((nb, s, DN_WIDTH), BF16),
        scratch_shapes=[
            pltpu.VMEM((nb, SUBLANES, 3 * DN_WIDTH), F32),
            pltpu.VMEM((nb * DN_HEADS, DN_HEAD_DIM, DN_HEAD_DIM), F32),
        ],
        compiler_params=_params("arbitrary"),
        name="delta_net",
    )(qkv3, z3, ab3, conv_w, alog_row, dtb_row, dn_norm_w)


def _mix_kernel(x_ref, ydn_ref, ysg_ref, mod_ref, pn_ref, postn_ref, fpn_ref,
                wbg_ref, bbg_ref, wpd_ref, wps_ref, wout_ref, wrh_ref, wrl_ref, wsg_ref, wsu_ref, wsd_ref,
                x1_ref, hf_ref, sct_ref, sh_ref):
    d = x_ref.shape[1]
    x = x_ref[...]
    mod = mod_ref[0]
    hm = _rms(x) * pn_ref[...] * (1.0 + mod[1:2, :]) + mod[0:1, :]
    gates = _sigmoid(_dot(hm.astype(BF16), wbg_ref[...]) + bbg_ref[...])
    merged = gates[:, :d] * _dot(ydn_ref[...], wpd_ref[...]) + gates[:, d:] * _dot(ysg_ref[...], wps_ref[...])
    y = _dot(merged.astype(BF16), wout_ref[...])
    x1 = x + mod[2:3, :] * (_rms(y) * postn_ref[...])
    x1_ref[...] = x1
    hf = _rms(x1) * fpn_ref[...] * (1.0 + mod[4:5, :]) + mod[3:4, :]
    hf_ref[...] = hf
    hh, hl = _split2(hf)
    logits_t = _dot_nt(wrh_ref[...], hh) + (_dot_nt(wrh_ref[...], hl) + _dot_nt(wrl_ref[...], hh))
    sct_ref[...] = _sigmoid(logits_t)
    hid = _silu(_dot(hh, wsg_ref[...])) * _dot(hh, wsu_ref[...])
    sh_ref[...] = _dot(hid.astype(BF16), wsd_ref[...])


def _mix(x2, ydn, ysg, mod3, pre_norm, post_norm, ffn_pre_norm, wbg, bbg, wpd, wps, wout, wrh, wrl, wsg, wsu, wsd, seq, tm):
    t, d = x2.shape
    tiles_per_batch = seq // tm
    tok = lambda i: (i, 0)
    consts = [pre_norm, post_norm, ffn_pre_norm, wbg, bbg, wpd, wps, wout, wrh, wrl, wsg, wsu, wsd]
    return pl.pallas_call(
        _mix_kernel,
        grid=(t // tm,),
        in_specs=[
            pl.BlockSpec((tm, d), tok),
            pl.BlockSpec((tm, DN_WIDTH), tok),
            pl.BlockSpec((tm, SG_WIDTH), tok),
            pl.BlockSpec((1, 6, d), lambda i: (i // tiles_per_batch, 0, 0)),
        ] + [_const_spec(a.shape) for a in consts],
        out_specs=[
            pl.BlockSpec((tm, d), tok),
            pl.BlockSpec((tm, d), tok),
            pl.BlockSpec((N_EXPERTS, tm), lambda i: (0, i)),
            pl.BlockSpec((tm, d), tok),
        ],
        out_shape=[
            jax.ShapeDtypeStruct((t, d), F32),
            jax.ShapeDtypeStruct((t, d), F32),
            jax.ShapeDtypeStruct((N_EXPERTS, t), F32),
            jax.ShapeDtypeStruct((t, d), F32),
        ],
        compiler_params=_params("arbitrary"),
        name="mix",
    )(x2, ydn, ysg, mod3, *consts)


def _first_argmax(vals, idx):
    m = jnp.max(vals, axis=0, keepdims=True)
    first = jnp.min(jnp.where(vals == m, idx, jnp.int32(2 ** 30)), axis=0, keepdims=True)
    return m, first


def _route_kernel(sct_ref, bias_ref, idx_ref, wtt_ref, rank_ref, cnt_ref, carry_ref):
    tm = sct_ref.shape[1]

    @pl.when(pl.program_id(0) == 0)
    def _():
        carry_ref[...] = jnp.zeros_like(carry_ref)

    scores = sct_ref[...]
    sel = scores + bias_ref[...]
    erow = lax.broadcasted_iota(I32, (N_EXPERTS, tm), 0)
    grow = lax.broadcasted_iota(I32, (GROUP_SIZE, tm), 0)

    gs = []
    for gidx in range(N_EXPERT_GROUPS):
        sg = sel[gidx * GROUP_SIZE:(gidx + 1) * GROUP_SIZE]
        m1, first = _first_argmax(sg, grow)
        m2 = jnp.max(jnp.where(grow == first, NEG_INF, sg), axis=0, keepdims=True)
        gs.append(m1 + m2)
    gsc = jnp.concatenate(gs, axis=0)
    giota = lax.broadcasted_iota(I32, (N_EXPERT_GROUPS, tm), 0)
    gmask = jnp.zeros((N_EXPERT_GROUPS, tm), F32)
    cur = gsc
    for _ in range(TOPK_GROUPS):
        _, gi = _first_argmax(cur, giota)
        pick = giota == gi
        gmask = jnp.where(pick, 1.0, gmask)
        cur = jnp.where(pick, NEG_INF, cur)
    masked = jnp.concatenate(
        [jnp.where(gmask[gidx:gidx + 1, :] > 0.5, sel[gidx * GROUP_SIZE:(gidx + 1) * GROUP_SIZE], NEG_INF)
         for gidx in range(N_EXPERT_GROUPS)], axis=0)

    cur = masked
    idxs, wts = [], []
    onehot = jnp.zeros((N_EXPERTS, tm), F32)
    for _ in range(TOP_K):
        _, ei = _first_argmax(cur, erow)
        pick = erow == ei
        idxs.append(ei)
        wts.append(jnp.sum(jnp.where(pick, scores, 0.0), axis=0, keepdims=True))
        onehot = jnp.where(pick, 1.0, onehot)
        cur = jnp.where(pick, NEG_INF, cur)
    idx = jnp.concatenate(idxs, axis=0)
    wt = jnp.concatenate(wts, axis=0)
    wt = wt / jnp.sum(wt, axis=0, keepdims=True) * ROUTED_SCALE
    idx_ref[...] = idx
    wpad = jnp.concatenate([wt, jnp.zeros((LANES - TOP_K, tm), F32)], axis=0)
    wtt_ref[...] = wpad.T

    ti = lax.broadcasted_iota(I32, (tm, tm), 0)
    tj = lax.broadcasted_iota(I32, (tm, tm), 1)
    upper = (ti < tj).astype(BF16)
    before = _dot(onehot.astype(BF16), upper) + carry_ref[...]
    rank_ref[...] = jnp.concatenate(
        [jnp.sum(jnp.where(erow == idxs[kk], before, 0.0), axis=0, keepdims=True) for kk in range(TOP_K)],
        axis=0).astype(I32)
    total = carry_ref[...] + jnp.sum(onehot, axis=1, keepdims=True)
    carry_ref[...] = total
    cnt_ref[...] = total


def _route(sct, bias_col, tm):
    e, t = sct.shape
    tile = lambda i: (0, i)
    return pl.pallas_call(
        _route_kernel,
        grid=(t // tm,),
        in_specs=[pl.BlockSpec((e, tm), tile), _const_spec((e, 1))],
        out_specs=[
            pl.BlockSpec((TOP_K, tm), tile),
            pl.BlockSpec((tm, LANES), lambda i: (i, 0)),
            pl.BlockSpec((TOP_K, tm), tile),
            _const_spec((e, 1)),
        ],
        out_shape=[
            jax.ShapeDtypeStruct((TOP_K, t), I32),
            jax.ShapeDtypeStruct((t, LANES), F32),
            jax.ShapeDtypeStruct((TOP_K, t), I32),
            jax.ShapeDtypeStruct((e, 1), F32),
        ],
        scratch_shapes=[pltpu.VMEM((e, 1), F32)],
        compiler_params=_params("arbitrary"),
        name="route",
    )(sct, bias_col)


def _dest_kernel(cnt_ref, idx_ref, rank_ref, dest_ref, blke_ref):
    tm = idx_ref.shape[1]
    nblk = blke_ref.shape[1]
    e = N_EXPERTS
    cnt = cnt_ref[...]
    padded = jnp.floor((cnt + (MOE_BLOCK - 1)) * (1.0 / MOE_BLOCK)) * MOE_BLOCK
    pw = jnp.broadcast_to(padded, (e, LANES))
    ri = lax.broadcasted_iota(I32, (e, e), 0)
    ci = lax.broadcasted_iota(I32, (e, e), 1)
    lower = (ri >= ci).astype(BF16)
    ph, pm, plo = _split3(pw)
    pends = _dot(lower, ph) + (_dot(lower, pm) + _dot(lower, plo))
    pstart = (pends - pw)[:, 0:1]
    erow = lax.broadcasted_iota(I32, (e, tm), 0)
    idx = idx_ref[...]
    rows = [jnp.sum(jnp.where(erow == idx[kk:kk + 1, :], pstart, 0.0), axis=0, keepdims=True) for kk in range(TOP_K)]
    dest_ref[...] = jnp.concatenate(rows, axis=0).astype(I32) + rank_ref[...]
    bstart = (lax.broadcasted_iota(I32, (e, nblk), 1) * MOE_BLOCK).astype(F32)
    below = jnp.sum(jnp.where(pends[:, 0:1] <= bstart, 1.0, 0.0), axis=0, keepdims=True)
    blke_ref[...] = jnp.minimum(below, float(e - 1)).astype(I32)


def _dest(cnt, idx, rank, nblk, tm):
    e = cnt.shape[0]
    t = idx.shape[1]
    tile = lambda i: (0, i)
    return pl.pallas_call(
        _dest_kernel,
        grid=(t // tm,),
        in_specs=[_const_spec((e, 1)), pl.BlockSpec((TOP_K, tm), tile), pl.BlockSpec((TOP_K, tm), tile)],
        out_specs=[pl.BlockSpec((TOP_K, tm), tile), _const_spec((1, nblk))],
        out_shape=[jax.ShapeDtypeStruct((TOP_K, t), I32), jax.ShapeDtypeStruct((1, nblk), I32)],
        compiler_params=_params("arbitrary"),
        name="dest",
    )(cnt, idx, rank)


def _row_copy(src, dst, sem):
    return pltpu.make_async_copy(src, dst, sem)


def _scatter_kernel(dest_hbm, hf_ref, xs_in, xs_out, dest_smem, sem_idx, sem_rows):
    del xs_in
    tm = hf_ref.shape[0]
    i = pl.program_id(0)
    idx_copy = pltpu.make_async_copy(dest_hbm.at[i], dest_smem, sem_idx)
    idx_copy.start()
    idx_copy.wait()

    def issue(tok, carry):
        for kk in range(TOP_K):
            slot = dest_smem[kk * tm + tok]
            _row_copy(hf_ref.at[pl.ds(tok, 1), :], xs_out.at[pl.ds(slot, 1), :], sem_rows).start(priority=kk % 2)
        return carry

    lax.fori_loop(0, tm, issue, 0)
    for kk in range(TOP_K):
        _row_copy(hf_ref, xs_out.at[pl.ds(0, tm), :], sem_rows).wait()


def _scatter(dest_tiles, hf, xs_zero, tm):
    t, d = hf.shape
    n_tiles = t // tm
    return pl.pallas_call(
        _scatter_kernel,
        grid=(n_tiles,),
        in_specs=[
            pl.BlockSpec(memory_space=pl.ANY),
            pl.BlockSpec((tm, d), lambda i: (i, 0)),
            pl.BlockSpec(memory_space=pl.ANY),
        ],
        out_specs=pl.BlockSpec(memory_space=pl.ANY),
        out_shape=jax.ShapeDtypeStruct(xs_zero.shape, xs_zero.dtype),
        scratch_shapes=[pltpu.SMEM((TOP_K * tm,), I32), pltpu.SemaphoreType.DMA, pltpu.SemaphoreType.DMA],
        input_output_aliases={2: 0},
        compiler_params=_params("arbitrary"),
        name="scatter",
    )(dest_tiles, hf, xs_zero)


def _moe_kernel(blke_ref, xs_ref, wg_ref, wu_ref, wd_ref, ys_ref, wgb_ref, wub_ref, wdb_ref):
    jb = pl.program_id(0)
    changed = jnp.logical_or(jb == 0, blke_ref[jb] != blke_ref[jnp.maximum(jb - 1, 0)])

    @pl.when(changed)
    def _():
        wgb_ref[...] = wg_ref[0].astype(BF16)
        wub_ref[...] = wu_ref[0].astype(BF16)
        wdb_ref[...] = wd_ref[0].astype(BF16)

    xb = xs_ref[...].astype(BF16)
    hid = _silu(_dot(xb, wgb_ref[...])) * _dot(xb, wub_ref[...])
    ys_ref[...] = _dot(hid.astype(BF16), wdb_ref[...])


def _moe(blk_e, xs, w_gate, w_up, w_down):
    ns, d = xs.shape
    nblk = ns // MOE_BLOCK
    eh = w_gate.shape[2]
    return pl.pallas_call(
        _moe_kernel,
        grid_spec=pltpu.PrefetchScalarGridSpec(
            num_scalar_prefetch=1,
            grid=(nblk,),
            in_specs=[
                pl.BlockSpec((MOE_BLOCK, d), lambda j, be: (j, 0)),
                pl.BlockSpec((1, d, eh), lambda j, be: (be[j], 0, 0)),
                pl.BlockSpec((1, d, eh), lambda j, be: (be[j], 0, 0)),
                pl.BlockSpec((1, eh, d), lambda j, be: (be[j], 0, 0)),
            ],
            out_specs=pl.BlockSpec((MOE_BLOCK, d), lambda j, be: (j, 0)),
            scratch_shapes=[pltpu.VMEM((d, eh), BF16), pltpu.VMEM((d, eh), BF16), pltpu.VMEM((eh, d), BF16)],
        ),
        out_shape=jax.ShapeDtypeStruct((ns, d), F32),
        compiler_params=_params("arbitrary"),
        name="moe",
    )(blk_e, xs, w_gate, w_up, w_down)


def _combine_kernel(dest_hbm, ys_hbm, wtt_ref, x1_ref, sh_ref, mod_ref, postn_ref, o_ref, dest_smem, buf_ref, sem_idx, sem_rows):
    tc = x1_ref.shape[0]
    i = pl.program_id(0)
    idx_copy = pltpu.make_async_copy(dest_hbm.at[i], dest_smem, sem_idx)
    idx_copy.start()
    idx_copy.wait()

    def issue(tok, carry):
        for kk in range(TOP_K):
            slot = dest_smem[kk * tc + tok]
            _row_copy(ys_hbm.at[pl.ds(slot, 1), :], buf_ref.at[kk, pl.ds(tok, 1), :], sem_rows).start(priority=kk % 2)
        return carry

    lax.fori_loop(0, tc, issue, 0)
    for kk in range(TOP_K):
        _row_copy(ys_hbm.at[pl.ds(0, tc), :], buf_ref.at[kk], sem_rows).wait()

    wtt = wtt_ref[...]
    acc = buf_ref[0] * wtt[:, 0:1]
    for kk in range(1, TOP_K):
        acc = acc + buf_ref[kk] * wtt[:, kk:kk + 1]
    y = acc + sh_ref[...]
    mod = mod_ref[0]
    o_ref[...] = x1_ref[...] + mod[5:6, :] * (_rms(y) * postn_ref[...])


def _combine(dest_tiles, ys, wtt, x1, shared, mod3, ffn_post_norm, seq, tc):
    t, d = x1.shape
    tiles_per_batch = seq // tc
    tok = lambda i: (i, 0)
    return pl.pallas_call(
        _combine_kernel,
        grid=(t // tc,),
        in_specs=[
            pl.BlockSpec(memory_space=pl.ANY),
            pl.BlockSpec(memory_space=pl.ANY),
            pl.BlockSpec((tc, LANES), tok),
            pl.BlockSpec((tc, d), tok),
            pl.BlockSpec((tc, d), tok),
            pl.BlockSpec((1, 6, d), lambda i: (i // tiles_per_batch, 0, 0)),
            _const_spec((1, d)),
        ],
        out_specs=pl.BlockSpec((tc, d), tok),
        out_shape=jax.ShapeDtypeStruct((t, d), F32),
        scratch_shapes=[
            pltpu.SMEM((TOP_K * tc,), I32),
            pltpu.VMEM((TOP_K, tc, d), F32),
            pltpu.SemaphoreType.DMA,
            pltpu.SemaphoreType.DMA,
        ],
        compiler_params=_params("arbitrary"),
        name="combine",
    )(dest_tiles, ys, wtt, x1, shared, mod3, ffn_post_norm)


def _tile(n, want):
    t = min(n, want)
    assert n % t == 0
    return t


def _dest_tiles(dest, tile):
    k, t = dest.shape
    return dest.reshape(k, t // tile, tile).transpose(1, 0, 2).reshape(t // tile, k * tile)


def kernel(x, c, ada_w, ada_b, mix_pre_norm, mix_post_norm, w_in, conv_w, a_log, dt_bias, dn_norm_w, sg_ln_w, sg_ln_b, sg_w, sg_b, w_branch_gate, b_branch_gate, w_proj_dn, w_proj_sg, w_out, ffn_pre_norm, ffn_post_norm, w_router, router_bias, w_exp_gate, w_exp_up, w_exp_down, w_sh_gate, w_sh_up, w_sh_down):
    nb, seq, d = x.shape
    depth = ada_w.shape[0]
    t = nb * seq
    tm = _tile(seq, 512)
    tr = _tile(t, 512)
    tsc = _tile(t, 512)
    tcm = _tile(seq, 256)
    nblk = -(-t * TOP_K // MOE_BLOCK) + N_EXPERTS
    row = lambda v: v.reshape(1, -1)
    pad_lanes = lambda v: jnp.pad(v.astype(F32), (0, LANES - v.shape[0])).reshape(1, LANES)

    x2 = x.reshape(t, d)
    for l in range(depth):
        mod3 = _ada(c, ada_w[l], ada_b[l]).reshape(nb, 6, d)

        wi = w_in[l]
        qkvz, ab_cols, uv = wi[:, :4 * DN_WIDTH], wi[:, 4 * DN_WIDTH:4 * DN_WIDTH + 2 * DN_HEADS], wi[:, 4 * DN_WIDTH + 2 * DN_HEADS:]
        w1 = jnp.concatenate([qkvz, ab_cols, jnp.zeros((d, LANES - 2 * DN_HEADS), wi.dtype)], axis=1).astype(BF16)
        qkv, z, ab, ysg = _in_proj(x2, mod3, row(mix_pre_norm[l]), w1, uv.astype(BF16), row(sg_ln_w[l]), row(sg_ln_b[l]),
                                   sg_w[l], sg_b[l].T, seq, tm)

        ydn = _delta_net(qkv.reshape(nb, seq, -1), z.reshape(nb, seq, -1), ab.reshape(nb, seq, -1), conv_w[l],
                         pad_lanes(a_log[l]), pad_lanes(dt_bias[l]), row(dn_norm_w[l]))

        wr_t = w_router[l].T
        wrh = wr_t.astype(BF16)
        wrl = (wr_t - wrh.astype(F32)).astype(BF16)
        x1, hf, sct, shared = _mix(
            x2, ydn.reshape(t, -1), ysg, mod3, row(mix_pre_norm[l]), row(mix_post_norm[l]), row(ffn_pre_norm[l]),
            w_branch_gate[l].astype(BF16), row(b_branch_gate[l]), w_proj_dn[l].astype(BF16), w_proj_sg[l].astype(BF16),
            w_out[l].astype(BF16), wrh, wrl, w_sh_gate[l].astype(BF16), w_sh_up[l].astype(BF16), w_sh_down[l].astype(BF16),
            seq, tm)

        idx, wtt, rank, cnt = _route(sct, router_bias[l].reshape(-1, 1), tr)
        dest, blk_e = _dest(cnt, idx, rank, nblk, tr)
        xs = _scatter(_dest_tiles(dest, tsc), hf, jnp.zeros((nblk * MOE_BLOCK, d), F32), tsc)
        ys = _moe(blk_e.reshape(nblk), xs, w_exp_gate[l], w_exp_up[l], w_exp_down[l])
        x2 = _combine(_dest_tiles(dest, tcm), ys, wtt, x1, shared, mod3, row(ffn_post_norm[l]), seq, tcm)
    return x2.reshape(nb, seq, d)
```

```python
import functools

import jax
import jax.numpy as jnp
from jax import lax
from jax.experimental import pallas as pl
from jax.experimental.pallas import tpu as pltpu

F32 = jnp.float32
BF16 = jnp.bfloat16
I32 = jnp.int32

D_MODEL = 1024
DN_HEADS = 4
DN_HEAD_DIM = 128
DN_WIDTH = DN_HEADS * DN_HEAD_DIM
DN_CHUNK = 64
CONV_WIDTH = 4
SG_GROUPS = 4
SG_GROUP_DIM = 128
SG_WIDTH = SG_GROUPS * SG_GROUP_DIM
SG_CHUNK = 128
N_EXPERTS = 256
N_EXPERT_GROUPS = 8
GROUP_SIZE = N_EXPERTS // N_EXPERT_GROUPS
TOPK_GROUPS = 4
TOP_K = 8
EXPERT_HIDDEN = 256
ROUTED_SCALE = 2.5
MOE_BLOCK = 256
EPS = 1e-6

LANES = 128
SUBLANES = 8
VMEM_LIMIT = 56 * 1024 * 1024
NEG_INF = float("-inf")


def _dot(a, b):
    return jnp.dot(a, b, preferred_element_type=F32)


def _dot_nt(a, b):
    return lax.dot_general(a, b, (((1,), (1,)), ((), ())), preferred_element_type=F32)


def _dot_tn(a, b):
    return lax.dot_general(a, b, (((0,), (0,)), ((), ())), preferred_element_type=F32)


def _split2(x):
    hi = x.astype(BF16)
    lo = (x - hi.astype(F32)).astype(BF16)
    return hi, lo


def _split3(x):
    hi = x.astype(BF16)
    r = x - hi.astype(F32)
    mid = r.astype(BF16)
    lo = (r - mid.astype(F32)).astype(BF16)
    return hi, mid, lo


def _dot_hp(a, b, dot=_dot):
    ah, al = _split2(a)
    bh, bl = _split2(b)
    return dot(ah, bh) + (dot(ah, bl) + dot(al, bh))


def _sigmoid(x):
    return 1.0 / (1.0 + jnp.exp(-x))


def _silu(x):
    return x * _sigmoid(x)


def _gelu(x):
    return 0.5 * x * (1.0 + lax.erf(x * (2.0 ** -0.5)))


def _softplus(x):
    return jnp.maximum(x, 0.0) + jnp.log1p(jnp.exp(-jnp.abs(x)))


def _rms(x):
    return x * lax.rsqrt(jnp.mean(x * x, axis=-1, keepdims=True) + EPS)


def _params(*sem):
    return pltpu.CompilerParams(dimension_semantics=sem, vmem_limit_bytes=VMEM_LIMIT)


def _const_spec(shape):
    nd = len(shape)
    return pl.BlockSpec(shape, lambda *_: (0,) * nd)


def _ada_kernel(c_ref, w_ref, b_ref, o_ref):
    cs = _silu(c_ref[...])
    o_ref[...] = _dot_hp(cs, w_ref[...]) + b_ref[...]


def _ada(c, ada_w, ada_b):
    b, d = c.shape
    n = ada_w.shape[1]
    tn = d
    return pl.pallas_call(
        _ada_kernel,
        grid=(n // tn,),
        in_specs=[_const_spec((b, d)), pl.BlockSpec((d, tn), lambda j: (0, j)), pl.BlockSpec((1, tn), lambda j: (0, j))],
        out_specs=pl.BlockSpec((b, tn), lambda j: (0, j)),
        out_shape=jax.ShapeDtypeStruct((b, n), F32),
        compiler_params=_params("arbitrary"),
        name="ada",
    )(c, ada_w, ada_b.reshape(1, n))


W1_COLS = 4 * DN_WIDTH + LANES


def _in_kernel(x_ref, mod_ref, pn_ref, w1_ref, wuv_ref, lnw_ref, lnb_ref, sgw_ref, sgbt_ref,
               qkv_ref, z_ref, ab_ref, ysg_ref):
    tm = x_ref.shape[0]
    mod = mod_ref[0]
    hm = _rms(x_ref[...]) * pn_ref[...] * (1.0 + mod[1:2, :]) + mod[0:1, :]
    hb = hm.astype(BF16)
    p1 = _dot(hb, w1_ref[...])
    qkv_ref[...] = p1[:, :3 * DN_WIDTH]
    z_ref[...] = p1[:, 3 * DN_WIDTH:4 * DN_WIDTH]
    ab_ref[...] = p1[:, 4 * DN_WIDTH:]
    uv = _dot(hb, wuv_ref[...])
    u = _gelu(uv[:, :SG_WIDTH])
    vg = _gelu(uv[:, SG_WIDTH:])
    mu = jnp.mean(vg, axis=-1, keepdims=True)
    dv = vg - mu
    var = jnp.mean(dv * dv, axis=-1, keepdims=True)
    vgn = (dv * lax.rsqrt(var + EPS) * lnw_ref[...] + lnb_ref[...]).astype(BF16)
    row = lax.broadcasted_iota(I32, (SG_CHUNK, SG_CHUNK), 0)
    col = lax.broadcasted_iota(I32, (SG_CHUNK, SG_CHUNK), 1)
    tril = row >= col
    for g in range(SG_GROUPS):
        wg = jnp.where(tril, sgw_ref[g], 0.0).astype(BF16)
        bg = sgbt_ref[:, g:g + 1]
        cs = slice(g * SG_GROUP_DIM, (g + 1) * SG_GROUP_DIM)
        for n in range(tm // SG_CHUNK):
            rs = slice(n * SG_CHUNK, (n + 1) * SG_CHUNK)
            mixed = _dot(wg, vgn[rs, cs]) + bg
            ysg_ref[rs, cs] = (u[rs, cs] * mixed).astype(BF16)


def _in_proj(x2, mod3, pre_norm, w1, wuv, sg_ln_w, sg_ln_b, sg_w, sg_bt, seq, tm):
    t, d = x2.shape
    tiles_per_batch = seq // tm
    tok = lambda i: (i, 0)
    return pl.pallas_call(
        _in_kernel,
        grid=(t // tm,),
        in_specs=[
            pl.BlockSpec((tm, d), tok),
            pl.BlockSpec((1, 6, d), lambda i: (i // tiles_per_batch, 0, 0)),
            _const_spec((1, d)),
            _const_spec(w1.shape),
            _const_spec(wuv.shape),
            _const_spec((1, SG_WIDTH)),
            _const_spec((1, SG_WIDTH)),
            _const_spec(sg_w.shape),
            _const_spec(sg_bt.shape),
        ],
        out_specs=[
            pl.BlockSpec((tm, 3 * DN_WIDTH), tok),
            pl.BlockSpec((tm, DN_WIDTH), tok),
            pl.BlockSpec((tm, LANES), tok),
            pl.BlockSpec((tm, SG_WIDTH), tok),
        ],
        out_shape=[
            jax.ShapeDtypeStruct((t, 3 * DN_WIDTH), F32),
            jax.ShapeDtypeStruct((t, DN_WIDTH), F32),
            jax.ShapeDtypeStruct((t, LANES), F32),
            jax.ShapeDtypeStruct((t, SG_WIDTH), BF16),
        ],
        compiler_params=_params("arbitrary"),
        name="in_proj",
    )(x2, mod3, pre_norm, w1, wuv, sg_ln_w, sg_ln_b, sg_w, sg_bt)


def _hp_parts(ap, bp, dot=_dot):
    return dot(ap[0], bp[0]) + (dot(ap[0], bp[1]) + dot(ap[1], bp[0]))


def _unit_lower_inverses(a_list):
    c = a_list[0].shape[0]
    i = lax.broadcasted_iota(I32, (c, c), 0)
    j = lax.broadcasted_iota(I32, (c, c), 1)
    eye = (i == j).astype(F32)
    first = (i == j + 1) & ((i & 1) == 1)
    d_list = [eye - jnp.where(first, a, 0.0) for a in a_list]
    b = 2
    while b < c:
        shift = b.bit_length()
        off = ((i >> shift) == (j >> shift)) & ((i & b) != 0) & ((j & b) == 0)
        a_parts = [_split2(jnp.where(off, a, 0.0)) for a in a_list]
        d_parts = [_split2(d) for d in d_list]
        t_list = [_hp_parts(dp, ap) for dp, ap in zip(d_parts, a_parts)]
        t_parts = [_split2(t) for t in t_list]
        d_list = [d - _hp_parts(tp, dp) for d, tp, dp in zip(d_list, t_parts, d_parts)]
        b *= 2
    return d_list


def _dn_kernel(qkv_ref, z_ref, ab_ref, convw_ref, alog_ref, dtb_ref, nw_ref, y_ref, carry_ref, state_ref):
    nb, c = qkv_ref.shape[0], qkv_ref.shape[1]
    hd = DN_HEAD_DIM
    probs = [(b, h) for b in range(nb) for h in range(DN_HEADS)]

    @pl.when(pl.program_id(0) == 0)
    def _():
        carry_ref[...] = jnp.zeros_like(carry_ref)
        state_ref[...] = jnp.zeros_like(state_ref)

    ab = ab_ref[...].reshape(nb * c, LANES)
    g = -jnp.exp(alog_ref[...]) * _softplus(ab + dtb_ref[...])
    beta_all = _sigmoid(ab)
    ri = lax.broadcasted_iota(I32, (nb * c, nb * c), 0)
    ci = lax.broadcasted_iota(I32, (nb * c, nb * c), 1)
    tri = ((ri >= ci) & ((ri // c) == (ci // c))).astype(BF16)
    gh, gm, gl = _split3(g)
    gc = _dot(tri, gh) + (_dot(tri, gm) + _dot(tri, gl))
    gct = gc.T

    i = lax.broadcasted_iota(I32, (c, c), 0)
    j = lax.broadcasted_iota(I32, (c, c), 1)
    causal = i >= j
    strict = i > j
    row8 = lax.broadcasted_iota(I32, (SUBLANES, 3 * DN_WIDTH), 0)

    acts, gcs, e_gcs, e_rems, e_lasts = [], [], [], [], []
    for b in range(nb):
        xc = qkv_ref[b]
        prev = carry_ref[b]
        acc = xc * convw_ref[CONV_WIDTH - 1:CONV_WIDTH, :]
        for s in range(1, CONV_WIDTH):
            rolled = pltpu.roll(xc, s, axis=0)
            top = jnp.where(row8 < s, pltpu.roll(prev, s, axis=0), rolled[:SUBLANES])
            shifted = jnp.concatenate([top, rolled[SUBLANES:]], axis=0)
            acc = acc + shifted * convw_ref[CONV_WIDTH - 1 - s:CONV_WIDTH - s, :]
        carry_ref[b] = xc[c - SUBLANES:]
        acts.append(_silu(acc))
        gcb = gc[b * c:(b + 1) * c]
        g_last = gcb[c - 1:c, :]
        gcs.append(gcb)
        e_gcs.append(jnp.exp(gcb))
        e_rems.append(jnp.exp(g_last - gcb))
        e_lasts.append(jnp.exp(g_last))

    qn, kn, knb, kb, vb, decay = {}, {}, {}, {}, {}, {}
    for p in probs:
        b, h = p
        act = acts[b]
        q = act[:, h * hd:(h + 1) * hd]
        k = act[:, DN_WIDTH + h * hd:DN_WIDTH + (h + 1) * hd]
        v = act[:, 2 * DN_WIDTH + h * hd:2 * DN_WIDTH + (h + 1) * hd]
        qn[p] = q * lax.rsqrt(jnp.sum(q * q, axis=-1, keepdims=True) + EPS) * (hd ** -0.5)
        kn[p] = k * lax.rsqrt(jnp.sum(k * k, axis=-1, keepdims=True) + EPS)
        beta = beta_all[b * c:(b + 1) * c, DN_HEADS + h:DN_HEADS + h + 1]
        diff = gcs[b][:, h:h + 1] - gct[h:h + 1, b * c:(b + 1) * c]
        decay[p] = jnp.where(causal, jnp.exp(jnp.where(causal, diff, 0.0)), 0.0)
        kb[p] = kn[p] * beta
        vb[p] = v * beta
        knb[p] = kn[p].astype(BF16)

    kk = {p: _dot_nt(kb[p].astype(BF16), knb[p]) for p in probs}
    qk = {p: _dot_nt(qn[p].astype(BF16), knb[p]) for p in probs}
    tinv = dict(zip(probs, _unit_lower_inverses([jnp.where(strict, kk[p] * decay[p], 0.0) for p in probs])))
    t_parts = {p: _split2(tinv[p]) for p in probs}
    rhs = {p: jnp.concatenate([vb[p], kb[p] * e_gcs[p[0]][:, p[1]:p[1] + 1]], axis=1) for p in probs}
    r_parts = {p: _split2(rhs[p]) for p in probs}
    sol = {p: _hp_parts(t_parts[p], r_parts[p]) for p in probs}

    lhs = {p: jnp.concatenate([sol[p][:, hd:], qn[p] * e_gcs[p[0]][:, p[1]:p[1] + 1]], axis=0).astype(BF16) for p in probs}
    state = {p: state_ref[p[0] * DN_HEADS + p[1]] for p in probs}
    ws = {p: _dot(lhs[p], state[p].astype(BF16)) for p in probs}
    vnb = {p: (sol[p][:, :hd] - ws[p][:c]).astype(BF16) for p in probs}
    qkm = {p: jnp.where(causal, qk[p] * decay[p], 0.0).astype(BF16) for p in probs}
    o = {p: ws[p][c:] + _dot(qkm[p], vnb[p]) for p in probs}
    kd = {p: (kn[p] * e_rems[p[0]][:, p[1]:p[1] + 1]).astype(BF16) for p in probs}
    upd = {p: _dot_tn(kd[p], vnb[p]) for p in probs}
    for p in probs:
        b, h = p
        state_ref[b * DN_HEADS + h] = state[p] * e_lasts[b][:, h:h + 1] + upd[p]
        zh = z_ref[b, :, h * hd:(h + 1) * hd]
        y_ref[b, :, h * hd:(h + 1) * hd] = (_rms(o[p]) * nw_ref[...] * _silu(zh)).astype(BF16)


def _delta_net(qkv3, z3, ab3, conv_w, alog_row, dtb_row, dn_norm_w):
    nb, s, _ = qkv3.shape
    c = DN_CHUNK
    blk = lambda n: (0, n, 0)
    return pl.pallas_call(
        _dn_kernel,
        grid=(s // c,),
        in_specs=[
            pl.BlockSpec((nb, c, 3 * DN_WIDTH), blk),
            pl.BlockSpec((nb, c, DN_WIDTH), blk),
            pl.BlockSpec((nb, c, LANES), blk),
            _const_spec(conv_w.shape),
            _const_spec((1, LANES)),
            _const_spec((1, LANES)),
            _const_spec((1, DN_HEAD_DIM)),
        ],
        out_specs=pl.BlockSpec((nb, c, DN_WIDTH), blk),
        out_shape=jax.ShapeDtypeStruct((nb, s, DN_WIDTH), BF16),
        scratch_shapes=[
            pltpu.VMEM((nb, SUBLANES, 3 * DN_WIDTH), F32),
            pltpu.VMEM((nb * DN_HEADS, DN_HEAD_DIM, DN_HEAD_DIM), F32),
        ],
        compiler_params=_params("arbitrary"),
        name="delta_net",
    )(qkv3, z3, ab3, conv_w, alog_row, dtb_row, dn_norm_w)


def _mix_kernel(x_ref, ydn_ref, ysg_ref, mod_ref, pn_ref, postn_ref, fpn_ref,
                wbg_ref, bbg_ref, wpd_ref, wps_ref, wout_ref, wrh_ref, wrl_ref, wsg_ref, wsu_ref, wsd_ref,
                x1_ref, hf_ref, sct_ref, sh_ref):
    d = x_ref.shape[1]
    x = x_ref[...]
    mod = mod_ref[0]
    hm = _rms(x) * pn_ref[...] * (1.0 + mod[1:2, :]) + mod[0:1, :]
    gates = _sigmoid(_dot(hm.astype(BF16), wbg_ref[...]) + bbg_ref[...])
    merged = gates[:, :d] * _dot(ydn_ref[...], wpd_ref[...]) + gates[:, d:] * _dot(ysg_ref[...], wps_ref[...])
    y = _dot(merged.astype(BF16), wout_ref[...])
    x1 = x + mod[2:3, :] * (_rms(y) * postn_ref[...])
    x1_ref[...] = x1
    hf = _rms(x1) * fpn_ref[...] * (1.0 + mod[4:5, :]) + mod[3:4, :]
    hf_ref[...] = hf
    hh, hl = _split2(hf)
    logits_t = _dot_nt(wrh_ref[...], hh) + (_dot_nt(wrh_ref[...], hl) + _dot_nt(wrl_ref[...], hh))
    sct_ref[...] = _sigmoid(logits_t)
    hid = _silu(_dot(hh, wsg_ref[...])) * _dot(hh, wsu_ref[...])
    sh_ref[...] = _dot(hid.astype(BF16), wsd_ref[...])


def _mix(x2, ydn, ysg, mod3, pre_norm, post_norm, ffn_pre_norm, wbg, bbg, wpd, wps, wout, wrh, wrl, wsg, wsu, wsd, seq, tm):
    t, d = x2.shape
    tiles_per_batch = seq // tm
    tok = lambda i: (i, 0)
    consts = [pre_norm, post_norm, ffn_pre_norm, wbg, bbg, wpd, wps, wout, wrh, wrl, wsg, wsu, wsd]
    return pl.pallas_call(
        _mix_kernel,
        grid=(t // tm,),
        in_specs=[
            pl.BlockSpec((tm, d), tok),
            pl.BlockSpec((tm, DN_WIDTH), tok),
            pl.BlockSpec((tm, SG_WIDTH), tok),
            pl.BlockSpec((1, 6, d), lambda i: (i // tiles_per_batch, 0, 0)),
        ] + [_const_spec(a.shape) for a in consts],
        out_specs=[
            pl.BlockSpec((tm, d), tok),
            pl.BlockSpec((tm, d), tok),
            pl.BlockSpec((N_EXPERTS, tm), lambda i: (0, i)),
            pl.BlockSpec((tm, d), tok),
        ],
        out_shape=[
            jax.ShapeDtypeStruct((t, d), F32),
            jax.ShapeDtypeStruct((t, d), F32),
            jax.ShapeDtypeStruct((N_EXPERTS, t), F32),
            jax.ShapeDtypeStruct((t, d), F32),
        ],
        compiler_params=_params("arbitrary"),
        name="mix",
    )(x2, ydn, ysg, mod3, *consts)


def _first_argmax(vals, idx):
    m = jnp.max(vals, axis=0, keepdims=True)
    first = jnp.min(jnp.where(vals == m, idx, jnp.int32(2 ** 30)), axis=0, keepdims=True)
    return m, first


def _route_kernel(sct_ref, bias_ref, idx_ref, wtt_ref, rank_ref, cnt_ref, carry_ref):
    tm = sct_ref.shape[1]

    @pl.when(pl.program_id(0) == 0)
    def _():
        carry_ref[...] = jnp.zeros_like(carry_ref)

    scores = sct_ref[...]
    sel = scores + bias_ref[...]
    erow = lax.broadcasted_iota(I32, (N_EXPERTS, tm), 0)
    grow = lax.broadcasted_iota(I32, (GROUP_SIZE, tm), 0)

    gs = []
    for gidx in range(N_EXPERT_GROUPS):
        sg = sel[gidx * GROUP_SIZE:(gidx + 1) * GROUP_SIZE]
        m1, first = _first_argmax(sg, grow)
        m2 = jnp.max(jnp.where(grow == first, NEG_INF, sg), axis=0, keepdims=True)
        gs.append(m1 + m2)
    gsc = jnp.concatenate(gs, axis=0)
    giota = lax.broadcasted_iota(I32, (N_EXPERT_GROUPS, tm), 0)
    gmask = jnp.zeros((N_EXPERT_GROUPS, tm), F32)
    cur = gsc
    for _ in range(TOPK_GROUPS):
        _, gi = _first_argmax(cur, giota)
        pick = giota == gi
        gmask = jnp.where(pick, 1.0, gmask)
        cur = jnp.where(pick, NEG_INF, cur)
    masked = jnp.concatenate(
        [jnp.where(gmask[gidx:gidx + 1, :] > 0.5, sel[gidx * GROUP_SIZE:(gidx + 1) * GROUP_SIZE], NEG_INF)
         for gidx in range(N_EXPERT_GROUPS)], axis=0)

    cur = masked
    idxs, wts = [], []
    onehot = jnp.zeros((N_EXPERTS, tm), F32)
    for _ in range(TOP_K):
        _, ei = _first_argmax(cur, erow)
        pick = erow == ei
        idxs.append(ei)
        wts.append(jnp.sum(jnp.where(pick, scores, 0.0), axis=0, keepdims=True))
        onehot = jnp.where(pick, 1.0, onehot)
        cur = jnp.where(pick, NEG_INF, cur)
    idx = jnp.concatenate(idxs, axis=0)
    wt = jnp.concatenate(wts, axis=0)
    wt = wt / jnp.sum(wt, axis=0, keepdims=True) * ROUTED_SCALE
    idx_ref[...] = idx
    wpad = jnp.concatenate([wt, jnp.zeros((LANES - TOP_K, tm), F32)], axis=0)
    wtt_ref[...] = wpad.T

    ti = lax.broadcasted_iota(I32, (tm, tm), 0)
    tj = lax.broadcasted_iota(I32, (tm, tm), 1)
    upper = (ti < tj).astype(BF16)
    before = _dot(onehot.astype(BF16), upper) + carry_ref[...]
    rank_ref[...] = jnp.concatenate(
        [jnp.sum(jnp.where(erow == idxs[kk], before, 0.0), axis=0, keepdims=True) for kk in range(TOP_K)],
        axis=0).astype(I32)
    total = carry_ref[...] + jnp.sum(onehot, axis=1, keepdims=True)
    carry_ref[...] = total
    cnt_ref[...] = total


def _route(sct, bias_col, tm):
    e, t = sct.shape
    tile = lambda i: (0, i)
    return pl.pallas_call(
        _route_kernel,
        grid=(t // tm,),
        in_specs=[pl.BlockSpec((e, tm), tile), _const_spec((e, 1))],
        out_specs=[
            pl.BlockSpec((TOP_K, tm), tile),
            pl.BlockSpec((tm, LANES), lambda i: (i, 0)),
            pl.BlockSpec((TOP_K, tm), tile),
            _const_spec((e, 1)),
        ],
        out_shape=[
            jax.ShapeDtypeStruct((TOP_K, t), I32),
            jax.ShapeDtypeStruct((t, LANES), F32),
            jax.ShapeDtypeStruct((TOP_K, t), I32),
            jax.ShapeDtypeStruct((e, 1), F32),
        ],
        scratch_shapes=[pltpu.VMEM((e, 1), F32)],
        compiler_params=_params("arbitrary"),
        name="route",
    )(sct, bias_col)


def _dest_kernel(cnt_ref, idx_ref, rank_ref, dest_ref, meta_ref):
    tm = idx_ref.shape[1]
    e = N_EXPERTS
    cnt = cnt_ref[...]
    padded = jnp.floor((cnt + (MOE_BLOCK - 1)) * (1.0 / MOE_BLOCK)) * MOE_BLOCK
    pw = jnp.broadcast_to(padded, (e, LANES))
    ri = lax.broadcasted_iota(I32, (e, e), 0)
    ci = lax.broadcasted_iota(I32, (e, e), 1)
    lower = (ri >= ci).astype(BF16)
    ph, pm, plo = _split3(pw)
    pends = _dot(lower, ph) + (_dot(lower, pm) + _dot(lower, plo))
    pstart = pends - pw
    erow = lax.broadcasted_iota(I32, (e, tm), 0)
    idx = idx_ref[...]
    rows = [jnp.sum(jnp.where(erow == idx[kk:kk + 1, :], pstart[:, 0:1], 0.0), axis=0, keepdims=True)
            for kk in range(TOP_K)]
    dest_ref[...] = jnp.concatenate(rows, axis=0).astype(I32) + rank_ref[...]
    lane = lax.broadcasted_iota(I32, (e, LANES), 1)
    blocks = jnp.where(lane == 0, pstart, pw) * (1.0 / MOE_BLOCK)
    pads = jnp.where(lane == 2, pstart + cnt, pw - cnt)
    meta_ref[...] = jnp.where(lane < 2, blocks, pads).astype(I32)


def _dest(cnt, idx, rank, tm):
    e = cnt.shape[0]
    t = idx.shape[1]
    tile = lambda i: (0, i)
    return pl.pallas_call(
        _dest_kernel,
        grid=(t // tm,),
        in_specs=[_const_spec((e, 1)), pl.BlockSpec((TOP_K, tm), tile), pl.BlockSpec((TOP_K, tm), tile)],
        out_specs=[pl.BlockSpec((TOP_K, tm), tile), _const_spec((e, LANES))],
        out_shape=[jax.ShapeDtypeStruct((TOP_K, t), I32), jax.ShapeDtypeStruct((e, LANES), I32)],
        compiler_params=_params("arbitrary"),
        name="dest",
    )(cnt, idx, rank)


def _row_copy(src, dst, sem):
    return pltpu.make_async_copy(src, dst, sem)


PAD_PIECES = tuple(MOE_BLOCK >> s for s in range(1, MOE_BLOCK.bit_length()))


def _zero_unassigned_rows(pad_start_ref, pad_rows_ref, xs_out, zero_ref, sem):
    zero_ref[...] = jnp.zeros_like(zero_ref)
    ne = pad_start_ref.shape[0]
    total = xs_out.shape[0] // MOE_BLOCK

    def piece(rows, pos):
        return pltpu.make_async_copy(zero_ref.at[pl.ds(0, rows), :], xs_out.at[pl.ds(pos, rows), :], sem)

    def pads(wait):
        def per_expert(ex, carry):
            pos = pad_start_ref[ex]
            pad = pad_rows_ref[ex]
            for rows in reversed(PAD_PIECES):
                @pl.when((pad & rows) != 0)
                def _():
                    if rows < SUBLANES:
                        copies = [piece(1, pos + r) for r in range(rows)]
                    else:
                        copies = [piece(rows, pl.multiple_of(pos, SUBLANES))]
                    for cp in copies:
                        cp.wait() if wait else cp.start()
                pos = pos + (pad & rows)
            return carry
        lax.fori_loop(0, ne, per_expert, 0)

    def tail(wait):
        used = (pad_start_ref[ne - 1] + pad_rows_ref[ne - 1]) // MOE_BLOCK

        def per_block(blk, carry):
            cp = piece(MOE_BLOCK, blk * MOE_BLOCK)
            cp.wait() if wait else cp.start()
            return carry
        lax.fori_loop(used, total, per_block, 0)

    pads(False)
    tail(False)
    pads(True)
    tail(True)


def _scatter_kernel(pad_start_ref, pad_rows_ref, dest_hbm, hf_ref, xs_out, dest_smem, zero_ref, sem_idx, sem_rows, sem_zero):
    tm = hf_ref.shape[0]
    i = pl.program_id(0)

    @pl.when(i == 0)
    def _():
        _zero_unassigned_rows(pad_start_ref, pad_rows_ref, xs_out, zero_ref, sem_zero)

    idx_copy = pltpu.make_async_copy(dest_hbm.at[i], dest_smem, sem_idx)
    idx_copy.start()
    idx_copy.wait()

    def issue(tok, carry):
        for kk in range(TOP_K):
            slot = dest_smem[kk * tm + tok]
            _row_copy(hf_ref.at[pl.ds(tok, 1), :], xs_out.at[pl.ds(slot, 1), :], sem_rows).start(priority=kk % 2)
        return carry

    lax.fori_loop(0, tm, issue, 0)
    for kk in range(TOP_K):
        _row_copy(hf_ref, xs_out.at[pl.ds(0, tm), :], sem_rows).wait()


def _scatter(pad_start, pad_rows, dest_tiles, hf, n_slots, tm):
    t, d = hf.shape
    n_tiles = t // tm
    return pl.pallas_call(
        _scatter_kernel,
        grid_spec=pltpu.PrefetchScalarGridSpec(
            num_scalar_prefetch=2,
            grid=(n_tiles,),
            in_specs=[pl.BlockSpec(memory_space=pl.ANY), pl.BlockSpec((tm, d), lambda i, ps, pr: (i, 0))],
            out_specs=pl.BlockSpec(memory_space=pl.ANY),
            scratch_shapes=[
                pltpu.SMEM((TOP_K * tm,), I32),
                pltpu.VMEM((MOE_BLOCK, d), hf.dtype),
                pltpu.SemaphoreType.DMA,
                pltpu.SemaphoreType.DMA,
                pltpu.SemaphoreType.DMA,
            ],
        ),
        out_shape=jax.ShapeDtypeStruct((n_slots, d), hf.dtype),
        compiler_params=_params("arbitrary"),
        name="scatter",
    )(pad_start, pad_rows, dest_tiles, hf)


MOE_BUFFERS = 3


def _moe_kernel(bstart_ref, nblk_ref, xs_hbm, wg_ref, wu_ref, wd_ref, ys_hbm,
                xbuf, ybuf, wgb_ref, wub_ref, wdb_ref, sem_in, sem_out):
    ex = pl.program_id(0)
    n = nblk_ref[ex]
    b0 = bstart_ref[ex]

    def in_copy(blk, slot):
        return pltpu.make_async_copy(xs_hbm.at[pl.ds((b0 + blk) * MOE_BLOCK, MOE_BLOCK), :], xbuf.at[slot], sem_in.at[slot])

    def out_copy(blk, slot):
        return pltpu.make_async_copy(ybuf.at[slot], ys_hbm.at[pl.ds((b0 + blk) * MOE_BLOCK, MOE_BLOCK), :], sem_out.at[slot])

    for s in range(MOE_BUFFERS - 1):
        @pl.when(s < n)
        def _():
            in_copy(s, s).start()

    wgb_ref[...] = wg_ref[0].astype(BF16)
    wub_ref[...] = wu_ref[0].astype(BF16)
    wdb_ref[...] = wd_ref[0].astype(BF16)

    def body(blk, carry):
        slot = lax.rem(blk, MOE_BUFFERS)
        in_copy(blk, slot).wait()
        ahead = blk + (MOE_BUFFERS - 1)

        @pl.when(ahead < n)
        def _():
            in_copy(ahead, lax.rem(ahead, MOE_BUFFERS)).start()

        @pl.when(blk >= MOE_BUFFERS)
        def _():
            out_copy(blk - MOE_BUFFERS, slot).wait()

        xb = xbuf[slot].astype(BF16)
        hid = _silu(_dot(xb, wgb_ref[...])) * _dot(xb, wub_ref[...])
        ybuf[slot] = _dot(hid.astype(BF16), wdb_ref[...])
        out_copy(blk, slot).start()
        return carry

    lax.fori_loop(0, n, body, 0)
    for s in range(MOE_BUFFERS):
        @pl.when(s < n)
        def _():
            out_copy(0, s).wait()

    @pl.when(ex == pl.num_programs(0) - 1)
    def _():
        used = b0 + n
        total = ys_hbm.shape[0] // MOE_BLOCK
        ybuf[0] = jnp.zeros(ybuf.shape[1:], ybuf.dtype)

        def tail_copy(blk):
            return pltpu.make_async_copy(ybuf.at[0], ys_hbm.at[pl.ds(blk * MOE_BLOCK, MOE_BLOCK), :], sem_out.at[0])

        def start_tail(blk, carry):
            tail_copy(blk).start()
            return carry

        def wait_tail(blk, carry):
            tail_copy(blk).wait()
            return carry

        lax.fori_loop(used, total, start_tail, 0)
        lax.fori_loop(used, total, wait_tail, 0)


def _moe(bstart, nblk_e, xs, w_gate, w_up, w_down):
    ns, d = xs.shape
    ne, _, eh = w_gate.shape
    wspec = lambda shape: pl.BlockSpec(shape, lambda ex, bs, nb: (ex, 0, 0))
    return pl.pallas_call(
        _moe_kernel,
        grid_spec=pltpu.PrefetchScalarGridSpec(
            num_scalar_prefetch=2,
            grid=(ne,),
            in_specs=[pl.BlockSpec(memory_space=pl.ANY), wspec((1, d, eh)), wspec((1, d, eh)), wspec((1, eh, d))],
            out_specs=pl.BlockSpec(memory_space=pl.ANY),
            scratch_shapes=[
                pltpu.VMEM((MOE_BUFFERS, MOE_BLOCK, d), F32),
                pltpu.VMEM((MOE_BUFFERS, MOE_BLOCK, d), F32),
                pltpu.VMEM((d, eh), BF16), pltpu.VMEM((d, eh), BF16), pltpu.VMEM((eh, d), BF16),
                pltpu.SemaphoreType.DMA((MOE_BUFFERS,)),
                pltpu.SemaphoreType.DMA((MOE_BUFFERS,)),
            ],
        ),
        out_shape=jax.ShapeDtypeStruct((ns, d), F32),
        compiler_params=_params("arbitrary"),
        name="moe",
    )(bstart, nblk_e, xs, w_gate, w_up, w_down)


def _combine_kernel(dest_hbm, ys_hbm, wtt_ref, x1_ref, sh_ref, mod_ref, postn_ref, o_ref, dest_smem, buf_ref, sem_idx, sem_rows):
    tc = x1_ref.shape[0]
    i = pl.program_id(0)
    n = pl.num_programs(0)
    cur = lax.rem(i, 2)
    nxt = 1 - cur

    def idx_copy(tile, slot):
        return pltpu.make_async_copy(dest_hbm.at[tile], dest_smem.at[slot], sem_idx.at[slot])

    def issue_rows(slot):
        def issue(tok, carry):
            for kk in range(TOP_K):
                row = dest_smem[slot, kk * tc + tok]
                _row_copy(ys_hbm.at[pl.ds(row, 1), :], buf_ref.at[slot, kk, pl.ds(tok, 1), :],
                          sem_rows.at[slot]).start(priority=kk % 2)
            return carry
        lax.fori_loop(0, tc, issue, 0)

    @pl.when(i == 0)
    def _():
        idx_copy(0, 0).start()
        idx_copy(0, 0).wait()

        @pl.when(n > 1)
        def _():
            idx_copy(1, 1).start()
        issue_rows(0)

    @pl.when(i + 1 < n)
    def _():
        idx_copy(i + 1, nxt).wait()

        @pl.when(i + 2 < n)
        def _():
            idx_copy(i + 2, cur).start()
        issue_rows(nxt)

    for kk in range(TOP_K):
        _row_copy(ys_hbm.at[pl.ds(0, tc), :], buf_ref.at[cur, kk], sem_rows.at[cur]).wait()

    wtt = wtt_ref[...]
    acc = buf_ref[cur, 0] * wtt[:, 0:1]
    for kk in range(1, TOP_K):
        acc = acc + buf_ref[cur, kk] * wtt[:, kk:kk + 1]
    y = acc + sh_ref[...]
    mod = mod_ref[0]
    o_ref[...] = x1_ref[...] + mod[5:6, :] * (_rms(y) * postn_ref[...])


def _combine(dest_tiles, ys, wtt, x1, shared, mod3, ffn_post_norm, seq, tc):
    t, d = x1.shape
    tiles_per_batch = seq // tc
    tok = lambda i: (i, 0)
    return pl.pallas_call(
        _combine_kernel,
        grid=(t // tc,),
        in_specs=[
            pl.BlockSpec(memory_space=pl.ANY),
            pl.BlockSpec(memory_space=pl.ANY),
            pl.BlockSpec((tc, LANES), tok),
            pl.BlockSpec((tc, d), tok),
            pl.BlockSpec((tc, d), tok),
            pl.BlockSpec((1, 6, d), lambda i: (i // tiles_per_batch, 0, 0)),
            _const_spec((1, d)),
        ],
        out_specs=pl.BlockSpec((tc, d), tok),
        out_shape=jax.ShapeDtypeStruct((t, d), F32),
        scratch_shapes=[
            pltpu.SMEM((2, TOP_K * tc), I32),
            pltpu.VMEM((2, TOP_K, tc, d), F32),
            pltpu.SemaphoreType.DMA((2,)),
            pltpu.SemaphoreType.DMA((2,)),
        ],
        compiler_params=_params("arbitrary"),
        name="combine",
    )(dest_tiles, ys, wtt, x1, shared, mod3, ffn_post_norm)


def _tile(n, want):
    t = min(n, want)
    assert n % t == 0
    return t


def _dest_tiles(dest, tile):
    k, t = dest.shape
    return dest.reshape(k, t // tile, tile).transpose(1, 0, 2).reshape(t // tile, k * tile)


def kernel(x, c, ada_w, ada_b, mix_pre_norm, mix_post_norm, w_in, conv_w, a_log, dt_bias, dn_norm_w, sg_ln_w, sg_ln_b, sg_w, sg_b, w_branch_gate, b_branch_gate, w_proj_dn, w_proj_sg, w_out, ffn_pre_norm, ffn_post_norm, w_router, router_bias, w_exp_gate, w_exp_up, w_exp_down, w_sh_gate, w_sh_up, w_sh_down):
    nb, seq, d = x.shape
    depth = ada_w.shape[0]
    t = nb * seq
    tm = _tile(seq, 512)
    tr = _tile(t, 512)
    tsc = _tile(t, 512)
    tcm = _tile(seq, 256)
    nblk = -(-t * TOP_K // MOE_BLOCK) + N_EXPERTS
    row = lambda v: v.reshape(1, -1)
    pad_lanes = lambda v: jnp.pad(v.astype(F32), (0, LANES - v.shape[0])).reshape(1, LANES)

    x2 = x.reshape(t, d)
    for l in range(depth):
        mod3 = _ada(c, ada_w[l], ada_b[l]).reshape(nb, 6, d)

        wi = w_in[l]
        qkvz, ab_cols, uv = wi[:, :4 * DN_WIDTH], wi[:, 4 * DN_WIDTH:4 * DN_WIDTH + 2 * DN_HEADS], wi[:, 4 * DN_WIDTH + 2 * DN_HEADS:]
        w1 = jnp.concatenate([qkvz, ab_cols, jnp.zeros((d, LANES - 2 * DN_HEADS), wi.dtype)], axis=1).astype(BF16)
        qkv, z, ab, ysg = _in_proj(x2, mod3, row(mix_pre_norm[l]), w1, uv.astype(BF16), row(sg_ln_w[l]), row(sg_ln_b[l]),
                                   sg_w[l], sg_b[l].T, seq, tm)

        ydn = _delta_net(qkv.reshape(nb, seq, -1), z.reshape(nb, seq, -1), ab.reshape(nb, seq, -1), conv_w[l],
                         pad_lanes(a_log[l]), pad_lanes(dt_bias[l]), row(dn_norm_w[l]))

        wr_t = w_router[l].T
        wrh = wr_t.astype(BF16)
        wrl = (wr_t - wrh.astype(F32)).astype(BF16)
        x1, hf, sct, shared = _mix(
            x2, ydn.reshape(t, -1), ysg, mod3, row(mix_pre_norm[l]), row(mix_post_norm[l]), row(ffn_pre_norm[l]),
            w_branch_gate[l].astype(BF16), row(b_branch_gate[l]), w_proj_dn[l].astype(BF16), w_proj_sg[l].astype(BF16),
            w_out[l].astype(BF16), wrh, wrl, w_sh_gate[l].astype(BF16), w_sh_up[l].astype(BF16), w_sh_down[l].astype(BF16),
            seq, tm)

        idx, wtt, rank, cnt = _route(sct, router_bias[l].reshape(-1, 1), tr)
        dest, meta = _dest(cnt, idx, rank, tr)
        xs = _scatter(meta[:, 2], meta[:, 3], _dest_tiles(dest, tsc), hf, nblk * MOE_BLOCK, tsc)
        ys = _moe(meta[:, 0], meta[:, 1], xs, w_exp_gate[l], w_exp_up[l], w_exp_down[l])
        x2 = _combine(_dest_tiles(dest, tcm), ys, wtt, x1, shared, mod3, row(ffn_post_norm[l]), seq, tcm)
    return x2.reshape(nb, seq, d)
```

```python
import functools

import jax
import jax.numpy as jnp
from jax import lax
from jax.experimental import pallas as pl
from jax.experimental.pallas import tpu as pltpu

F32 = jnp.float32
BF16 = jnp.bfloat16
I32 = jnp.int32

D_MODEL = 1024
DN_HEADS = 4
DN_HEAD_DIM = 128
DN_WIDTH = DN_HEADS * DN_HEAD_DIM
DN_CHUNK = 64
CONV_WIDTH = 4
SG_GROUPS = 4
SG_GROUP_DIM = 128
SG_WIDTH = SG_GROUPS * SG_GROUP_DIM
SG_CHUNK = 128
N_EXPERTS = 256
N_EXPERT_GROUPS = 8
GROUP_SIZE = N_EXPERTS // N_EXPERT_GROUPS
TOPK_GROUPS = 4
TOP_K = 8
EXPERT_HIDDEN = 256
ROUTED_SCALE = 2.5
MOE_BLOCK = 256
EPS = 1e-6

LANES = 128
SUBLANES = 8
VMEM_LIMIT = 56 * 1024 * 1024
NEG_INF = float("-inf")


def _dot(a, b):
    return jnp.dot(a, b, preferred_element_type=F32)


def _dot_nt(a, b):
    return lax.dot_general(a, b, (((1,), (1,)), ((), ())), preferred_element_type=F32)


def _dot_tn(a, b):
    return lax.dot_general(a, b, (((0,), (0,)), ((), ())), preferred_element_type=F32)


def _split2(x):
    hi = x.astype(BF16)
    lo = (x - hi.astype(F32)).astype(BF16)
    return hi, lo


def _split3(x):
    hi = x.astype(BF16)
    r = x - hi.astype(F32)
    mid = r.astype(BF16)
    lo = (r - mid.astype(F32)).astype(BF16)
    return hi, mid, lo


def _dot_hp(a, b, dot=_dot):
    ah, al = _split2(a)
    bh, bl = _split2(b)
    return dot(ah, bh) + (dot(ah, bl) + dot(al, bh))


def _sigmoid(x):
    return 1.0 / (1.0 + jnp.exp(-x))


def _silu(x):
    return x * _sigmoid(x)


def _gelu(x):
    return 0.5 * x * (1.0 + lax.erf(x * (2.0 ** -0.5)))


def _softplus(x):
    return jnp.maximum(x, 0.0) + jnp.log1p(jnp.exp(-jnp.abs(x)))


def _rms(x):
    return x * lax.rsqrt(jnp.mean(x * x, axis=-1, keepdims=True) + EPS)


def _params(*sem):
    return pltpu.CompilerParams(dimension_semantics=sem, vmem_limit_bytes=VMEM_LIMIT)


def _const_spec(shape):
    nd = len(shape)
    return pl.BlockSpec(shape, lambda *_: (0,) * nd)


def _ada_kernel(c_ref, w_ref, b_ref, o_ref):
    cs = _silu(c_ref[...])
    o_ref[...] = _dot_hp(cs, w_ref[...]) + b_ref[...]


def _ada(c, ada_w, ada_b):
    b, d = c.shape
    n = ada_w.shape[1]
    tn = d
    return pl.pallas_call(
        _ada_kernel,
        grid=(n // tn,),
        in_specs=[_const_spec((b, d)), pl.BlockSpec((d, tn), lambda j: (0, j)), pl.BlockSpec((1, tn), lambda j: (0, j))],
        out_specs=pl.BlockSpec((b, tn), lambda j: (0, j)),
        out_shape=jax.ShapeDtypeStruct((b, n), F32),
        compiler_params=_params("arbitrary"),
        name="ada",
    )(c, ada_w, ada_b.reshape(1, n))


W1_COLS = 4 * DN_WIDTH + LANES


def _in_kernel(x_ref, mod_ref, pn_ref, w1_ref, wuv_ref, lnw_ref, lnb_ref, sgw_ref, sgbt_ref,
               qkv_ref, z_ref, ab_ref, ysg_ref):
    tm = x_ref.shape[0]
    mod = mod_ref[0]
    hm = _rms(x_ref[...]) * pn_ref[...] * (1.0 + mod[1:2, :]) + mod[0:1, :]
    hb = hm.astype(BF16)
    p1 = _dot(hb, w1_ref[...])
    qkv_ref[...] = p1[:, :3 * DN_WIDTH]
    z_ref[...] = p1[:, 3 * DN_WIDTH:4 * DN_WIDTH]
    ab_ref[...] = p1[:, 4 * DN_WIDTH:]
    uv = _dot(hb, wuv_ref[...])
    u = _gelu(uv[:, :SG_WIDTH])
    vg = _gelu(uv[:, SG_WIDTH:])
    mu = jnp.mean(vg, axis=-1, keepdims=True)
    dv = vg - mu
    var = jnp.mean(dv * dv, axis=-1, keepdims=True)
    vgn = (dv * lax.rsqrt(var + EPS) * lnw_ref[...] + lnb_ref[...]).astype(BF16)
    row = lax.broadcasted_iota(I32, (SG_CHUNK, SG_CHUNK), 0)
    col = lax.broadcasted_iota(I32, (SG_CHUNK, SG_CHUNK), 1)
    tril = row >= col
    for g in range(SG_GROUPS):
        wg = jnp.where(tril, sgw_ref[g], 0.0).astype(BF16)
        bg = sgbt_ref[:, g:g + 1]
        cs = slice(g * SG_GROUP_DIM, (g + 1) * SG_GROUP_DIM)
        for n in range(tm // SG_CHUNK):
            rs = slice(n * SG_CHUNK, (n + 1) * SG_CHUNK)
            mixed = _dot(wg, vgn[rs, cs]) + bg
            ysg_ref[rs, cs] = (u[rs, cs] * mixed).astype(BF16)


def _in_proj(x2, mod3, pre_norm, w1, wuv, sg_ln_w, sg_ln_b, sg_w, sg_bt, seq, tm):
    t, d = x2.shape
    tiles_per_batch = seq // tm
    tok = lambda i: (i, 0)
    return pl.pallas_call(
        _in_kernel,
        grid=(t // tm,),
        in_specs=[
            pl.BlockSpec((tm, d), tok),
            pl.BlockSpec((1, 6, d), lambda i: (i // tiles_per_batch, 0, 0)),
            _const_spec((1, d)),
            _const_spec(w1.shape),
            _const_spec(wuv.shape),
            _const_spec((1, SG_WIDTH)),
            _const_spec((1, SG_WIDTH)),
            _const_spec(sg_w.shape),
            _const_spec(sg_bt.shape),
        ],
        out_specs=[
            pl.BlockSpec((tm, 3 * DN_WIDTH), tok),
            pl.BlockSpec((tm, DN_WIDTH), tok),
            pl.BlockSpec((tm, LANES), tok),
            pl.BlockSpec((tm, SG_WIDTH), tok),
        ],
        out_shape=[
            jax.ShapeDtypeStruct((t, 3 * DN_WIDTH), F32),
            jax.ShapeDtypeStruct((t, DN_WIDTH), F32),
            jax.ShapeDtypeStruct((t, LANES), F32),
            jax.ShapeDtypeStruct((t, SG_WIDTH), BF16),
        ],
        compiler_params=_params("arbitrary"),
        name="in_proj",
    )(x2, mod3, pre_norm, w1, wuv, sg_ln_w, sg_ln_b, sg_w, sg_bt)


def _hp_parts(ap, bp, dot=_dot):
    return dot(ap[0], bp[0]) + (dot(ap[0], bp[1]) + dot(ap[1], bp[0]))


def _unit_lower_inverses(a_list):
    c = a_list[0].shape[0]
    i = lax.broadcasted_iota(I32, (c, c), 0)
    j = lax.broadcasted_iota(I32, (c, c), 1)
    eye = (i == j).astype(F32)
    first = (i == j + 1) & ((i & 1) == 1)
    d_list = [eye - jnp.where(first, a, 0.0) for a in a_list]
    b = 2
    while b < c:
        shift = b.bit_length()
        off = ((i >> shift) == (j >> shift)) & ((i & b) != 0) & ((j & b) == 0)
        a_parts = [_split2(jnp.where(off, a, 0.0)) for a in a_list]
        d_parts = [_split2(d) for d in d_list]
        t_list = [_hp_parts(dp, ap) for dp, ap in zip(d_parts, a_parts)]
        t_parts = [_split2(t) for t in t_list]
        d_list = [d - _hp_parts(tp, dp) for d, tp, dp in zip(d_list, t_parts, d_parts)]
        b *= 2
    return d_list


def _dn_kernel(qkv_ref, z_ref, ab_ref, convw_ref, alog_ref, dtb_ref, nw_ref, y_ref, carry_ref, state_ref):
    nb, c = qkv_ref.shape[0], qkv_ref.shape[1]
    hd = DN_HEAD_DIM
    probs = [(b, h) for b in range(nb) for h in range(DN_HEADS)]

    @pl.when(pl.program_id(0) == 0)
    def _():
        carry_ref[...] = jnp.zeros_like(carry_ref)
        state_ref[...] = jnp.zeros_like(state_ref)

    ab = ab_ref[...].reshape(nb * c, LANES)
    g = -jnp.exp(alog_ref[...]) * _softplus(ab + dtb_ref[...])
    beta_all = _sigmoid(ab)
    ri = lax.broadcasted_iota(I32, (nb * c, nb * c), 0)
    ci = lax.broadcasted_iota(I32, (nb * c, nb * c), 1)
    tri = ((ri >= ci) & ((ri // c) == (ci // c))).astype(BF16)
    gh, gm, gl = _split3(g)
    gc = _dot(tri, gh) + (_dot(tri, gm) + _dot(tri, gl))
    gct = gc.T

    i = lax.broadcasted_iota(I32, (c, c), 0)
    j = lax.broadcasted_iota(I32, (c, c), 1)
    causal = i >= j
    strict = i > j
    row8 = lax.broadcasted_iota(I32, (SUBLANES, 3 * DN_WIDTH), 0)

    acts, gcs, e_gcs, e_rems, e_lasts = [], [], [], [], []
    for b in range(nb):
        xc = qkv_ref[b]
        prev = carry_ref[b]
        acc = xc * convw_ref[CONV_WIDTH - 1:CONV_WIDTH, :]
        for s in range(1, CONV_WIDTH):
            rolled = pltpu.roll(xc, s, axis=0)
            top = jnp.where(row8 < s, pltpu.roll(prev, s, axis=0), rolled[:SUBLANES])
            shifted = jnp.concatenate([top, rolled[SUBLANES:]], axis=0)
            acc = acc + shifted * convw_ref[CONV_WIDTH - 1 - s:CONV_WIDTH - s, :]
        carry_ref[b] = xc[c - SUBLANES:]
        acts.append(_silu(acc))
        gcb = gc[b * c:(b + 1) * c]
        g_last = gcb[c - 1:c, :]
        gcs.append(gcb)
        e_gcs.append(jnp.exp(gcb))
        e_rems.append(jnp.exp(g_last - gcb))
        e_lasts.append(jnp.exp(g_last))

    qn, kn, knb, kb, vb, decay = {}, {}, {}, {}, {}, {}
    for p in probs:
        b, h = p
        act = acts[b]
        q = act[:, h * hd:(h + 1) * hd]
        k = act[:, DN_WIDTH + h * hd:DN_WIDTH + (h + 1) * hd]
        v = act[:, 2 * DN_WIDTH + h * hd:2 * DN_WIDTH + (h + 1) * hd]
        qn[p] = q * lax.rsqrt(jnp.sum(q * q, axis=-1, keepdims=True) + EPS) * (hd ** -0.5)
        kn[p] = k * lax.rsqrt(jnp.sum(k * k, axis=-1, keepdims=True) + EPS)
        beta = beta_all[b * c:(b + 1) * c, DN_HEADS + h:DN_HEADS + h + 1]
        diff = gcs[b][:, h:h + 1] - gct[h:h + 1, b * c:(b + 1) * c]
        decay[p] = jnp.where(causal, jnp.exp(jnp.where(causal, diff, 0.0)), 0.0)
        kb[p] = kn[p] * beta
        vb[p] = v * beta
        knb[p] = kn[p].astype(BF16)

    kk = {p: _dot_nt(kb[p].astype(BF16), knb[p]) for p in probs}
    qk = {p: _dot_nt(qn[p].astype(BF16), knb[p]) for p in probs}
    tinv = dict(zip(probs, _unit_lower_inverses([jnp.where(strict, kk[p] * decay[p], 0.0) for p in probs])))
    t_parts = {p: _split2(tinv[p]) for p in probs}
    rhs = {p: jnp.concatenate([vb[p], kb[p] * e_gcs[p[0]][:, p[1]:p[1] + 1]], axis=1) for p in probs}
    r_parts = {p: _split2(rhs[p]) for p in probs}
    sol = {p: _hp_parts(t_parts[p], r_parts[p]) for p in probs}

    lhs = {p: jnp.concatenate([sol[p][:, hd:], qn[p] * e_gcs[p[0]][:, p[1]:p[1] + 1]], axis=0).astype(BF16) for p in probs}
    state = {p: state_ref[p[0] * DN_HEADS + p[1]] for p in probs}
    ws = {p: _dot(lhs[p], state[p].astype(BF16)) for p in probs}
    vnb = {p: (sol[p][:, :hd] - ws[p][:c]).astype(BF16) for p in probs}
    qkm = {p: jnp.where(causal, qk[p] * decay[p], 0.0).astype(BF16) for p in probs}
    o = {p: ws[p][c:] + _dot(qkm[p], vnb[p]) for p in probs}
    kd = {p: (kn[p] * e_rems[p[0]][:, p[1]:p[1] + 1]).astype(BF16) for p in probs}
    upd = {p: _dot_tn(kd[p], vnb[p]) for p in probs}
    for p in probs:
        b, h = p
        state_ref[b * DN_HEADS + h] = state[p] * e_lasts[b][:, h:h + 1] + upd[p]
        zh = z_ref[b, :, h * hd:(h + 1) * hd]
        y_ref[b, :, h * hd:(h + 1) * hd] = (_rms(o[p]) * nw_ref[...] * _silu(zh)).astype(BF16)


def _delta_net(qkv3, z3, ab3, conv_w, alog_row, dtb_row, dn_norm_w):
    nb, s, _ = qkv3.shape
    c = DN_CHUNK
    blk = lambda n: (0, n, 0)
    return pl.pallas_call(
        _dn_kernel,
        grid=(s // c,),
        in_specs=[
            pl.BlockSpec((nb, c, 3 * DN_WIDTH), blk),
            pl.BlockSpec((nb, c, DN_WIDTH), blk),
            pl.BlockSpec((nb, c, LANES), blk),
            _const_spec(conv_w.shape),
            _const_spec((1, LANES)),
            _const_spec((1, LANES)),
            _const_spec((1, DN_HEAD_DIM)),
        ],
        out_specs=pl.BlockSpec((nb, c, DN_WIDTH), blk),
        out_shape=jax.ShapeDtypeStruct((nb, s, DN_WIDTH), BF16),
        scratch_shapes=[
            pltpu.VMEM((nb, SUBLANES, 3 * DN_WIDTH), F32),
            pltpu.VMEM((nb * DN_HEADS, DN_HEAD_DIM, DN_HEAD_DIM), F32),
        ],
        compiler_params=_params("arbitrary"),
        name="delta_net",
    )(qkv3, z3, ab3, conv_w, alog_row, dtb_row, dn_norm_w)


def _mix_kernel(x_ref, ydn_ref, ysg_ref, mod_ref, pn_ref, postn_ref, fpn_ref,
                wbg_ref, bbg_ref, wpd_ref, wps_ref, wout_ref, wrh_ref, wrl_ref, wsg_ref, wsu_ref, wsd_ref,
                x1_ref, hf_ref, sct_ref, sh_ref):
    d = x_ref.shape[1]
    x = x_ref[...]
    mod = mod_ref[0]
    hm = _rms(x) * pn_ref[...] * (1.0 + mod[1:2, :]) + mod[0:1, :]
    gates = _sigmoid(_dot(hm.astype(BF16), wbg_ref[...]) + bbg_ref[...])
    merged = gates[:, :d] * _dot(ydn_ref[...], wpd_ref[...]) + gates[:, d:] * _dot(ysg_ref[...], wps_ref[...])
    y = _dot(merged.astype(BF16), wout_ref[...])
    x1 = x + mod[2:3, :] * (_rms(y) * postn_ref[...])
    x1_ref[...] = x1
    hf = _rms(x1) * fpn_ref[...] * (1.0 + mod[4:5, :]) + mod[3:4, :]
    hf_ref[...] = hf
    hh, hl = _split2(hf)
    logits_t = _dot_nt(wrh_ref[...], hh) + (_dot_nt(wrh_ref[...], hl) + _dot_nt(wrl_ref[...], hh))
    sct_ref[...] = _sigmoid(logits_t)
    hid = _silu(_dot(hh, wsg_ref[...])) * _dot(hh, wsu_ref[...])
    sh_ref[...] = _dot(hid.astype(BF16), wsd_ref[...])


def _mix(x2, ydn, ysg, mod3, pre_norm, post_norm, ffn_pre_norm, wbg, bbg, wpd, wps, wout, wrh, wrl, wsg, wsu, wsd, seq, tm):
    t, d = x2.shape
    tiles_per_batch = seq // tm
    tok = lambda i: (i, 0)
    consts = [pre_norm, post_norm, ffn_pre_norm, wbg, bbg, wpd, wps, wout, wrh, wrl, wsg, wsu, wsd]
    return pl.pallas_call(
        _mix_kernel,
        grid=(t // tm,),
        in_specs=[
            pl.BlockSpec((tm, d), tok),
            pl.BlockSpec((tm, DN_WIDTH), tok),
            pl.BlockSpec((tm, SG_WIDTH), tok),
            pl.BlockSpec((1, 6, d), lambda i: (i // tiles_per_batch, 0, 0)),
        ] + [_const_spec(a.shape) for a in consts],
        out_specs=[
            pl.BlockSpec((tm, d), tok),
            pl.BlockSpec((tm, d), tok),
            pl.BlockSpec((N_EXPERTS, tm), lambda i: (0, i)),
            pl.BlockSpec((tm, d), tok),
        ],
        out_shape=[
            jax.ShapeDtypeStruct((t, d), F32),
            jax.ShapeDtypeStruct((t, d), F32),
            jax.ShapeDtypeStruct((N_EXPERTS, t), F32),
            jax.ShapeDtypeStruct((t, d), F32),
        ],
        compiler_params=_params("arbitrary"),
        name="mix",
    )(x2, ydn, ysg, mod3, *consts)


def _first_argmax(vals, idx):
    m = jnp.max(vals, axis=0, keepdims=True)
    first = jnp.min(jnp.where(vals == m, idx, jnp.int32(2 ** 30)), axis=0, keepdims=True)
    return m, first


def _route_kernel(sct_ref, bias_ref, idx_ref, wtt_ref, rank_ref, cnt_ref, carry_ref):
    tm = sct_ref.shape[1]

    @pl.when(pl.program_id(0) == 0)
    def _():
        carry_ref[...] = jnp.zeros_like(carry_ref)

    scores = sct_ref[...]
    sel = scores + bias_ref[...]
    erow = lax.broadcasted_iota(I32, (N_EXPERTS, tm), 0)
    grow = lax.broadcasted_iota(I32, (GROUP_SIZE, tm), 0)

    gs = []
    for gidx in range(N_EXPERT_GROUPS):
        sg = sel[gidx * GROUP_SIZE:(gidx + 1) * GROUP_SIZE]
        m1, first = _first_argmax(sg, grow)
        m2 = jnp.max(jnp.where(grow == first, NEG_INF, sg), axis=0, keepdims=True)
        gs.append(m1 + m2)
    gsc = jnp.concatenate(gs, axis=0)
    giota = lax.broadcasted_iota(I32, (N_EXPERT_GROUPS, tm), 0)
    gmask = jnp.zeros((N_EXPERT_GROUPS, tm), F32)
    cur = gsc
    for _ in range(TOPK_GROUPS):
        _, gi = _first_argmax(cur, giota)
        pick = giota == gi
        gmask = jnp.where(pick, 1.0, gmask)
        cur = jnp.where(pick, NEG_INF, cur)
    masked = jnp.concatenate(
        [jnp.where(gmask[gidx:gidx + 1, :] > 0.5, sel[gidx * GROUP_SIZE:(gidx + 1) * GROUP_SIZE], NEG_INF)
         for gidx in range(N_EXPERT_GROUPS)], axis=0)

    cur = masked
    idxs, wts = [], []
    onehot = jnp.zeros((N_EXPERTS, tm), F32)
    for _ in range(TOP_K):
        _, ei = _first_argmax(cur, erow)
        pick = erow == ei
        idxs.append(ei)
        wts.append(jnp.sum(jnp.where(pick, scores, 0.0), axis=0, keepdims=True))
        onehot = jnp.where(pick, 1.0, onehot)
        cur = jnp.where(pick, NEG_INF, cur)
    idx = jnp.concatenate(idxs, axis=0)
    wt = jnp.concatenate(wts, axis=0)
    wt = wt / jnp.sum(wt, axis=0, keepdims=True) * ROUTED_SCALE
    idx_ref[...] = idx
    wpad = jnp.concatenate([wt, jnp.zeros((LANES - TOP_K, tm), F32)], axis=0)
    wtt_ref[...] = wpad.T

    ti = lax.broadcasted_iota(I32, (tm, tm), 0)
    tj = lax.broadcasted_iota(I32, (tm, tm), 1)
    upper = (ti < tj).astype(BF16)
    before = _dot(onehot.astype(BF16), upper) + carry_ref[...]
    rank_ref[...] = jnp.concatenate(
        [jnp.sum(jnp.where(erow == idxs[kk], before, 0.0), axis=0, keepdims=True) for kk in range(TOP_K)],
        axis=0).astype(I32)
    total = carry_ref[...] + jnp.sum(onehot, axis=1, keepdims=True)
    carry_ref[...] = total
    cnt_ref[...] = total


def _route(sct, bias_col, tm):
    e, t = sct.shape
    tile = lambda i: (0, i)
    return pl.pallas_call(
        _route_kernel,
        grid=(t // tm,),
        in_specs=[pl.BlockSpec((e, tm), tile), _const_spec((e, 1))],
        out_specs=[
            pl.BlockSpec((TOP_K, tm), tile),
            pl.BlockSpec((tm, LANES), lambda i: (i, 0)),
            pl.BlockSpec((TOP_K, tm), tile),
            _const_spec((e, 1)),
        ],
        out_shape=[
            jax.ShapeDtypeStruct((TOP_K, t), I32),
            jax.ShapeDtypeStruct((t, LANES), F32),
            jax.ShapeDtypeStruct((TOP_K, t), I32),
            jax.ShapeDtypeStruct((e, 1), F32),
        ],
        scratch_shapes=[pltpu.VMEM((e, 1), F32)],
        compiler_params=_params("arbitrary"),
        name="route",
    )(sct, bias_col)


def _dest_kernel(cnt_ref, idx_ref, rank_ref, dest_ref, meta_ref):
    tm = idx_ref.shape[1]
    e = N_EXPERTS
    cnt = cnt_ref[...]
    padded = jnp.floor((cnt + (MOE_BLOCK - 1)) * (1.0 / MOE_BLOCK)) * MOE_BLOCK
    pw = jnp.broadcast_to(padded, (e, LANES))
    ri = lax.broadcasted_iota(I32, (e, e), 0)
    ci = lax.broadcasted_iota(I32, (e, e), 1)
    lower = (ri >= ci).astype(BF16)
    ph, pm, plo = _split3(pw)
    pends = _dot(lower, ph) + (_dot(lower, pm) + _dot(lower, plo))
    pstart = pends - pw
    erow = lax.broadcasted_iota(I32, (e, tm), 0)
    idx = idx_ref[...]
    rows = [jnp.sum(jnp.where(erow == idx[kk:kk + 1, :], pstart[:, 0:1], 0.0), axis=0, keepdims=True)
            for kk in range(TOP_K)]
    dest_ref[...] = jnp.concatenate(rows, axis=0).astype(I32) + rank_ref[...]
    lane = lax.broadcasted_iota(I32, (e, LANES), 1)
    blocks = jnp.where(lane == 0, pstart, pw) * (1.0 / MOE_BLOCK)
    pads = jnp.where(lane == 2, pstart + cnt, pw - cnt)
    meta_ref[...] = jnp.where(lane < 2, blocks, pads).astype(I32)


def _dest(cnt, idx, rank, tm):
    e = cnt.shape[0]
    t = idx.shape[1]
    tile = lambda i: (0, i)
    return pl.pallas_call(
        _dest_kernel,
        grid=(t // tm,),
        in_specs=[_const_spec((e, 1)), pl.BlockSpec((TOP_K, tm), tile), pl.BlockSpec((TOP_K, tm), tile)],
        out_specs=[pl.BlockSpec((TOP_K, tm), tile), _const_spec((e, LANES))],
        out_shape=[jax.ShapeDtypeStruct((TOP_K, t), I32), jax.ShapeDtypeStruct((e, LANES), I32)],
        compiler_params=_params("arbitrary"),
        name="dest",
    )(cnt, idx, rank)


def _row_copy(src, dst, sem):
    return pltpu.make_async_copy(src, dst, sem)


PAD_PIECES = tuple(MOE_BLOCK >> s for s in range(1, MOE_BLOCK.bit_length()))


def _zero_unassigned_rows(pad_start_ref, pad_rows_ref, xs_out, zero_ref, sem):
    zero_ref[...] = jnp.zeros_like(zero_ref)
    ne = pad_start_ref.shape[0]
    total = xs_out.shape[0] // MOE_BLOCK

    def piece(rows, pos):
        return pltpu.make_async_copy(zero_ref.at[pl.ds(0, rows), :], xs_out.at[pl.ds(pos, rows), :], sem)

    def pads(wait):
        def per_expert(ex, carry):
            pos = pad_start_ref[ex]
            pad = pad_rows_ref[ex]
            for rows in reversed(PAD_PIECES):
                @pl.when((pad & rows) != 0)
                def _():
                    if rows < SUBLANES:
                        copies = [piece(1, pos + r) for r in range(rows)]
                    else:
                        copies = [piece(rows, pl.multiple_of(pos, SUBLANES))]
                    for cp in copies:
                        cp.wait() if wait else cp.start()
                pos = pos + (pad & rows)
            return carry
        lax.fori_loop(0, ne, per_expert, 0)

    def tail(wait):
        used = (pad_start_ref[ne - 1] + pad_rows_ref[ne - 1]) // MOE_BLOCK

        def per_block(blk, carry):
            cp = piece(MOE_BLOCK, blk * MOE_BLOCK)
            cp.wait() if wait else cp.start()
            return carry
        lax.fori_loop(used, total, per_block, 0)

    pads(False)
    tail(False)
    pads(True)
    tail(True)


def _scatter_kernel(pad_start_ref, pad_rows_ref, dest_hbm, hf_ref, xs_out, dest_smem, zero_ref, sem_idx, sem_rows, sem_zero):
    tm = hf_ref.shape[0]
    i = pl.program_id(0)

    @pl.when(i == 0)
    def _():
        _zero_unassigned_rows(pad_start_ref, pad_rows_ref, xs_out, zero_ref, sem_zero)

    idx_copy = pltpu.make_async_copy(dest_hbm.at[i], dest_smem, sem_idx)
    idx_copy.start()
    idx_copy.wait()

    def issue(tok, carry):
        for kk in range(TOP_K):
            slot = dest_smem[kk * tm + tok]
            _row_copy(hf_ref.at[pl.ds(tok, 1), :], xs_out.at[pl.ds(slot, 1), :], sem_rows).start(priority=kk % 2)
        return carry

    lax.fori_loop(0, tm, issue, 0)
    for kk in range(TOP_K):
        _row_copy(hf_ref, xs_out.at[pl.ds(0, tm), :], sem_rows).wait()


def _scatter(pad_start, pad_rows, dest_tiles, hf, n_slots, tm):
    t, d = hf.shape
    n_tiles = t // tm
    return pl.pallas_call(
        _scatter_kernel,
        grid_spec=pltpu.PrefetchScalarGridSpec(
            num_scalar_prefetch=2,
            grid=(n_tiles,),
            in_specs=[pl.BlockSpec(memory_space=pl.ANY), pl.BlockSpec((tm, d), lambda i, ps, pr: (i, 0))],
            out_specs=pl.BlockSpec(memory_space=pl.ANY),
            scratch_shapes=[
                pltpu.SMEM((TOP_K * tm,), I32),
                pltpu.VMEM((MOE_BLOCK, d), hf.dtype),
                pltpu.SemaphoreType.DMA,
                pltpu.SemaphoreType.DMA,
                pltpu.SemaphoreType.DMA,
            ],
        ),
        out_shape=jax.ShapeDtypeStruct((n_slots, d), hf.dtype),
        compiler_params=_params("arbitrary"),
        name="scatter",
    )(pad_start, pad_rows, dest_tiles, hf)


MOE_BUFFERS = 3


def _moe_kernel(bstart_ref, nblk_ref, xs_hbm, wg_ref, wu_ref, wd_ref, ys_hbm,
                xbuf, ybuf, wgb_ref, wub_ref, wdb_ref, sem_in, sem_out):
    ex = pl.program_id(0)
    last = pl.num_programs(0) - 1
    n = nblk_ref[ex]
    b0 = bstart_ref[ex]
    used = bstart_ref[last] + nblk_ref[last]

    def in_copy(g):
        slot = lax.rem(g, MOE_BUFFERS)
        return pltpu.make_async_copy(xs_hbm.at[pl.ds(g * MOE_BLOCK, MOE_BLOCK), :], xbuf.at[slot], sem_in.at[slot])

    def out_copy(g):
        slot = lax.rem(g, MOE_BUFFERS)
        return pltpu.make_async_copy(ybuf.at[slot], ys_hbm.at[pl.ds(g * MOE_BLOCK, MOE_BLOCK), :], sem_out.at[slot])

    @pl.when(ex == 0)
    def _():
        for s in range(MOE_BUFFERS - 1):
            @pl.when(s < used)
            def _():
                in_copy(s).start()

    wgb_ref[...] = wg_ref[0].astype(BF16)
    wub_ref[...] = wu_ref[0].astype(BF16)
    wdb_ref[...] = wd_ref[0].astype(BF16)

    def body(g, carry):
        slot = lax.rem(g, MOE_BUFFERS)
        in_copy(g).wait()

        @pl.when(g + (MOE_BUFFERS - 1) < used)
        def _():
            in_copy(g + (MOE_BUFFERS - 1)).start()

        @pl.when(g >= MOE_BUFFERS)
        def _():
            out_copy(g - MOE_BUFFERS).wait()

        xb = xbuf[slot].astype(BF16)
        hid = _silu(_dot(xb, wgb_ref[...])) * _dot(xb, wub_ref[...])
        ybuf[slot] = _dot(hid.astype(BF16), wdb_ref[...])
        out_copy(g).start()
        return carry

    lax.fori_loop(b0, b0 + n, body, 0)

    @pl.when(ex == last)
    def _():
        for s in range(MOE_BUFFERS):
            @pl.when(s < used)
            def _():
                out_copy(used - 1 - s).wait()

        total = ys_hbm.shape[0] // MOE_BLOCK
        ybuf[0] = jnp.zeros(ybuf.shape[1:], ybuf.dtype)

        def tail_copy(blk):
            return pltpu.make_async_copy(ybuf.at[0], ys_hbm.at[pl.ds(blk * MOE_BLOCK, MOE_BLOCK), :], sem_out.at[0])

        def start_tail(blk, carry):
            tail_copy(blk).start()
            return carry

        def wait_tail(blk, carry):
            tail_copy(blk).wait()
            return carry

        lax.fori_loop(used, total, start_tail, 0)
        lax.fori_loop(used, total, wait_tail, 0)


def _moe(bstart, nblk_e, xs, w_gate, w_up, w_down):
    ns, d = xs.shape
    ne, _, eh = w_gate.shape
    wspec = lambda shape: pl.BlockSpec(shape, lambda ex, bs, nb: (ex, 0, 0))
    return pl.pallas_call(
        _moe_kernel,
        grid_spec=pltpu.PrefetchScalarGridSpec(
            num_scalar_prefetch=2,
            grid=(ne,),
            in_specs=[pl.BlockSpec(memory_space=pl.ANY), wspec((1, d, eh)), wspec((1, d, eh)), wspec((1, eh, d))],
            out_specs=pl.BlockSpec(memory_space=pl.ANY),
            scratch_shapes=[
                pltpu.VMEM((MOE_BUFFERS, MOE_BLOCK, d), F32),
                pltpu.VMEM((MOE_BUFFERS, MOE_BLOCK, d), F32),
                pltpu.VMEM((d, eh), BF16), pltpu.VMEM((d, eh), BF16), pltpu.VMEM((eh, d), BF16),
                pltpu.SemaphoreType.DMA((MOE_BUFFERS,)),
                pltpu.SemaphoreType.DMA((MOE_BUFFERS,)),
            ],
        ),
        out_shape=jax.ShapeDtypeStruct((ns, d), F32),
        compiler_params=_params("arbitrary"),
        name="moe",
    )(bstart, nblk_e, xs, w_gate, w_up, w_down)


def _combine_kernel(dest_hbm, ys_hbm, wtt_ref, x1_ref, sh_ref, mod_ref, postn_ref, o_ref, dest_smem0, dest_smem1, buf_ref, sem_idx, sem_rows):
    tc = x1_ref.shape[0]
    i = pl.program_id(0)
    n = pl.num_programs(0)
    dest_smem = (dest_smem0, dest_smem1)

    def idx_copy(tile, slot):
        return pltpu.make_async_copy(dest_hbm.at[tile], dest_smem[slot], sem_idx.at[slot])

    def issue_rows(slot):
        def issue(tok, carry):
            for kk in range(TOP_K):
                row = dest_smem[slot][kk * tc + tok]
                _row_copy(ys_hbm.at[pl.ds(row, 1), :], buf_ref.at[slot, kk, pl.ds(tok, 1), :],
                          sem_rows.at[slot]).start(priority=kk % 2)
            return carry
        lax.fori_loop(0, tc, issue, 0)

    @pl.when(i == 0)
    def _():
        idx_copy(0, 0).start()
        idx_copy(0, 0).wait()

        @pl.when(n > 1)
        def _():
            idx_copy(1, 1).start()
        issue_rows(0)

    def step(cur):
        nxt = 1 - cur

        @pl.when(i + 1 < n)
        def _():
            idx_copy(i + 1, nxt).wait()

            @pl.when(i + 2 < n)
            def _():
                idx_copy(i + 2, cur).start()
            issue_rows(nxt)

        for kk in range(TOP_K):
            _row_copy(ys_hbm.at[pl.ds(0, tc), :], buf_ref.at[cur, kk], sem_rows.at[cur]).wait()

        wtt = wtt_ref[...]
        acc = buf_ref[cur, 0] * wtt[:, 0:1]
        for kk in range(1, TOP_K):
            acc = acc + buf_ref[cur, kk] * wtt[:, kk:kk + 1]
        y = acc + sh_ref[...]
        mod = mod_ref[0]
        o_ref[...] = x1_ref[...] + mod[5:6, :] * (_rms(y) * postn_ref[...])

    parity = lax.rem(i, 2)
    for cur in range(2):
        @pl.when(parity == cur)
        def _():
            step(cur)


def _combine(dest_tiles, ys, wtt, x1, shared, mod3, ffn_post_norm, seq, tc):
    t, d = x1.shape
    tiles_per_batch = seq // tc
    tok = lambda i: (i, 0)
    return pl.pallas_call(
        _combine_kernel,
        grid=(t // tc,),
        in_specs=[
            pl.BlockSpec(memory_space=pl.ANY),
            pl.BlockSpec(memory_space=pl.ANY),
            pl.BlockSpec((tc, LANES), tok),
            pl.BlockSpec((tc, d), tok),
            pl.BlockSpec((tc, d), tok),
            pl.BlockSpec((1, 6, d), lambda i: (i // tiles_per_batch, 0, 0)),
            _const_spec((1, d)),
        ],
        out_specs=pl.BlockSpec((tc, d), tok),
        out_shape=jax.ShapeDtypeStruct((t, d), F32),
        scratch_shapes=[
            pltpu.SMEM((TOP_K * tc,), I32),
            pltpu.SMEM((TOP_K * tc,), I32),
            pltpu.VMEM((2, TOP_K, tc, d), F32),
            pltpu.SemaphoreType.DMA((2,)),
            pltpu.SemaphoreType.DMA((2,)),
        ],
        compiler_params=_params("arbitrary"),
        name="combine",
    )(dest_tiles, ys, wtt, x1, shared, mod3, ffn_post_norm)


def _tile(n, want):
    t = min(n, want)
    assert n % t == 0
    return t


def _dest_tiles(dest, tile):
    k, t = dest.shape
    return dest.reshape(k, t // tile, tile).transpose(1, 0, 2).reshape(t // tile, k * tile)


def kernel(x, c, ada_w, ada_b, mix_pre_norm, mix_post_norm, w_in, conv_w, a_log, dt_bias, dn_norm_w, sg_ln_w, sg_ln_b, sg_w, sg_b, w_branch_gate, b_branch_gate, w_proj_dn, w_proj_sg, w_out, ffn_pre_norm, ffn_post_norm, w_router, router_bias, w_exp_gate, w_exp_up, w_exp_down, w_sh_gate, w_sh_up, w_sh_down):
    nb, seq, d = x.shape
    depth = ada_w.shape[0]
    t = nb * seq
    tm = _tile(seq, 512)
    tr = _tile(t, 512)
    tsc = _tile(t, 512)
    tcm = _tile(seq, 256)
    nblk = -(-t * TOP_K // MOE_BLOCK) + N_EXPERTS
    row = lambda v: v.reshape(1, -1)
    pad_lanes = lambda v: jnp.pad(v.astype(F32), (0, LANES - v.shape[0])).reshape(1, LANES)

    x2 = x.reshape(t, d)
    for l in range(depth):
        mod3 = _ada(c, ada_w[l], ada_b[l]).reshape(nb, 6, d)

        wi = w_in[l]
        qkvz, ab_cols, uv = wi[:, :4 * DN_WIDTH], wi[:, 4 * DN_WIDTH:4 * DN_WIDTH + 2 * DN_HEADS], wi[:, 4 * DN_WIDTH + 2 * DN_HEADS:]
        w1 = jnp.concatenate([qkvz, ab_cols, jnp.zeros((d, LANES - 2 * DN_HEADS), wi.dtype)], axis=1).astype(BF16)
        qkv, z, ab, ysg = _in_proj(x2, mod3, row(mix_pre_norm[l]), w1, uv.astype(BF16), row(sg_ln_w[l]), row(sg_ln_b[l]),
                                   sg_w[l], sg_b[l].T, seq, tm)

        ydn = _delta_net(qkv.reshape(nb, seq, -1), z.reshape(nb, seq, -1), ab.reshape(nb, seq, -1), conv_w[l],
                         pad_lanes(a_log[l]), pad_lanes(dt_bias[l]), row(dn_norm_w[l]))

        wr_t = w_router[l].T
        wrh = wr_t.astype(BF16)
        wrl = (wr_t - wrh.astype(F32)).astype(BF16)
        x1, hf, sct, shared = _mix(
            x2, ydn.reshape(t, -1), ysg, mod3, row(mix_pre_norm[l]), row(mix_post_norm[l]), row(ffn_pre_norm[l]),
            w_branch_gate[l].astype(BF16), row(b_branch_gate[l]), w_proj_dn[l].astype(BF16), w_proj_sg[l].astype(BF16),
            w_out[l].astype(BF16), wrh, wrl, w_sh_gate[l].astype(BF16), w_sh_up[l].astype(BF16), w_sh_down[l].astype(BF16),
            seq, tm)

        idx, wtt, rank, cnt = _route(sct, router_bias[l].reshape(-1, 1), tr)
        dest, meta = _dest(cnt, idx, rank, tr)
        xs = _scatter(meta[:, 2], meta[:, 3], _dest_tiles(dest, tsc), hf, nblk * MOE_BLOCK, tsc)
        ys = _moe(meta[:, 0], meta[:, 1], xs, w_exp_gate[l], w_exp_up[l], w_exp_down[l])
        x2 = _combine(_dest_tiles(dest, tcm), ys, wtt, x1, shared, mod3, row(ffn_post_norm[l]), seq, tcm)
    return x2.reshape(nb, seq, d)
```

```python
import functools

import jax
import jax.numpy as jnp
from jax import lax
from jax.experimental import pallas as pl
from jax.experimental.pallas import tpu as pltpu

F32 = jnp.float32
BF16 = jnp.bfloat16
I32 = jnp.int32

D_MODEL = 1024
DN_HEADS = 4
DN_HEAD_DIM = 128
DN_WIDTH = DN_HEADS * DN_HEAD_DIM
DN_CHUNK = 64
CONV_WIDTH = 4
SG_GROUPS = 4
SG_GROUP_DIM = 128
SG_WIDTH = SG_GROUPS * SG_GROUP_DIM
SG_CHUNK = 128
N_EXPERTS = 256
N_EXPERT_GROUPS = 8
GROUP_SIZE = N_EXPERTS // N_EXPERT_GROUPS
TOPK_GROUPS = 4
TOP_K = 8
EXPERT_HIDDEN = 256
ROUTED_SCALE = 2.5
MOE_BLOCK = 256
EPS = 1e-6

LANES = 128
SUBLANES = 8
VMEM_LIMIT = 56 * 1024 * 1024
NEG_INF = float("-inf")


def _dot(a, b):
    return jnp.dot(a, b, preferred_element_type=F32)


def _dot_nt(a, b):
    return lax.dot_general(a, b, (((1,), (1,)), ((), ())), preferred_element_type=F32)


def _dot_tn(a, b):
    return lax.dot_general(a, b, (((0,), (0,)), ((), ())), preferred_element_type=F32)


def _split2(x):
    hi = x.astype(BF16)
    lo = (x - hi.astype(F32)).astype(BF16)
    return hi, lo


def _split3(x):
    hi = x.astype(BF16)
    r = x - hi.astype(F32)
    mid = r.astype(BF16)
    lo = (r - mid.astype(F32)).astype(BF16)
    return hi, mid, lo


def _dot_hp(a, b, dot=_dot):
    ah, al = _split2(a)
    bh, bl = _split2(b)
    return dot(ah, bh) + (dot(ah, bl) + dot(al, bh))


def _sigmoid(x):
    return 1.0 / (1.0 + jnp.exp(-x))


def _silu(x):
    return x * _sigmoid(x)


def _gelu(x):
    return 0.5 * x * (1.0 + lax.erf(x * (2.0 ** -0.5)))


def _softplus(x):
    return jnp.maximum(x, 0.0) + jnp.log1p(jnp.exp(-jnp.abs(x)))


def _rms(x):
    return x * lax.rsqrt(jnp.mean(x * x, axis=-1, keepdims=True) + EPS)


def _params(*sem):
    return pltpu.CompilerParams(dimension_semantics=sem, vmem_limit_bytes=VMEM_LIMIT)


def _const_spec(shape):
    nd = len(shape)
    return pl.BlockSpec(shape, lambda *_: (0,) * nd)


def _ada_kernel(c_ref, w_ref, b_ref, o_ref):
    cs = _silu(c_ref[...])
    o_ref[...] = _dot_hp(cs, w_ref[...]) + b_ref[...]


def _ada(c, ada_w, ada_b):
    b, d = c.shape
    n = ada_w.shape[1]
    tn = d
    return pl.pallas_call(
        _ada_kernel,
        grid=(n // tn,),
        in_specs=[_const_spec((b, d)), pl.BlockSpec((d, tn), lambda j: (0, j)), pl.BlockSpec((1, tn), lambda j: (0, j))],
        out_specs=pl.BlockSpec((b, tn), lambda j: (0, j)),
        out_shape=jax.ShapeDtypeStruct((b, n), F32),
        compiler_params=_params("arbitrary"),
        name="ada",
    )(c, ada_w, ada_b.reshape(1, n))


W1_COLS = 4 * DN_WIDTH + LANES


def _in_kernel(x_ref, mod_ref, pn_ref, w1_ref, wuv_ref, lnw_ref, lnb_ref, sgw_ref, sgbt_ref,
               qkv_ref, z_ref, ab_ref, ysg_ref):
    tm = x_ref.shape[0]
    mod = mod_ref[0]
    hm = _rms(x_ref[...]) * pn_ref[...] * (1.0 + mod[1:2, :]) + mod[0:1, :]
    hb = hm.astype(BF16)
    p1 = _dot(hb, w1_ref[...])
    qkv_ref[...] = p1[:, :3 * DN_WIDTH]
    z_ref[...] = p1[:, 3 * DN_WIDTH:4 * DN_WIDTH]
    ab_ref[...] = p1[:, 4 * DN_WIDTH:]
    uv = _dot(hb, wuv_ref[...])
    u = _gelu(uv[:, :SG_WIDTH])
    vg = _gelu(uv[:, SG_WIDTH:])
    mu = jnp.mean(vg, axis=-1, keepdims=True)
    dv = vg - mu
    var = jnp.mean(dv * dv, axis=-1, keepdims=True)
    vgn = (dv * lax.rsqrt(var + EPS) * lnw_ref[...] + lnb_ref[...]).astype(BF16)
    row = lax.broadcasted_iota(I32, (SG_CHUNK, SG_CHUNK), 0)
    col = lax.broadcasted_iota(I32, (SG_CHUNK, SG_CHUNK), 1)
    tril = row >= col
    for g in range(SG_GROUPS):
        wg = jnp.where(tril, sgw_ref[g], 0.0).astype(BF16)
        bg = sgbt_ref[:, g:g + 1]
        cs = slice(g * SG_GROUP_DIM, (g + 1) * SG_GROUP_DIM)
        for n in range(tm // SG_CHUNK):
            rs = slice(n * SG_CHUNK, (n + 1) * SG_CHUNK)
            mixed = _dot(wg, vgn[rs, cs]) + bg
            ysg_ref[rs, cs] = (u[rs, cs] * mixed).astype(BF16)


def _in_proj(x2, mod3, pre_norm, w1, wuv, sg_ln_w, sg_ln_b, sg_w, sg_bt, seq, tm):
    t, d = x2.shape
    tiles_per_batch = seq // tm
    tok = lambda i: (i, 0)
    return pl.pallas_call(
        _in_kernel,
        grid=(t // tm,),
        in_specs=[
            pl.BlockSpec((tm, d), tok),
            pl.BlockSpec((1, 6, d), lambda i: (i // tiles_per_batch, 0, 0)),
            _const_spec((1, d)),
            _const_spec(w1.shape),
            _const_spec(wuv.shape),
            _const_spec((1, SG_WIDTH)),
            _const_spec((1, SG_WIDTH)),
            _const_spec(sg_w.shape),
            _const_spec(sg_bt.shape),
        ],
        out_specs=[
            pl.BlockSpec((tm, 3 * DN_WIDTH), tok),
            pl.BlockSpec((tm, DN_WIDTH), tok),
            pl.BlockSpec((tm, LANES), tok),
            pl.BlockSpec((tm, SG_WIDTH), tok),
        ],
        out_shape=[
            jax.ShapeDtypeStruct((t, 3 * DN_WIDTH), F32),
            jax.ShapeDtypeStruct((t, DN_WIDTH), F32),
            jax.ShapeDtypeStruct((t, LANES), F32),
            jax.ShapeDtypeStruct((t, SG_WIDTH), BF16),
        ],
        compiler_params=_params("arbitrary"),
        name="in_proj",
    )(x2, mod3, pre_norm, w1, wuv, sg_ln_w, sg_ln_b, sg_w, sg_bt)


def _hp_parts(ap, bp, dot=_dot):
    return dot(ap[0], bp[0]) + (dot(ap[0], bp[1]) + dot(ap[1], bp[0]))


def _unit_lower_inverses(a_list):
    c = a_list[0].shape[0]
    i = lax.broadcasted_iota(I32, (c, c), 0)
    j = lax.broadcasted_iota(I32, (c, c), 1)
    eye = (i == j).astype(F32)
    first = (i == j + 1) & ((i & 1) == 1)
    d_list = [eye - jnp.where(first, a, 0.0) for a in a_list]
    b = 2
    while b < c:
        shift = b.bit_length()
        off = ((i >> shift) == (j >> shift)) & ((i & b) != 0) & ((j & b) == 0)
        a_parts = [_split2(jnp.where(off, a, 0.0)) for a in a_list]
        d_parts = [_split2(d) for d in d_list]
        t_list = [_hp_parts(dp, ap) for dp, ap in zip(d_parts, a_parts)]
        t_parts = [_split2(t) for t in t_list]
        d_list = [d - _hp_parts(tp, dp) for d, tp, dp in zip(d_list, t_parts, d_parts)]
        b *= 2
    return d_list


def _dn_kernel(qkv_ref, z_ref, ab_ref, convw_ref, alog_ref, dtb_ref, nw_ref, y_ref, carry_ref, state_ref):
    nb, c = qkv_ref.shape[0], qkv_ref.shape[1]
    hd = DN_HEAD_DIM
    probs = [(b, h) for b in range(nb) for h in range(DN_HEADS)]

    @pl.when(pl.program_id(0) == 0)
    def _():
        carry_ref[...] = jnp.zeros_like(carry_ref)
        state_ref[...] = jnp.zeros_like(state_ref)

    ab = ab_ref[...].reshape(nb * c, LANES)
    g = -jnp.exp(alog_ref[...]) * _softplus(ab + dtb_ref[...])
    beta_all = _sigmoid(ab)
    ri = lax.broadcasted_iota(I32, (nb * c, nb * c), 0)
    ci = lax.broadcasted_iota(I32, (nb * c, nb * c), 1)
    tri = ((ri >= ci) & ((ri // c) == (ci // c))).astype(BF16)
    gh, gm, gl = _split3(g)
    gc = _dot(tri, gh) + (_dot(tri, gm) + _dot(tri, gl))
    gct = gc.T

    i = lax.broadcasted_iota(I32, (c, c), 0)
    j = lax.broadcasted_iota(I32, (c, c), 1)
    causal = i >= j
    strict = i > j
    row8 = lax.broadcasted_iota(I32, (SUBLANES, 3 * DN_WIDTH), 0)

    acts, gcs, e_gcs, e_rems, e_lasts = [], [], [], [], []
    for b in range(nb):
        xc = qkv_ref[b]
        prev = carry_ref[b]
        acc = xc * convw_ref[CONV_WIDTH - 1:CONV_WIDTH, :]
        for s in range(1, CONV_WIDTH):
            rolled = pltpu.roll(xc, s, axis=0)
            top = jnp.where(row8 < s, pltpu.roll(prev, s, axis=0), rolled[:SUBLANES])
            shifted = jnp.concatenate([top, rolled[SUBLANES:]], axis=0)
            acc = acc + shifted * convw_ref[CONV_WIDTH - 1 - s:CONV_WIDTH - s, :]
        carry_ref[b] = xc[c - SUBLANES:]
        acts.append(_silu(acc))
        gcb = gc[b * c:(b + 1) * c]
        g_last = gcb[c - 1:c, :]
        gcs.append(gcb)
        e_gcs.append(jnp.exp(gcb))
        e_rems.append(jnp.exp(g_last - gcb))
        e_lasts.append(jnp.exp(g_last))

    qn, kn, knb, kb, vb, decay = {}, {}, {}, {}, {}, {}
    for p in probs:
        b, h = p
        act = acts[b]
        q = act[:, h * hd:(h + 1) * hd]
        k = act[:, DN_WIDTH + h * hd:DN_WIDTH + (h + 1) * hd]
        v = act[:, 2 * DN_WIDTH + h * hd:2 * DN_WIDTH + (h + 1) * hd]
        qn[p] = q * lax.rsqrt(jnp.sum(q * q, axis=-1, keepdims=True) + EPS) * (hd ** -0.5)
        kn[p] = k * lax.rsqrt(jnp.sum(k * k, axis=-1, keepdims=True) + EPS)
        beta = beta_all[b * c:(b + 1) * c, DN_HEADS + h:DN_HEADS + h + 1]
        diff = gcs[b][:, h:h + 1] - gct[h:h + 1, b * c:(b + 1) * c]
        decay[p] = jnp.where(causal, jnp.exp(jnp.where(causal, diff, 0.0)), 0.0)
        kb[p] = kn[p] * beta
        vb[p] = v * beta
        knb[p] = kn[p].astype(BF16)

    kk = {p: _dot_nt(kb[p].astype(BF16), knb[p]) for p in probs}
    qk = {p: _dot_nt(qn[p].astype(BF16), knb[p]) for p in probs}
    tinv = dict(zip(probs, _unit_lower_inverses([jnp.where(strict, kk[p] * decay[p], 0.0) for p in probs])))
    t_parts = {p: _split2(tinv[p]) for p in probs}
    rhs = {p: jnp.concatenate([vb[p], kb[p] * e_gcs[p[0]][:, p[1]:p[1] + 1]], axis=1) for p in probs}
    r_parts = {p: _split2(rhs[p]) for p in probs}
    sol = {p: _hp_parts(t_parts[p], r_parts[p]) for p in probs}

    lhs = {p: jnp.concatenate([sol[p][:, hd:], qn[p] * e_gcs[p[0]][:, p[1]:p[1] + 1]], axis=0).astype(BF16) for p in probs}
    state = {p: state_ref[p[0] * DN_HEADS + p[1]] for p in probs}
    ws = {p: _dot(lhs[p], state[p].astype(BF16)) for p in probs}
    vnb = {p: (sol[p][:, :hd] - ws[p][:c]).astype(BF16) for p in probs}
    qkm = {p: jnp.where(causal, qk[p] * decay[p], 0.0).astype(BF16) for p in probs}
    o = {p: ws[p][c:] + _dot(qkm[p], vnb[p]) for p in probs}
    kd = {p: (kn[p] * e_rems[p[0]][:, p[1]:p[1] + 1]).astype(BF16) for p in probs}
    upd = {p: _dot_tn(kd[p], vnb[p]) for p in probs}
    for p in probs:
        b, h = p
        state_ref[b * DN_HEADS + h] = state[p] * e_lasts[b][:, h:h + 1] + upd[p]
        zh = z_ref[b, :, h * hd:(h + 1) * hd]
        y_ref[b, :, h * hd:(h + 1) * hd] = (_rms(o[p]) * nw_ref[...] * _silu(zh)).astype(BF16)


def _delta_net(qkv3, z3, ab3, conv_w, alog_row, dtb_row, dn_norm_w):
    nb, s, _ = qkv3.shape
    c = DN_CHUNK
    blk = lambda n: (0, n, 0)
    return pl.pallas_call(
        _dn_kernel,
        grid=(s // c,),
        in_specs=[
            pl.BlockSpec((nb, c, 3 * DN_WIDTH), blk),
            pl.BlockSpec((nb, c, DN_WIDTH), blk),
            pl.BlockSpec((nb, c, LANES), blk),
            _const_spec(conv_w.shape),
            _const_spec((1, LANES)),
            _const_spec((1, LANES)),
            _const_spec((1, DN_HEAD_DIM)),
        ],
        out_specs=pl.BlockSpec((nb, c, DN_WIDTH), blk),
        out_shape=jax.ShapeDtypeStruct((nb, s, DN_WIDTH), BF16),
        scratch_shapes=[
            pltpu.VMEM((nb, SUBLANES, 3 * DN_WIDTH), F32),
            pltpu.VMEM((nb * DN_HEADS, DN_HEAD_DIM, DN_HEAD_DIM), F32),
        ],
        compiler_params=_params("arbitrary"),
        name="delta_net",
    )(qkv3, z3, ab3, conv_w, alog_row, dtb_row, dn_norm_w)


def _mix_kernel(x_ref, ydn_ref, ysg_ref, mod_ref, pn_ref, postn_ref, fpn_ref,
                wbg_ref, bbg_ref, wpd_ref, wps_ref, wout_ref, wrh_ref, wrl_ref, wsg_ref, wsu_ref, wsd_ref,
                x1_ref, hf_ref, sct_ref, sh_ref):
    d = x_ref.shape[1]
    x = x_ref[...]
    mod = mod_ref[0]
    hm = _rms(x) * pn_ref[...] * (1.0 + mod[1:2, :]) + mod[0:1, :]
    gates = _sigmoid(_dot(hm.astype(BF16), wbg_ref[...]) + bbg_ref[...])
    merged = gates[:, :d] * _dot(ydn_ref[...], wpd_ref[...]) + gates[:, d:] * _dot(ysg_ref[...], wps_ref[...])
    y = _dot(merged.astype(BF16), wout_ref[...])
    x1 = x + mod[2:3, :] * (_rms(y) * postn_ref[...])
    x1_ref[...] = x1
    hf = _rms(x1) * fpn_ref[...] * (1.0 + mod[4:5, :]) + mod[3:4, :]
    for j in range(d // LANES):
        hf_ref[:, j, :] = hf[:, j * LANES:(j + 1) * LANES]
    hh, hl = _split2(hf)
    logits_t = _dot_nt(wrh_ref[...], hh) + (_dot_nt(wrh_ref[...], hl) + _dot_nt(wrl_ref[...], hh))
    sct_ref[...] = _sigmoid(logits_t)
    hid = _silu(_dot(hh, wsg_ref[...])) * _dot(hh, wsu_ref[...])
    sh_ref[...] = _dot(hid.astype(BF16), wsd_ref[...])


def _mix(x2, ydn, ysg, mod3, pre_norm, post_norm, ffn_pre_norm, wbg, bbg, wpd, wps, wout, wrh, wrl, wsg, wsu, wsd, seq, tm):
    t, d = x2.shape
    tiles_per_batch = seq // tm
    tok = lambda i: (i, 0)
    consts = [pre_norm, post_norm, ffn_pre_norm, wbg, bbg, wpd, wps, wout, wrh, wrl, wsg, wsu, wsd]
    return pl.pallas_call(
        _mix_kernel,
        grid=(t // tm,),
        in_specs=[
            pl.BlockSpec((tm, d), tok),
            pl.BlockSpec((tm, DN_WIDTH), tok),
            pl.BlockSpec((tm, SG_WIDTH), tok),
            pl.BlockSpec((1, 6, d), lambda i: (i // tiles_per_batch, 0, 0)),
        ] + [_const_spec(a.shape) for a in consts],
        out_specs=[
            pl.BlockSpec((tm, d), tok),
            pl.BlockSpec((tm, d // LANES, LANES), lambda i: (i, 0, 0)),
            pl.BlockSpec((N_EXPERTS, tm), lambda i: (0, i)),
            pl.BlockSpec((tm, d), tok),
        ],
        out_shape=[
            jax.ShapeDtypeStruct((t, d), F32),
            jax.ShapeDtypeStruct((t, d // LANES, LANES), F32),
            jax.ShapeDtypeStruct((N_EXPERTS, t), F32),
            jax.ShapeDtypeStruct((t, d), F32),
        ],
        compiler_params=_params("arbitrary"),
        name="mix",
    )(x2, ydn, ysg, mod3, *consts)


def _first_argmax(vals, idx):
    m = jnp.max(vals, axis=0, keepdims=True)
    first = jnp.min(jnp.where(vals == m, idx, jnp.int32(2 ** 30)), axis=0, keepdims=True)
    return m, first


def _route_kernel(sct_ref, bias_ref, idx_ref, wtt_ref, rank_ref, cnt_ref, carry_ref):
    tm = sct_ref.shape[1]

    @pl.when(pl.program_id(0) == 0)
    def _():
        carry_ref[...] = jnp.zeros_like(carry_ref)

    scores = sct_ref[...]
    sel = scores + bias_ref[...]
    erow = lax.broadcasted_iota(I32, (N_EXPERTS, tm), 0)
    grow = lax.broadcasted_iota(I32, (GROUP_SIZE, tm), 0)

    gs = []
    for gidx in range(N_EXPERT_GROUPS):
        sg = sel[gidx * GROUP_SIZE:(gidx + 1) * GROUP_SIZE]
        m1, first = _first_argmax(sg, grow)
        m2 = jnp.max(jnp.where(grow == first, NEG_INF, sg), axis=0, keepdims=True)
        gs.append(m1 + m2)
    gsc = jnp.concatenate(gs, axis=0)
    giota = lax.broadcasted_iota(I32, (N_EXPERT_GROUPS, tm), 0)
    gmask = jnp.zeros((N_EXPERT_GROUPS, tm), F32)
    cur = gsc
    for _ in range(TOPK_GROUPS):
        _, gi = _first_argmax(cur, giota)
        pick = giota == gi
        gmask = jnp.where(pick, 1.0, gmask)
        cur = jnp.where(pick, NEG_INF, cur)
    masked = jnp.concatenate(
        [jnp.where(gmask[gidx:gidx + 1, :] > 0.5, sel[gidx * GROUP_SIZE:(gidx + 1) * GROUP_SIZE], NEG_INF)
         for gidx in range(N_EXPERT_GROUPS)], axis=0)

    cur = masked
    idxs, wts = [], []
    onehot = jnp.zeros((N_EXPERTS, tm), F32)
    for _ in range(TOP_K):
        _, ei = _first_argmax(cur, erow)
        pick = erow == ei
        idxs.append(ei)
        wts.append(jnp.sum(jnp.where(pick, scores, 0.0), axis=0, keepdims=True))
        onehot = jnp.where(pick, 1.0, onehot)
        cur = jnp.where(pick, NEG_INF, cur)
    idx = jnp.concatenate(idxs, axis=0)
    wt = jnp.concatenate(wts, axis=0)
    wt = wt / jnp.sum(wt, axis=0, keepdims=True) * ROUTED_SCALE
    idx_ref[...] = idx
    wpad = jnp.concatenate([wt, jnp.zeros((LANES - TOP_K, tm), F32)], axis=0)
    wtt_ref[...] = wpad.T

    ti = lax.broadcasted_iota(I32, (tm, tm), 0)
    tj = lax.broadcasted_iota(I32, (tm, tm), 1)
    upper = (ti < tj).astype(BF16)
    before = _dot(onehot.astype(BF16), upper) + carry_ref[...]
    rank_ref[...] = jnp.concatenate(
        [jnp.sum(jnp.where(erow == idxs[kk], before, 0.0), axis=0, keepdims=True) for kk in range(TOP_K)],
        axis=0).astype(I32)
    total = carry_ref[...] + jnp.sum(onehot, axis=1, keepdims=True)
    carry_ref[...] = total
    cnt_ref[...] = total


def _route(sct, bias_col, tm):
    e, t = sct.shape
    tile = lambda i: (0, i)
    return pl.pallas_call(
        _route_kernel,
        grid=(t // tm,),
        in_specs=[pl.BlockSpec((e, tm), tile), _const_spec((e, 1))],
        out_specs=[
            pl.BlockSpec((TOP_K, tm), tile),
            pl.BlockSpec((tm, LANES), lambda i: (i, 0)),
            pl.BlockSpec((TOP_K, tm), tile),
            _const_spec((e, 1)),
        ],
        out_shape=[
            jax.ShapeDtypeStruct((TOP_K, t), I32),
            jax.ShapeDtypeStruct((t, LANES), F32),
            jax.ShapeDtypeStruct((TOP_K, t), I32),
            jax.ShapeDtypeStruct((e, 1), F32),
        ],
        scratch_shapes=[pltpu.VMEM((e, 1), F32)],
        compiler_params=_params("arbitrary"),
        name="route",
    )(sct, bias_col)


def _dest_kernel(cnt_ref, idx_ref, rank_ref, dest_ref, meta_ref):
    tm = idx_ref.shape[1]
    e = N_EXPERTS
    cnt = cnt_ref[...]
    padded = jnp.floor((cnt + (MOE_BLOCK - 1)) * (1.0 / MOE_BLOCK)) * MOE_BLOCK
    pw = jnp.broadcast_to(padded, (e, LANES))
    ri = lax.broadcasted_iota(I32, (e, e), 0)
    ci = lax.broadcasted_iota(I32, (e, e), 1)
    lower = (ri >= ci).astype(BF16)
    ph, pm, plo = _split3(pw)
    pends = _dot(lower, ph) + (_dot(lower, pm) + _dot(lower, plo))
    pstart = pends - pw
    erow = lax.broadcasted_iota(I32, (e, tm), 0)
    idx = idx_ref[...]
    rows = [jnp.sum(jnp.where(erow == idx[kk:kk + 1, :], pstart[:, 0:1], 0.0), axis=0, keepdims=True)
            for kk in range(TOP_K)]
    dest_ref[...] = jnp.concatenate(rows, axis=0).astype(I32) + rank_ref[...]
    lane = lax.broadcasted_iota(I32, (e, LANES), 1)
    blocks = jnp.where(lane == 0, pstart, pw) * (1.0 / MOE_BLOCK)
    pads = jnp.where(lane == 2, pstart + cnt, pw - cnt)
    meta_ref[...] = jnp.where(lane < 2, blocks, pads).astype(I32)


def _dest(cnt, idx, rank, tm):
    e = cnt.shape[0]
    t = idx.shape[1]
    tile = lambda i: (0, i)
    return pl.pallas_call(
        _dest_kernel,
        grid=(t // tm,),
        in_specs=[_const_spec((e, 1)), pl.BlockSpec((TOP_K, tm), tile), pl.BlockSpec((TOP_K, tm), tile)],
        out_specs=[pl.BlockSpec((TOP_K, tm), tile), _const_spec((e, LANES))],
        out_shape=[jax.ShapeDtypeStruct((TOP_K, t), I32), jax.ShapeDtypeStruct((e, LANES), I32)],
        compiler_params=_params("arbitrary"),
        name="dest",
    )(cnt, idx, rank)


def _row_copy(src, dst, sem):
    return pltpu.make_async_copy(src, dst, sem)


PAD_PIECES = tuple(MOE_BLOCK >> s for s in range(1, MOE_BLOCK.bit_length()))


def _zero_unassigned_rows(pad_start_ref, pad_rows_ref, xs_out, zero_ref, sem):
    zero_ref[...] = jnp.zeros_like(zero_ref)
    ne = pad_start_ref.shape[0]
    total = xs_out.shape[0] // MOE_BLOCK

    def piece(rows, pos):
        return pltpu.make_async_copy(zero_ref.at[pl.ds(0, rows)], xs_out.at[pl.ds(pos, rows)], sem)

    def pads(wait):
        def per_expert(ex, carry):
            pos = pad_start_ref[ex]
            pad = pad_rows_ref[ex]
            for rows in PAD_PIECES:
                @pl.when((pad & rows) != 0)
                def _():
                    piece(rows, pos).wait() if wait else piece(rows, pos).start()
                pos = pos + (pad & rows)
            return carry
        lax.fori_loop(0, ne, per_expert, 0)

    def tail(wait):
        used = (pad_start_ref[ne - 1] + pad_rows_ref[ne - 1]) // MOE_BLOCK

        def per_block(blk, carry):
            cp = piece(MOE_BLOCK, blk * MOE_BLOCK)
            cp.wait() if wait else cp.start()
            return carry
        lax.fori_loop(used, total, per_block, 0)

    pads(False)
    tail(False)
    pads(True)
    tail(True)


def _scatter_kernel(pad_start_ref, pad_rows_ref, dest_hbm, hf_ref, xs_out, dest_smem, zero_ref, sem_idx, sem_rows, sem_zero):
    tm = hf_ref.shape[0]
    i = pl.program_id(0)

    @pl.when(i == 0)
    def _():
        _zero_unassigned_rows(pad_start_ref, pad_rows_ref, xs_out, zero_ref, sem_zero)

    idx_copy = pltpu.make_async_copy(dest_hbm.at[i], dest_smem, sem_idx)
    idx_copy.start()
    idx_copy.wait()

    def issue(tok, carry):
        for kk in range(TOP_K):
            slot = dest_smem[kk * tm + tok]
            _row_copy(hf_ref.at[pl.ds(tok, 1)], xs_out.at[pl.ds(slot, 1)], sem_rows).start(priority=kk % 2)
        return carry

    lax.fori_loop(0, tm, issue, 0)
    for kk in range(TOP_K):
        _row_copy(hf_ref, xs_out.at[pl.ds(0, tm)], sem_rows).wait()


def _scatter(pad_start, pad_rows, dest_tiles, hf, n_slots, tm):
    t = hf.shape[0]
    row = hf.shape[1:]
    n_tiles = t // tm
    return pl.pallas_call(
        _scatter_kernel,
        grid_spec=pltpu.PrefetchScalarGridSpec(
            num_scalar_prefetch=2,
            grid=(n_tiles,),
            in_specs=[pl.BlockSpec(memory_space=pl.ANY), pl.BlockSpec((tm,) + row, lambda i, ps, pr: (i, 0, 0))],
            out_specs=pl.BlockSpec(memory_space=pl.ANY),
            scratch_shapes=[
                pltpu.SMEM((TOP_K * tm,), I32),
                pltpu.VMEM((MOE_BLOCK,) + row, hf.dtype),
                pltpu.SemaphoreType.DMA,
                pltpu.SemaphoreType.DMA,
                pltpu.SemaphoreType.DMA,
            ],
        ),
        out_shape=jax.ShapeDtypeStruct((n_slots,) + row, hf.dtype),
        compiler_params=_params("arbitrary"),
        name="scatter",
    )(pad_start, pad_rows, dest_tiles, hf)


MOE_BUFFERS = 3


def _moe_kernel(bstart_ref, nblk_ref, xs_hbm, wg_ref, wu_ref, wd_ref, ys_hbm,
                xbuf, ybuf, wgb_ref, wub_ref, wdb_ref, sem_in, sem_out):
    ex = pl.program_id(0)
    last = pl.num_programs(0) - 1
    n = nblk_ref[ex]
    b0 = bstart_ref[ex]
    used = bstart_ref[last] + nblk_ref[last]

    def in_copy(g):
        slot = lax.rem(g, MOE_BUFFERS)
        return pltpu.make_async_copy(xs_hbm.at[pl.ds(g * MOE_BLOCK, MOE_BLOCK)], xbuf.at[slot], sem_in.at[slot])

    def out_copy(g):
        slot = lax.rem(g, MOE_BUFFERS)
        return pltpu.make_async_copy(ybuf.at[slot], ys_hbm.at[pl.ds(g * MOE_BLOCK, MOE_BLOCK)], sem_out.at[slot])

    @pl.when(ex == 0)
    def _():
        for s in range(MOE_BUFFERS - 1):
            @pl.when(s < used)
            def _():
                in_copy(s).start()

    wgb_ref[...] = wg_ref[0].astype(BF16)
    wub_ref[...] = wu_ref[0].astype(BF16)
    wdb_ref[...] = wd_ref[0].astype(BF16)

    def body(g, carry):
        slot = lax.rem(g, MOE_BUFFERS)
        in_copy(g).wait()

        @pl.when(g + (MOE_BUFFERS - 1) < used)
        def _():
            in_copy(g + (MOE_BUFFERS - 1)).start()

        @pl.when(g >= MOE_BUFFERS)
        def _():
            out_copy(g - MOE_BUFFERS).wait()

        nj = xbuf.shape[2]
        xb = jnp.concatenate([xbuf[slot, :, j, :] for j in range(nj)], axis=1).astype(BF16)
        hid = _silu(_dot(xb, wgb_ref[...])) * _dot(xb, wub_ref[...])
        y = _dot(hid.astype(BF16), wdb_ref[...])
        for j in range(nj):
            ybuf[slot, :, j, :] = y[:, j * LANES:(j + 1) * LANES]
        out_copy(g).start()
        return carry

    lax.fori_loop(b0, b0 + n, body, 0)

    @pl.when(ex == last)
    def _():
        for s in range(MOE_BUFFERS):
            @pl.when(s < used)
            def _():
                out_copy(used - 1 - s).wait()

        total = ys_hbm.shape[0] // MOE_BLOCK
        ybuf[0] = jnp.zeros(ybuf.shape[1:], ybuf.dtype)

        def tail_copy(blk):
            return pltpu.make_async_copy(ybuf.at[0], ys_hbm.at[pl.ds(blk * MOE_BLOCK, MOE_BLOCK)], sem_out.at[0])

        def start_tail(blk, carry):
            tail_copy(blk).start()
            return carry

        def wait_tail(blk, carry):
            tail_copy(blk).wait()
            return carry

        lax.fori_loop(used, total, start_tail, 0)
        lax.fori_loop(used, total, wait_tail, 0)


def _moe(bstart, nblk_e, xs, w_gate, w_up, w_down):
    ns = xs.shape[0]
    row = xs.shape[1:]
    ne, d, eh = w_gate.shape
    wspec = lambda shape: pl.BlockSpec(shape, lambda ex, bs, nb: (ex, 0, 0))
    return pl.pallas_call(
        _moe_kernel,
        grid_spec=pltpu.PrefetchScalarGridSpec(
            num_scalar_prefetch=2,
            grid=(ne,),
            in_specs=[pl.BlockSpec(memory_space=pl.ANY), wspec((1, d, eh)), wspec((1, d, eh)), wspec((1, eh, d))],
            out_specs=pl.BlockSpec(memory_space=pl.ANY),
            scratch_shapes=[
                pltpu.VMEM((MOE_BUFFERS, MOE_BLOCK) + row, F32),
                pltpu.VMEM((MOE_BUFFERS, MOE_BLOCK) + row, F32),
                pltpu.VMEM((d, eh), BF16), pltpu.VMEM((d, eh), BF16), pltpu.VMEM((eh, d), BF16),
                pltpu.SemaphoreType.DMA((MOE_BUFFERS,)),
                pltpu.SemaphoreType.DMA((MOE_BUFFERS,)),
            ],
        ),
        out_shape=jax.ShapeDtypeStruct((ns,) + row, F32),
        compiler_params=_params("arbitrary"),
        name="moe",
    )(bstart, nblk_e, xs, w_gate, w_up, w_down)


def _combine_kernel(dest_hbm, ys_hbm, wtt_ref, x1_ref, sh_ref, mod_ref, postn_ref, o_ref, dest_smem0, dest_smem1, buf_ref, sem_idx, sem_rows):
    tc = x1_ref.shape[0]
    i = pl.program_id(0)
    n = pl.num_programs(0)
    dest_smem = (dest_smem0, dest_smem1)

    def idx_copy(tile, slot):
        return pltpu.make_async_copy(dest_hbm.at[tile], dest_smem[slot], sem_idx.at[slot])

    def issue_rows(slot):
        def issue(tok, carry):
            for kk in range(TOP_K):
                row = dest_smem[slot][kk * tc + tok]
                _row_copy(ys_hbm.at[pl.ds(row, 1)], buf_ref.at[slot, kk, pl.ds(tok, 1)],
                          sem_rows.at[slot]).start(priority=kk % 2)
            return carry
        lax.fori_loop(0, tc, issue, 0)

    @pl.when(i == 0)
    def _():
        idx_copy(0, 0).start()
        idx_copy(0, 0).wait()

        @pl.when(n > 1)
        def _():
            idx_copy(1, 1).start()
        issue_rows(0)

    def step(cur):
        nxt = 1 - cur

        @pl.when(i + 1 < n)
        def _():
            idx_copy(i + 1, nxt).wait()

            @pl.when(i + 2 < n)
            def _():
                idx_copy(i + 2, cur).start()
            issue_rows(nxt)

        for kk in range(TOP_K):
            _row_copy(ys_hbm.at[pl.ds(0, tc)], buf_ref.at[cur, kk], sem_rows.at[cur]).wait()

        wtt = wtt_ref[...]
        wk = [jnp.broadcast_to(wtt[:, kk:kk + 1], (tc, LANES)) for kk in range(TOP_K)]
        cols = []
        for j in range(buf_ref.shape[3]):
            acc = buf_ref[cur, 0, :, j, :] * wk[0]
            for kk in range(1, TOP_K):
                acc = acc + buf_ref[cur, kk, :, j, :] * wk[kk]
            cols.append(acc)
        y = jnp.concatenate(cols, axis=1) + sh_ref[...]
        mod = mod_ref[0]
        o_ref[...] = x1_ref[...] + mod[5:6, :] * (_rms(y) * postn_ref[...])

    parity = lax.rem(i, 2)
    for cur in range(2):
        @pl.when(parity == cur)
        def _():
            step(cur)


def _combine(dest_tiles, ys, wtt, x1, shared, mod3, ffn_post_norm, seq, tc):
    t, d = x1.shape
    tiles_per_batch = seq // tc
    tok = lambda i: (i, 0)
    return pl.pallas_call(
        _combine_kernel,
        grid=(t // tc,),
        in_specs=[
            pl.BlockSpec(memory_space=pl.ANY),
            pl.BlockSpec(memory_space=pl.ANY),
            pl.BlockSpec((tc, LANES), tok),
            pl.BlockSpec((tc, d), tok),
            pl.BlockSpec((tc, d), tok),
            pl.BlockSpec((1, 6, d), lambda i: (i // tiles_per_batch, 0, 0)),
            _const_spec((1, d)),
        ],
        out_specs=pl.BlockSpec((tc, d), tok),
        out_shape=jax.ShapeDtypeStruct((t, d), F32),
        scratch_shapes=[
            pltpu.SMEM((TOP_K * tc,), I32),
            pltpu.SMEM((TOP_K * tc,), I32),
            pltpu.VMEM((2, TOP_K, tc) + tuple(ys.shape[1:]), F32),
            pltpu.SemaphoreType.DMA((2,)),
            pltpu.SemaphoreType.DMA((2,)),
        ],
        compiler_params=_params("arbitrary"),
        name="combine",
    )(dest_tiles, ys, wtt, x1, shared, mod3, ffn_post_norm)


def _tile(n, want):
    t = min(n, want)
    assert n % t == 0
    return t


def _dest_tiles(dest, tile):
    k, t = dest.shape
    return dest.reshape(k, t // tile, tile).transpose(1, 0, 2).reshape(t // tile, k * tile)


def kernel(x, c, ada_w, ada_b, mix_pre_norm, mix_post_norm, w_in, conv_w, a_log, dt_bias, dn_norm_w, sg_ln_w, sg_ln_b, sg_w, sg_b, w_branch_gate, b_branch_gate, w_proj_dn, w_proj_sg, w_out, ffn_pre_norm, ffn_post_norm, w_router, router_bias, w_exp_gate, w_exp_up, w_exp_down, w_sh_gate, w_sh_up, w_sh_down):
    nb, seq, d = x.shape
    depth = ada_w.shape[0]
    t = nb * seq
    tm = _tile(seq, 512)
    tr = _tile(t, 512)
    tsc = _tile(t, 512)
    tcm = _tile(seq, 256)
    nblk = -(-t * TOP_K // MOE_BLOCK) + N_EXPERTS
    row = lambda v: v.reshape(1, -1)
    pad_lanes = lambda v: jnp.pad(v.astype(F32), (0, LANES - v.shape[0])).reshape(1, LANES)

    x2 = x.reshape(t, d)
    for l in range(depth):
        mod3 = _ada(c, ada_w[l], ada_b[l]).reshape(nb, 6, d)

        wi = w_in[l]
        qkvz, ab_cols, uv = wi[:, :4 * DN_WIDTH], wi[:, 4 * DN_WIDTH:4 * DN_WIDTH + 2 * DN_HEADS], wi[:, 4 * DN_WIDTH + 2 * DN_HEADS:]
        w1 = jnp.concatenate([qkvz, ab_cols, jnp.zeros((d, LANES - 2 * DN_HEADS), wi.dtype)], axis=1).astype(BF16)
        qkv, z, ab, ysg = _in_proj(x2, mod3, row(mix_pre_norm[l]), w1, uv.astype(BF16), row(sg_ln_w[l]), row(sg_ln_b[l]),
                                   sg_w[l], sg_b[l].T, seq, tm)

        ydn = _delta_net(qkv.reshape(nb, seq, -1), z.reshape(nb, seq, -1), ab.reshape(nb, seq, -1), conv_w[l],
                         pad_lanes(a_log[l]), pad_lanes(dt_bias[l]), row(dn_norm_w[l]))

        wr_t = w_router[l].T
        wrh = wr_t.astype(BF16)
        wrl = (wr_t - wrh.astype(F32)).astype(BF16)
        x1, hf, sct, shared = _mix(
            x2, ydn.reshape(t, -1), ysg, mod3, row(mix_pre_norm[l]), row(mix_post_norm[l]), row(ffn_pre_norm[l]),
            w_branch_gate[l].astype(BF16), row(b_branch_gate[l]), w_proj_dn[l].astype(BF16), w_proj_sg[l].astype(BF16),
            w_out[l].astype(BF16), wrh, wrl, w_sh_gate[l].astype(BF16), w_sh_up[l].astype(BF16), w_sh_down[l].astype(BF16),
            seq, tm)

        idx, wtt, rank, cnt = _route(sct, router_bias[l].reshape(-1, 1), tr)
        dest, meta = _dest(cnt, idx, rank, tr)
        xs = _scatter(meta[:, 2], meta[:, 3], _dest_tiles(dest, tsc), hf, nblk * MOE_BLOCK, tsc)
        ys = _moe(meta[:, 0], meta[:, 1], xs, w_exp_gate[l], w_exp_up[l], w_exp_down[l])
        x2 = _combine(_dest_tiles(dest, tcm), ys, wtt, x1, shared, mod3, row(ffn_post_norm[l]), seq, tcm)
    return x2.reshape(nb, seq, d)
```

```python
import functools

import jax
import jax.numpy as jnp
from jax import lax
from jax.experimental import pallas as pl
from jax.experimental.pallas import tpu as pltpu

F32 = jnp.float32
BF16 = jnp.bfloat16
I32 = jnp.int32

D_MODEL = 1024
DN_HEADS = 4
DN_HEAD_DIM = 128
DN_WIDTH = DN_HEADS * DN_HEAD_DIM
DN_CHUNK = 64
CONV_WIDTH = 4
SG_GROUPS = 4
SG_GROUP_DIM = 128
SG_WIDTH = SG_GROUPS * SG_GROUP_DIM
SG_CHUNK = 128
N_EXPERTS = 256
N_EXPERT_GROUPS = 8
GROUP_SIZE = N_EXPERTS // N_EXPERT_GROUPS
TOPK_GROUPS = 4
TOP_K = 8
EXPERT_HIDDEN = 256
ROUTED_SCALE = 2.5
MOE_BLOCK = 256
EPS = 1e-6

LANES = 128
SUBLANES = 8
ROW_TILE = SUBLANES
VMEM_LIMIT = 56 * 1024 * 1024
NEG_INF = float("-inf")


def _dot(a, b):
    return jnp.dot(a, b, preferred_element_type=F32)


def _dot_nt(a, b):
    return lax.dot_general(a, b, (((1,), (1,)), ((), ())), preferred_element_type=F32)


def _dot_tn(a, b):
    return lax.dot_general(a, b, (((0,), (0,)), ((), ())), preferred_element_type=F32)


def _split2(x):
    hi = x.astype(BF16)
    lo = (x - hi.astype(F32)).astype(BF16)
    return hi, lo


def _split3(x):
    hi = x.astype(BF16)
    r = x - hi.astype(F32)
    mid = r.astype(BF16)
    lo = (r - mid.astype(F32)).astype(BF16)
    return hi, mid, lo


def _dot_hp(a, b, dot=_dot):
    ah, al = _split2(a)
    bh, bl = _split2(b)
    return dot(ah, bh) + (dot(ah, bl) + dot(al, bh))


def _sigmoid(x):
    return 1.0 / (1.0 + jnp.exp(-x))


def _silu(x):
    return x * _sigmoid(x)


def _gelu(x):
    return 0.5 * x * (1.0 + lax.erf(x * (2.0 ** -0.5)))


def _softplus(x):
    return jnp.maximum(x, 0.0) + jnp.log1p(jnp.exp(-jnp.abs(x)))


def _rms(x):
    return x * lax.rsqrt(jnp.mean(x * x, axis=-1, keepdims=True) + EPS)


def _params(*sem):
    return pltpu.CompilerParams(dimension_semantics=sem, vmem_limit_bytes=VMEM_LIMIT)


def _const_spec(shape):
    nd = len(shape)
    return pl.BlockSpec(shape, lambda *_: (0,) * nd)


def _ada_kernel(c_ref, w_ref, b_ref, o_ref):
    cs = _silu(c_ref[...])
    o_ref[...] = _dot_hp(cs, w_ref[...]) + b_ref[...]


def _ada(c, ada_w, ada_b):
    b, d = c.shape
    n = ada_w.shape[1]
    tn = d
    return pl.pallas_call(
        _ada_kernel,
        grid=(n // tn,),
        in_specs=[_const_spec((b, d)), pl.BlockSpec((d, tn), lambda j: (0, j)), pl.BlockSpec((1, tn), lambda j: (0, j))],
        out_specs=pl.BlockSpec((b, tn), lambda j: (0, j)),
        out_shape=jax.ShapeDtypeStruct((b, n), F32),
        compiler_params=_params("arbitrary"),
        name="ada",
    )(c, ada_w, ada_b.reshape(1, n))


W1_COLS = 4 * DN_WIDTH + LANES


def _in_kernel(x_ref, mod_ref, pn_ref, w1_ref, wuv_ref, lnw_ref, lnb_ref, sgw_ref, sgbt_ref,
               qkv_ref, z_ref, ab_ref, ysg_ref):
    tm = x_ref.shape[0]
    mod = mod_ref[0]
    hm = _rms(x_ref[...]) * pn_ref[...] * (1.0 + mod[1:2, :]) + mod[0:1, :]
    hb = hm.astype(BF16)
    p1 = _dot(hb, w1_ref[...])
    qkv_ref[...] = p1[:, :3 * DN_WIDTH]
    z_ref[...] = p1[:, 3 * DN_WIDTH:4 * DN_WIDTH]
    ab_ref[...] = p1[:, 4 * DN_WIDTH:]
    uv = _dot(hb, wuv_ref[...])
    u = _gelu(uv[:, :SG_WIDTH])
    vg = _gelu(uv[:, SG_WIDTH:])
    mu = jnp.mean(vg, axis=-1, keepdims=True)
    dv = vg - mu
    var = jnp.mean(dv * dv, axis=-1, keepdims=True)
    vgn = (dv * lax.rsqrt(var + EPS) * lnw_ref[...] + lnb_ref[...]).astype(BF16)
    row = lax.broadcasted_iota(I32, (SG_CHUNK, SG_CHUNK), 0)
    col = lax.broadcasted_iota(I32, (SG_CHUNK, SG_CHUNK), 1)
    tril = row >= col
    for g in range(SG_GROUPS):
        wg = jnp.where(tril, sgw_ref[g], 0.0).astype(BF16)
        bg = sgbt_ref[:, g:g + 1]
        cs = slice(g * SG_GROUP_DIM, (g + 1) * SG_GROUP_DIM)
        for n in range(tm // SG_CHUNK):
            rs = slice(n * SG_CHUNK, (n + 1) * SG_CHUNK)
            mixed = _dot(wg, vgn[rs, cs]) + bg
            ysg_ref[rs, cs] = (u[rs, cs] * mixed).astype(BF16)


def _in_proj(x2, mod3, pre_norm, w1, wuv, sg_ln_w, sg_ln_b, sg_w, sg_bt, seq, tm):
    t, d = x2.shape
    tiles_per_batch = seq // tm
    tok = lambda i: (i, 0)
    return pl.pallas_call(
        _in_kernel,
        grid=(t // tm,),
        in_specs=[
            pl.BlockSpec((tm, d), tok),
            pl.BlockSpec((1, 6, d), lambda i: (i // tiles_per_batch, 0, 0)),
            _const_spec((1, d)),
            _const_spec(w1.shape),
            _const_spec(wuv.shape),
            _const_spec((1, SG_WIDTH)),
            _const_spec((1, SG_WIDTH)),
            _const_spec(sg_w.shape),
            _const_spec(sg_bt.shape),
        ],
        out_specs=[
            pl.BlockSpec((tm, 3 * DN_WIDTH), tok),
            pl.BlockSpec((tm, DN_WIDTH), tok),
            pl.BlockSpec((tm, LANES), tok),
            pl.BlockSpec((tm, SG_WIDTH), tok),
        ],
        out_shape=[
            jax.ShapeDtypeStruct((t, 3 * DN_WIDTH), F32),
            jax.ShapeDtypeStruct((t, DN_WIDTH), F32),
            jax.ShapeDtypeStruct((t, LANES), F32),
            jax.ShapeDtypeStruct((t, SG_WIDTH), BF16),
        ],
        compiler_params=_params("arbitrary"),
        name="in_proj",
    )(x2, mod3, pre_norm, w1, wuv, sg_ln_w, sg_ln_b, sg_w, sg_bt)


def _hp_parts(ap, bp, dot=_dot):
    return dot(ap[0], bp[0]) + (dot(ap[0], bp[1]) + dot(ap[1], bp[0]))


def _unit_lower_inverses(a_list):
    c = a_list[0].shape[0]
    i = lax.broadcasted_iota(I32, (c, c), 0)
    j = lax.broadcasted_iota(I32, (c, c), 1)
    eye = (i == j).astype(F32)
    first = (i == j + 1) & ((i & 1) == 1)
    d_list = [eye - jnp.where(first, a, 0.0) for a in a_list]
    b = 2
    while b < c:
        shift = b.bit_length()
        off = ((i >> shift) == (j >> shift)) & ((i & b) != 0) & ((j & b) == 0)
        a_parts = [_split2(jnp.where(off, a, 0.0)) for a in a_list]
        d_parts = [_split2(d) for d in d_list]
        t_list = [_hp_parts(dp, ap) for dp, ap in zip(d_parts, a_parts)]
        t_parts = [_split2(t) for t in t_list]
        d_list = [d - _hp_parts(tp, dp) for d, tp, dp in zip(d_list, t_parts, d_parts)]
        b *= 2
    return d_list


def _dn_kernel(qkv_ref, z_ref, ab_ref, convw_ref, alog_ref, dtb_ref, nw_ref, y_ref, carry_ref, state_ref):
    nb, c = qkv_ref.shape[0], qkv_ref.shape[1]
    hd = DN_HEAD_DIM
    probs = [(b, h) for b in range(nb) for h in range(DN_HEADS)]

    @pl.when(pl.program_id(0) == 0)
    def _():
        carry_ref[...] = jnp.zeros_like(carry_ref)
        state_ref[...] = jnp.zeros_like(state_ref)

    ab = ab_ref[...].reshape(nb * c, LANES)
    g = -jnp.exp(alog_ref[...]) * _softplus(ab + dtb_ref[...])
    beta_all = _sigmoid(ab)
    ri = lax.broadcasted_iota(I32, (nb * c, nb * c), 0)
    ci = lax.broadcasted_iota(I32, (nb * c, nb * c), 1)
    tri = ((ri >= ci) & ((ri // c) == (ci // c))).astype(BF16)
    gh, gm, gl = _split3(g)
    gc = _dot(tri, gh) + (_dot(tri, gm) + _dot(tri, gl))
    gct = gc.T

    i = lax.broadcasted_iota(I32, (c, c), 0)
    j = lax.broadcasted_iota(I32, (c, c), 1)
    causal = i >= j
    strict = i > j
    row8 = lax.broadcasted_iota(I32, (SUBLANES, 3 * DN_WIDTH), 0)

    acts, gcs, e_gcs, e_rems, e_lasts = [], [], [], [], []
    for b in range(nb):
        xc = qkv_ref[b]
        prev = carry_ref[b]
        acc = xc * convw_ref[CONV_WIDTH - 1:CONV_WIDTH, :]
        for s in range(1, CONV_WIDTH):
            rolled = pltpu.roll(xc, s, axis=0)
            top = jnp.where(row8 < s, pltpu.roll(prev, s, axis=0), rolled[:SUBLANES])
            shifted = jnp.concatenate([top, rolled[SUBLANES:]], axis=0)
            acc = acc + shifted * convw_ref[CONV_WIDTH - 1 - s:CONV_WIDTH - s, :]
        carry_ref[b] = xc[c - SUBLANES:]
        acts.append(_silu(acc))
        gcb = gc[b * c:(b + 1) * c]
        g_last = gcb[c - 1:c, :]
        gcs.append(gcb)
        e_gcs.append(jnp.exp(gcb))
        e_rems.append(jnp.exp(g_last - gcb))
        e_lasts.append(jnp.exp(g_last))

    qn, kn, knb, kb, vb, decay = {}, {}, {}, {}, {}, {}
    for p in probs:
        b, h = p
        act = acts[b]
        q = act[:, h * hd:(h + 1) * hd]
        k = act[:, DN_WIDTH + h * hd:DN_WIDTH + (h + 1) * hd]
        v = act[:, 2 * DN_WIDTH + h * hd:2 * DN_WIDTH + (h + 1) * hd]
        qn[p] = q * lax.rsqrt(jnp.sum(q * q, axis=-1, keepdims=True) + EPS) * (hd ** -0.5)
        kn[p] = k * lax.rsqrt(jnp.sum(k * k, axis=-1, keepdims=True) + EPS)
        beta = beta_all[b * c:(b + 1) * c, DN_HEADS + h:DN_HEADS + h + 1]
        diff = gcs[b][:, h:h + 1] - gct[h:h + 1, b * c:(b + 1) * c]
        decay[p] = jnp.where(causal, jnp.exp(jnp.where(causal, diff, 0.0)), 0.0)
        kb[p] = kn[p] * beta
        vb[p] = v * beta
        knb[p] = kn[p].astype(BF16)

    kk = {p: _dot_nt(kb[p].astype(BF16), knb[p]) for p in probs}
    qk = {p: _dot_nt(qn[p].astype(BF16), knb[p]) for p in probs}
    tinv = dict(zip(probs, _unit_lower_inverses([jnp.where(strict, kk[p] * decay[p], 0.0) for p in probs])))
    t_parts = {p: _split2(tinv[p]) for p in probs}
    rhs = {p: jnp.concatenate([vb[p], kb[p] * e_gcs[p[0]][:, p[1]:p[1] + 1]], axis=1) for p in probs}
    r_parts = {p: _split2(rhs[p]) for p in probs}
    sol = {p: _hp_parts(t_parts[p], r_parts[p]) for p in probs}

    lhs = {p: jnp.concatenate([sol[p][:, hd:], qn[p] * e_gcs[p[0]][:, p[1]:p[1] + 1]], axis=0).astype(BF16) for p in probs}
    state = {p: state_ref[p[0] * DN_HEADS + p[1]] for p in probs}
    ws = {p: _dot(lhs[p], state[p].astype(BF16)) for p in probs}
    vnb = {p: (sol[p][:, :hd] - ws[p][:c]).astype(BF16) for p in probs}
    qkm = {p: jnp.where(causal, qk[p] * decay[p], 0.0).astype(BF16) for p in probs}
    o = {p: ws[p][c:] + _dot(qkm[p], vnb[p]) for p in probs}
    kd = {p: (kn[p] * e_rems[p[0]][:, p[1]:p[1] + 1]).astype(BF16) for p in probs}
    upd = {p: _dot_tn(kd[p], vnb[p]) for p in probs}
    for p in probs:
        b, h = p
        state_ref[b * DN_HEADS + h] = state[p] * e_lasts[b][:, h:h + 1] + upd[p]
        zh = z_ref[b, :, h * hd:(h + 1) * hd]
        y_ref[b, :, h * hd:(h + 1) * hd] = (_rms(o[p]) * nw_ref[...] * _silu(zh)).astype(BF16)


def _delta_net(qkv3, z3, ab3, conv_w, alog_row, dtb_row, dn_norm_w):
    nb, s, _ = qkv3.shape
    c = DN_CHUNK
    blk = lambda n: (0, n, 0)
    return pl.pallas_call(
        _dn_kernel,
        grid=(s // c,),
        in_specs=[
            pl.BlockSpec((nb, c, 3 * DN_WIDTH), blk),
            pl.BlockSpec((nb, c, DN_WIDTH), blk),
            pl.BlockSpec((nb, c, LANES), blk),
            _const_spec(conv_w.shape),
            _const_spec((1, LANES)),
            _const_spec((1, LANES)),
            _const_spec((1, DN_HEAD_DIM)),
        ],
        out_specs=pl.BlockSpec((nb, c, DN_WIDTH), blk),
        out_shape=jax.ShapeDtypeStruct((nb, s, DN_WIDTH), BF16),
        scratch_shapes=[
            pltpu.VMEM((nb, SUBLANES, 3 * DN_WIDTH), F32),
            pltpu.VMEM((nb * DN_HEADS, DN_HEAD_DIM, DN_HEAD_DIM), F32),
        ],
        compiler_params=_params("arbitrary"),
        name="delta_net",
    )(qkv3, z3, ab3, conv_w, alog_row, dtb_row, dn_norm_w)


def _mix_kernel(x_ref, ydn_ref, ysg_ref, mod_ref, pn_ref, postn_ref, fpn_ref,
                wbg_ref, bbg_ref, wpd_ref, wps_ref, wout_ref, wrh_ref, wrl_ref, wsg_ref, wsu_ref, wsd_ref,
                x1_ref, hf_ref, sct_ref, sh_ref):
    d = x_ref.shape[1]
    x = x_ref[...]
    mod = mod_ref[0]
    hm = _rms(x) * pn_ref[...] * (1.0 + mod[1:2, :]) + mod[0:1, :]
    gates = _sigmoid(_dot(hm.astype(BF16), wbg_ref[...]) + bbg_ref[...])
    merged = gates[:, :d] * _dot(ydn_ref[...], wpd_ref[...]) + gates[:, d:] * _dot(ysg_ref[...], wps_ref[...])
    y = _dot(merged.astype(BF16), wout_ref[...])
    x1 = x + mod[2:3, :] * (_rms(y) * postn_ref[...])
    x1_ref[...] = x1
    hf = _rms(x1) * fpn_ref[...] * (1.0 + mod[4:5, :]) + mod[3:4, :]
    for j in range(ROW_TILE):
        hf_ref[pl.ds(j, hf.shape[0], stride=ROW_TILE), :] = hf[:, j * LANES:(j + 1) * LANES]
    hh, hl = _split2(hf)
    logits_t = _dot_nt(wrh_ref[...], hh) + (_dot_nt(wrh_ref[...], hl) + _dot_nt(wrl_ref[...], hh))
    sct_ref[...] = _sigmoid(logits_t)
    hid = _silu(_dot(hh, wsg_ref[...])) * _dot(hh, wsu_ref[...])
    sh_ref[...] = _dot(hid.astype(BF16), wsd_ref[...])


def _mix(x2, ydn, ysg, mod3, pre_norm, post_norm, ffn_pre_norm, wbg, bbg, wpd, wps, wout, wrh, wrl, wsg, wsu, wsd, seq, tm):
    t, d = x2.shape
    tiles_per_batch = seq // tm
    tok = lambda i: (i, 0)
    consts = [pre_norm, post_norm, ffn_pre_norm, wbg, bbg, wpd, wps, wout, wrh, wrl, wsg, wsu, wsd]
    return pl.pallas_call(
        _mix_kernel,
        grid=(t // tm,),
        in_specs=[
            pl.BlockSpec((tm, d), tok),
            pl.BlockSpec((tm, DN_WIDTH), tok),
            pl.BlockSpec((tm, SG_WIDTH), tok),
            pl.BlockSpec((1, 6, d), lambda i: (i // tiles_per_batch, 0, 0)),
        ] + [_const_spec(a.shape) for a in consts],
        out_specs=[
            pl.BlockSpec((tm, d), tok),
            pl.BlockSpec((tm * ROW_TILE, LANES), tok),
            pl.BlockSpec((N_EXPERTS, tm), lambda i: (0, i)),
            pl.BlockSpec((tm, d), tok),
        ],
        out_shape=[
            jax.ShapeDtypeStruct((t, d), F32),
            jax.ShapeDtypeStruct((t * ROW_TILE, LANES), F32),
            jax.ShapeDtypeStruct((N_EXPERTS, t), F32),
            jax.ShapeDtypeStruct((t, d), F32),
        ],
        compiler_params=_params("arbitrary"),
        name="mix",
    )(x2, ydn, ysg, mod3, *consts)


def _first_argmax(vals, idx):
    m = jnp.max(vals, axis=0, keepdims=True)
    first = jnp.min(jnp.where(vals == m, idx, jnp.int32(2 ** 30)), axis=0, keepdims=True)
    return m, first


def _route_kernel(sct_ref, bias_ref, idx_ref, wtt_ref, rank_ref, cnt_ref, carry_ref):
    tm = sct_ref.shape[1]

    @pl.when(pl.program_id(0) == 0)
    def _():
        carry_ref[...] = jnp.zeros_like(carry_ref)

    scores = sct_ref[...]
    sel = scores + bias_ref[...]
    erow = lax.broadcasted_iota(I32, (N_EXPERTS, tm), 0)
    grow = lax.broadcasted_iota(I32, (GROUP_SIZE, tm), 0)

    gs = []
    for gidx in range(N_EXPERT_GROUPS):
        sg = sel[gidx * GROUP_SIZE:(gidx + 1) * GROUP_SIZE]
        m1, first = _first_argmax(sg, grow)
        m2 = jnp.max(jnp.where(grow == first, NEG_INF, sg), axis=0, keepdims=True)
        gs.append(m1 + m2)
    gsc = jnp.concatenate(gs, axis=0)
    giota = lax.broadcasted_iota(I32, (N_EXPERT_GROUPS, tm), 0)
    gmask = jnp.zeros((N_EXPERT_GROUPS, tm), F32)
    cur = gsc
    for _ in range(TOPK_GROUPS):
        _, gi = _first_argmax(cur, giota)
        pick = giota == gi
        gmask = jnp.where(pick, 1.0, gmask)
        cur = jnp.where(pick, NEG_INF, cur)
    masked = jnp.concatenate(
        [jnp.where(gmask[gidx:gidx + 1, :] > 0.5, sel[gidx * GROUP_SIZE:(gidx + 1) * GROUP_SIZE], NEG_INF)
         for gidx in range(N_EXPERT_GROUPS)], axis=0)

    cur = masked
    idxs, wts = [], []
    onehot = jnp.zeros((N_EXPERTS, tm), F32)
    for _ in range(TOP_K):
        _, ei = _first_argmax(cur, erow)
        pick = erow == ei
        idxs.append(ei)
        wts.append(jnp.sum(jnp.where(pick, scores, 0.0), axis=0, keepdims=True))
        onehot = jnp.where(pick, 1.0, onehot)
        cur = jnp.where(pick, NEG_INF, cur)
    idx = jnp.concatenate(idxs, axis=0)
    wt = jnp.concatenate(wts, axis=0)
    wt = wt / jnp.sum(wt, axis=0, keepdims=True) * ROUTED_SCALE
    idx_ref[...] = idx
    wpad = jnp.concatenate([wt, jnp.zeros((LANES - TOP_K, tm), F32)], axis=0)
    wtt_ref[...] = wpad.T

    ti = lax.broadcasted_iota(I32, (tm, tm), 0)
    tj = lax.broadcasted_iota(I32, (tm, tm), 1)
    upper = (ti < tj).astype(BF16)
    before = _dot(onehot.astype(BF16), upper) + carry_ref[...]
    rank_ref[...] = jnp.concatenate(
        [jnp.sum(jnp.where(erow == idxs[kk], before, 0.0), axis=0, keepdims=True) for kk in range(TOP_K)],
        axis=0).astype(I32)
    total = carry_ref[...] + jnp.sum(onehot, axis=1, keepdims=True)
    carry_ref[...] = total
    cnt_ref[...] = total


def _route(sct, bias_col, tm):
    e, t = sct.shape
    tile = lambda i: (0, i)
    return pl.pallas_call(
        _route_kernel,
        grid=(t // tm,),
        in_specs=[pl.BlockSpec((e, tm), tile), _const_spec((e, 1))],
        out_specs=[
            pl.BlockSpec((TOP_K, tm), tile),
            pl.BlockSpec((tm, LANES), lambda i: (i, 0)),
            pl.BlockSpec((TOP_K, tm), tile),
            _const_spec((e, 1)),
        ],
        out_shape=[
            jax.ShapeDtypeStruct((TOP_K, t), I32),
            jax.ShapeDtypeStruct((t, LANES), F32),
            jax.ShapeDtypeStruct((TOP_K, t), I32),
            jax.ShapeDtypeStruct((e, 1), F32),
        ],
        scratch_shapes=[pltpu.VMEM((e, 1), F32)],
        compiler_params=_params("arbitrary"),
        name="route",
    )(sct, bias_col)


def _dest_kernel(cnt_ref, idx_ref, rank_ref, dest_ref, meta_ref):
    tm = idx_ref.shape[1]
    e = N_EXPERTS
    cnt = cnt_ref[...]
    padded = jnp.floor((cnt + (MOE_BLOCK - 1)) * (1.0 / MOE_BLOCK)) * MOE_BLOCK
    pw = jnp.broadcast_to(padded, (e, LANES))
    ri = lax.broadcasted_iota(I32, (e, e), 0)
    ci = lax.broadcasted_iota(I32, (e, e), 1)
    lower = (ri >= ci).astype(BF16)
    ph, pm, plo = _split3(pw)
    pends = _dot(lower, ph) + (_dot(lower, pm) + _dot(lower, plo))
    pstart = pends - pw
    erow = lax.broadcasted_iota(I32, (e, tm), 0)
    idx = idx_ref[...]
    rows = [jnp.sum(jnp.where(erow == idx[kk:kk + 1, :], pstart[:, 0:1], 0.0), axis=0, keepdims=True)
            for kk in range(TOP_K)]
    dest_ref[...] = jnp.concatenate(rows, axis=0).astype(I32) + rank_ref[...]
    lane = lax.broadcasted_iota(I32, (e, LANES), 1)
    blocks = jnp.where(lane == 0, pstart, pw) * (1.0 / MOE_BLOCK)
    pads = jnp.where(lane == 2, pstart + cnt, pw - cnt)
    meta_ref[...] = jnp.where(lane < 2, blocks, pads).astype(I32)


def _dest(cnt, idx, rank, tm):
    e = cnt.shape[0]
    t = idx.shape[1]
    tile = lambda i: (0, i)
    return pl.pallas_call(
        _dest_kernel,
        grid=(t // tm,),
        in_specs=[_const_spec((e, 1)), pl.BlockSpec((TOP_K, tm), tile), pl.BlockSpec((TOP_K, tm), tile)],
        out_specs=[pl.BlockSpec((TOP_K, tm), tile), _const_spec((e, LANES))],
        out_shape=[jax.ShapeDtypeStruct((TOP_K, t), I32), jax.ShapeDtypeStruct((e, LANES), I32)],
        compiler_params=_params("arbitrary"),
        name="dest",
    )(cnt, idx, rank)


def _row_copy(src, dst, sem):
    return pltpu.make_async_copy(src, dst, sem)


def _rows(first, count=1):
    return pl.ds(pl.multiple_of(first * ROW_TILE, ROW_TILE), count * ROW_TILE)


PAD_PIECES = tuple(MOE_BLOCK >> s for s in range(1, MOE_BLOCK.bit_length()))


def _zero_unassigned_rows(pad_start_ref, pad_rows_ref, xs_out, zero_ref, sem):
    zero_ref[...] = jnp.zeros_like(zero_ref)
    ne = pad_start_ref.shape[0]
    total = xs_out.shape[0] // (MOE_BLOCK * ROW_TILE)

    def piece(rows, pos):
        return pltpu.make_async_copy(zero_ref.at[_rows(0, rows)], xs_out.at[_rows(pos, rows)], sem)

    def pads(wait):
        def per_expert(ex, carry):
            pos = pad_start_ref[ex]
            pad = pad_rows_ref[ex]
            for rows in PAD_PIECES:
                @pl.when((pad & rows) != 0)
                def _():
                    piece(rows, pos).wait() if wait else piece(rows, pos).start()
                pos = pos + (pad & rows)
            return carry
        lax.fori_loop(0, ne, per_expert, 0)

    def tail(wait):
        used = (pad_start_ref[ne - 1] + pad_rows_ref[ne - 1]) // MOE_BLOCK

        def per_block(blk, carry):
            cp = piece(MOE_BLOCK, blk * MOE_BLOCK)
            cp.wait() if wait else cp.start()
            return carry
        lax.fori_loop(used, total, per_block, 0)

    pads(False)
    tail(False)
    pads(True)
    tail(True)


def _scatter_kernel(pad_start_ref, pad_rows_ref, dest_hbm, hf_ref, xs_out, dest_smem, zero_ref, sem_idx, sem_rows, sem_zero):
    tm = hf_ref.shape[0] // ROW_TILE
    i = pl.program_id(0)

    @pl.when(i == 0)
    def _():
        _zero_unassigned_rows(pad_start_ref, pad_rows_ref, xs_out, zero_ref, sem_zero)

    idx_copy = pltpu.make_async_copy(dest_hbm.at[i], dest_smem, sem_idx)
    idx_copy.start()
    idx_copy.wait()

    def issue(tok, carry):
        for kk in range(TOP_K):
            slot = dest_smem[kk * tm + tok]
            _row_copy(hf_ref.at[_rows(tok)], xs_out.at[_rows(slot)], sem_rows).start(priority=kk % 2)
        return carry

    lax.fori_loop(0, tm, issue, 0)
    for kk in range(TOP_K):
        _row_copy(hf_ref, xs_out.at[_rows(0, tm)], sem_rows).wait()


def _scatter(pad_start, pad_rows, dest_tiles, hf, n_slots, tm):
    t = hf.shape[0] // ROW_TILE
    n_tiles = t // tm
    return pl.pallas_call(
        _scatter_kernel,
        grid_spec=pltpu.PrefetchScalarGridSpec(
            num_scalar_prefetch=2,
            grid=(n_tiles,),
            in_specs=[pl.BlockSpec(memory_space=pl.ANY), pl.BlockSpec((tm * ROW_TILE, LANES), lambda i, ps, pr: (i, 0))],
            out_specs=pl.BlockSpec(memory_space=pl.ANY),
            scratch_shapes=[
                pltpu.SMEM((TOP_K * tm,), I32),
                pltpu.VMEM((MOE_BLOCK * ROW_TILE, LANES), hf.dtype),
                pltpu.SemaphoreType.DMA,
                pltpu.SemaphoreType.DMA,
                pltpu.SemaphoreType.DMA,
            ],
        ),
        out_shape=jax.ShapeDtypeStruct((n_slots * ROW_TILE, LANES), hf.dtype),
        compiler_params=_params("arbitrary"),
        name="scatter",
    )(pad_start, pad_rows, dest_tiles, hf)


MOE_BUFFERS = 3


def _moe_kernel(bstart_ref, nblk_ref, xs_hbm, wg_ref, wu_ref, wd_ref, ys_hbm,
                xbuf, ybuf, wgb_ref, wub_ref, wdb_ref, sem_in, sem_out):
    ex = pl.program_id(0)
    last = pl.num_programs(0) - 1
    n = nblk_ref[ex]
    b0 = bstart_ref[ex]
    used = bstart_ref[last] + nblk_ref[last]

    def in_copy(g):
        slot = lax.rem(g, MOE_BUFFERS)
        return pltpu.make_async_copy(xs_hbm.at[_rows(g * MOE_BLOCK, MOE_BLOCK)], xbuf.at[slot], sem_in.at[slot])

    def out_copy(g):
        slot = lax.rem(g, MOE_BUFFERS)
        return pltpu.make_async_copy(ybuf.at[slot], ys_hbm.at[_rows(g * MOE_BLOCK, MOE_BLOCK)], sem_out.at[slot])

    @pl.when(ex == 0)
    def _():
        for s in range(MOE_BUFFERS - 1):
            @pl.when(s < used)
            def _():
                in_copy(s).start()

    wgb_ref[...] = wg_ref[0].astype(BF16)
    wub_ref[...] = wu_ref[0].astype(BF16)
    wdb_ref[...] = wd_ref[0].astype(BF16)

    def body(g, carry):
        slot = lax.rem(g, MOE_BUFFERS)
        in_copy(g).wait()

        @pl.when(g + (MOE_BUFFERS - 1) < used)
        def _():
            in_copy(g + (MOE_BUFFERS - 1)).start()

        @pl.when(g >= MOE_BUFFERS)
        def _():
            out_copy(g - MOE_BUFFERS).wait()

        xb = jnp.concatenate([xbuf[slot, pl.ds(j, MOE_BLOCK, stride=ROW_TILE), :] for j in range(ROW_TILE)],
                             axis=1).astype(BF16)
        hid = _silu(_dot(xb, wgb_ref[...])) * _dot(xb, wub_ref[...])
        y = _dot(hid.astype(BF16), wdb_ref[...])
        for j in range(ROW_TILE):
            ybuf[slot, pl.ds(j, MOE_BLOCK, stride=ROW_TILE), :] = y[:, j * LANES:(j + 1) * LANES]
        out_copy(g).start()
        return carry

    lax.fori_loop(b0, b0 + n, body, 0)

    @pl.when(ex == last)
    def _():
        for s in range(MOE_BUFFERS):
            @pl.when(s < used)
            def _():
                out_copy(used - 1 - s).wait()

        total = ys_hbm.shape[0] // (MOE_BLOCK * ROW_TILE)
        ybuf[0] = jnp.zeros(ybuf.shape[1:], ybuf.dtype)

        def tail_copy(blk):
            return pltpu.make_async_copy(ybuf.at[0], ys_hbm.at[_rows(blk * MOE_BLOCK, MOE_BLOCK)], sem_out.at[0])

        def start_tail(blk, carry):
            tail_copy(blk).start()
            return carry

        def wait_tail(blk, carry):
            tail_copy(blk).wait()
            return carry

        lax.fori_loop(used, total, start_tail, 0)
        lax.fori_loop(used, total, wait_tail, 0)


def _moe(bstart, nblk_e, xs, w_gate, w_up, w_down):
    ne, d, eh = w_gate.shape
    wspec = lambda shape: pl.BlockSpec(shape, lambda ex, bs, nb: (ex, 0, 0))
    return pl.pallas_call(
        _moe_kernel,
        grid_spec=pltpu.PrefetchScalarGridSpec(
            num_scalar_prefetch=2,
            grid=(ne,),
            in_specs=[pl.BlockSpec(memory_space=pl.ANY), wspec((1, d, eh)), wspec((1, d, eh)), wspec((1, eh, d))],
            out_specs=pl.BlockSpec(memory_space=pl.ANY),
            scratch_shapes=[
                pltpu.VMEM((MOE_BUFFERS, MOE_BLOCK * ROW_TILE, LANES), F32),
                pltpu.VMEM((MOE_BUFFERS, MOE_BLOCK * ROW_TILE, LANES), F32),
                pltpu.VMEM((d, eh), BF16), pltpu.VMEM((d, eh), BF16), pltpu.VMEM((eh, d), BF16),
                pltpu.SemaphoreType.DMA((MOE_BUFFERS,)),
                pltpu.SemaphoreType.DMA((MOE_BUFFERS,)),
            ],
        ),
        out_shape=jax.ShapeDtypeStruct(xs.shape, F32),
        compiler_params=_params("arbitrary"),
        name="moe",
    )(bstart, nblk_e, xs, w_gate, w_up, w_down)


def _combine_kernel(dest_hbm, ys_hbm, wtt_ref, x1_ref, sh_ref, mod_ref, postn_ref, o_ref, dest_smem0, dest_smem1, buf_ref, sem_idx, sem_rows):
    tc = x1_ref.shape[0]
    i = pl.program_id(0)
    n = pl.num_programs(0)
    dest_smem = (dest_smem0, dest_smem1)

    def idx_copy(tile, slot):
        return pltpu.make_async_copy(dest_hbm.at[tile], dest_smem[slot], sem_idx.at[slot])

    def issue_rows(slot):
        def issue(tok, carry):
            for kk in range(TOP_K):
                row = dest_smem[slot][kk * tc + tok]
                _row_copy(ys_hbm.at[_rows(row)], buf_ref.at[slot, kk, _rows(tok)],
                          sem_rows.at[slot]).start(priority=kk % 2)
            return carry
        lax.fori_loop(0, tc, issue, 0)

    @pl.when(i == 0)
    def _():
        idx_copy(0, 0).start()
        idx_copy(0, 0).wait()

        @pl.when(n > 1)
        def _():
            idx_copy(1, 1).start()
        issue_rows(0)

    def step(cur):
        nxt = 1 - cur

        @pl.when(i + 1 < n)
        def _():
            idx_copy(i + 1, nxt).wait()

            @pl.when(i + 2 < n)
            def _():
                idx_copy(i + 2, cur).start()
            issue_rows(nxt)

        for kk in range(TOP_K):
            _row_copy(ys_hbm.at[_rows(0, tc)], buf_ref.at[cur, kk], sem_rows.at[cur]).wait()

        wtt = wtt_ref[...]
        wk = [jnp.broadcast_to(wtt[:, kk:kk + 1], (tc, LANES)) for kk in range(TOP_K)]
        cols = []
        for j in range(ROW_TILE):
            lane_group = pl.ds(j, tc, stride=ROW_TILE)
            acc = buf_ref[cur, 0, lane_group, :] * wk[0]
            for kk in range(1, TOP_K):
                acc = acc + buf_ref[cur, kk, lane_group, :] * wk[kk]
            cols.append(acc)
        y = jnp.concatenate(cols, axis=1) + sh_ref[...]
        mod = mod_ref[0]
        o_ref[...] = x1_ref[...] + mod[5:6, :] * (_rms(y) * postn_ref[...])

    parity = lax.rem(i, 2)
    for cur in range(2):
        @pl.when(parity == cur)
        def _():
            step(cur)


def _combine(dest_tiles, ys, wtt, x1, shared, mod3, ffn_post_norm, seq, tc):
    t, d = x1.shape
    tiles_per_batch = seq // tc
    tok = lambda i: (i, 0)
    return pl.pallas_call(
        _combine_kernel,
        grid=(t // tc,),
        in_specs=[
            pl.BlockSpec(memory_space=pl.ANY),
            pl.BlockSpec(memory_space=pl.ANY),
            pl.BlockSpec((tc, LANES), tok),
            pl.BlockSpec((tc, d), tok),
            pl.BlockSpec((tc, d), tok),
            pl.BlockSpec((1, 6, d), lambda i: (i // tiles_per_batch, 0, 0)),
            _const_spec((1, d)),
        ],
        out_specs=pl.BlockSpec((tc, d), tok),
        out_shape=jax.ShapeDtypeStruct((t, d), F32),
        scratch_shapes=[
            pltpu.SMEM((TOP_K * tc,), I32),
            pltpu.SMEM((TOP_K * tc,), I32),
            pltpu.VMEM((2, TOP_K, tc * ROW_TILE, LANES), F32),
            pltpu.SemaphoreType.DMA((2,)),
            pltpu.SemaphoreType.DMA((2,)),
        ],
        compiler_params=_params("arbitrary"),
        name="combine",
    )(dest_tiles, ys, wtt, x1, shared, mod3, ffn_post_norm)


def _tile(n, want):
    t = min(n, want)
    assert n % t == 0
    return t


def _dest_tiles(dest, tile):
    k, t = dest.shape
    return dest.reshape(k, t // tile, tile).transpose(1, 0, 2).reshape(t // tile, k * tile)


def kernel(x, c, ada_w, ada_b, mix_pre_norm, mix_post_norm, w_in, conv_w, a_log, dt_bias, dn_norm_w, sg_ln_w, sg_ln_b, sg_w, sg_b, w_branch_gate, b_branch_gate, w_proj_dn, w_proj_sg, w_out, ffn_pre_norm, ffn_post_norm, w_router, router_bias, w_exp_gate, w_exp_up, w_exp_down, w_sh_gate, w_sh_up, w_sh_down):
    nb, seq, d = x.shape
    depth = ada_w.shape[0]
    t = nb * seq
    tm = _tile(seq, 512)
    tr = _tile(t, 512)
    tsc = _tile(t, 512)
    tcm = _tile(seq, 256)
    nblk = -(-t * TOP_K // MOE_BLOCK) + N_EXPERTS
    row = lambda v: v.reshape(1, -1)
    pad_lanes = lambda v: jnp.pad(v.astype(F32), (0, LANES - v.shape[0])).reshape(1, LANES)

    x2 = x.reshape(t, d)
    for l in range(depth):
        mod3 = _ada(c, ada_w[l], ada_b[l]).reshape(nb, 6, d)

        wi = w_in[l]
        qkvz, ab_cols, uv = wi[:, :4 * DN_WIDTH], wi[:, 4 * DN_WIDTH:4 * DN_WIDTH + 2 * DN_HEADS], wi[:, 4 * DN_WIDTH + 2 * DN_HEADS:]
        w1 = jnp.concatenate([qkvz, ab_cols, jnp.zeros((d, LANES - 2 * DN_HEADS), wi.dtype)], axis=1).astype(BF16)
        qkv, z, ab, ysg = _in_proj(x2, mod3, row(mix_pre_norm[l]), w1, uv.astype(BF16), row(sg_ln_w[l]), row(sg_ln_b[l]),
                                   sg_w[l], sg_b[l].T, seq, tm)

        ydn = _delta_net(qkv.reshape(nb, seq, -1), z.reshape(nb, seq, -1), ab.reshape(nb, seq, -1), conv_w[l],
                         pad_lanes(a_log[l]), pad_lanes(dt_bias[l]), row(dn_norm_w[l]))

        wr_t = w_router[l].T
        wrh = wr_t.astype(BF16)
        wrl = (wr_t - wrh.astype(F32)).astype(BF16)
        x1, hf, sct, shared = _mix(
            x2, ydn.reshape(t, -1), ysg, mod3, row(mix_pre_norm[l]), row(mix_post_norm[l]), row(ffn_pre_norm[l]),
            w_branch_gate[l].astype(BF16), row(b_branch_gate[l]), w_proj_dn[l].astype(BF16), w_proj_sg[l].astype(BF16),
            w_out[l].astype(BF16), wrh, wrl, w_sh_gate[l].astype(BF16), w_sh_up[l].astype(BF16), w_sh_down[l].astype(BF16),
            seq, tm)

        idx, wtt, rank, cnt = _route(sct, router_bias[l].reshape(-1, 1), tr)
        dest, meta = _dest(cnt, idx, rank, tr)
        xs = _scatter(meta[:, 2], meta[:, 3], _dest_tiles(dest, tsc), hf, nblk * MOE_BLOCK, tsc)
        ys = _moe(meta[:, 0], meta[:, 1], xs, w_exp_gate[l], w_exp_up[l], w_exp_down[l])
        x2 = _combine(_dest_tiles(dest, tcm), ys, wtt, x1, shared, mod3, row(ffn_post_norm[l]), seq, tcm)
    return x2.reshape(nb, seq, d)
```

```python
import functools

import jax
import jax.numpy as jnp
from jax import lax
from jax.experimental import pallas as pl
from jax.experimental.pallas import tpu as pltpu

F32 = jnp.float32
BF16 = jnp.bfloat16
I32 = jnp.int32

D_MODEL = 1024
DN_HEADS = 4
DN_HEAD_DIM = 128
DN_WIDTH = DN_HEADS * DN_HEAD_DIM
DN_CHUNK = 64
DN_CHUNKS_PER_STEP = 2
CONV_WIDTH = 4
SG_GROUPS = 4
SG_GROUP_DIM = 128
SG_WIDTH = SG_GROUPS * SG_GROUP_DIM
SG_CHUNK = 128
N_EXPERTS = 256
N_EXPERT_GROUPS = 8
GROUP_SIZE = N_EXPERTS // N_EXPERT_GROUPS
TOPK_GROUPS = 4
TOP_K = 8
EXPERT_HIDDEN = 256
ROUTED_SCALE = 2.5
MOE_BLOCK = 256
EPS = 1e-6

LANES = 128
SUBLANES = 8
ROW_TILE = SUBLANES
VMEM_LIMIT = 56 * 1024 * 1024
NEG_INF = float("-inf")


def _dot(a, b):
    return jnp.dot(a, b, preferred_element_type=F32)


def _dot_nt(a, b):
    return lax.dot_general(a, b, (((1,), (1,)), ((), ())), preferred_element_type=F32)


def _dot_tn(a, b):
    return lax.dot_general(a, b, (((0,), (0,)), ((), ())), preferred_element_type=F32)


def _split2(x):
    hi = x.astype(BF16)
    lo = (x - hi.astype(F32)).astype(BF16)
    return hi, lo


def _split3(x):
    hi = x.astype(BF16)
    r = x - hi.astype(F32)
    mid = r.astype(BF16)
    lo = (r - mid.astype(F32)).astype(BF16)
    return hi, mid, lo


def _dot_hp(a, b, dot=_dot):
    ah, al = _split2(a)
    bh, bl = _split2(b)
    return dot(ah, bh) + (dot(ah, bl) + dot(al, bh))


def _sigmoid(x):
    return 1.0 / (1.0 + jnp.exp(-x))


def _silu(x):
    return x * _sigmoid(x)


def _gelu(x):
    return 0.5 * x * (1.0 + lax.erf(x * (2.0 ** -0.5)))


def _softplus(x):
    return jnp.maximum(x, 0.0) + jnp.log1p(jnp.exp(-jnp.abs(x)))


def _rms(x):
    return x * lax.rsqrt(jnp.mean(x * x, axis=-1, keepdims=True) + EPS)


def _params(*sem):
    return pltpu.CompilerParams(dimension_semantics=sem, vmem_limit_bytes=VMEM_LIMIT)


def _const_spec(shape):
    nd = len(shape)
    return pl.BlockSpec(shape, lambda *_: (0,) * nd)


def _ada_kernel(c_ref, w_ref, b_ref, o_ref):
    cs = _silu(c_ref[...])
    o_ref[...] = _dot_hp(cs, w_ref[...]) + b_ref[...]


def _ada(c, ada_w, ada_b):
    b, d = c.shape
    n = ada_w.shape[1]
    tn = d
    return pl.pallas_call(
        _ada_kernel,
        grid=(n // tn,),
        in_specs=[_const_spec((b, d)), pl.BlockSpec((d, tn), lambda j: (0, j)), pl.BlockSpec((1, tn), lambda j: (0, j))],
        out_specs=pl.BlockSpec((b, tn), lambda j: (0, j)),
        out_shape=jax.ShapeDtypeStruct((b, n), F32),
        compiler_params=_params("arbitrary"),
        name="ada",
    )(c, ada_w, ada_b.reshape(1, n))


W1_COLS = 4 * DN_WIDTH + LANES


def _in_kernel(x_ref, mod_ref, pn_ref, w1_ref, wuv_ref, lnw_ref, lnb_ref, sgw_ref, sgbt_ref,
               qkv_ref, z_ref, ab_ref, ysg_ref):
    tm = x_ref.shape[0]
    mod = mod_ref[0]
    hm = _rms(x_ref[...]) * pn_ref[...] * (1.0 + mod[1:2, :]) + mod[0:1, :]
    hb = hm.astype(BF16)
    p1 = _dot(hb, w1_ref[...])
    qkv_ref[...] = p1[:, :3 * DN_WIDTH]
    z_ref[...] = p1[:, 3 * DN_WIDTH:4 * DN_WIDTH]
    ab_ref[...] = p1[:, 4 * DN_WIDTH:]
    uv = _dot(hb, wuv_ref[...])
    u = _gelu(uv[:, :SG_WIDTH])
    vg = _gelu(uv[:, SG_WIDTH:])
    mu = jnp.mean(vg, axis=-1, keepdims=True)
    dv = vg - mu
    var = jnp.mean(dv * dv, axis=-1, keepdims=True)
    vgn = (dv * lax.rsqrt(var + EPS) * lnw_ref[...] + lnb_ref[...]).astype(BF16)
    row = lax.broadcasted_iota(I32, (SG_CHUNK, SG_CHUNK), 0)
    col = lax.broadcasted_iota(I32, (SG_CHUNK, SG_CHUNK), 1)
    tril = row >= col
    for g in range(SG_GROUPS):
        wg = jnp.where(tril, sgw_ref[g], 0.0).astype(BF16)
        bg = sgbt_ref[:, g:g + 1]
        cs = slice(g * SG_GROUP_DIM, (g + 1) * SG_GROUP_DIM)
        for n in range(tm // SG_CHUNK):
            rs = slice(n * SG_CHUNK, (n + 1) * SG_CHUNK)
            mixed = _dot(wg, vgn[rs, cs]) + bg
            ysg_ref[rs, cs] = (u[rs, cs] * mixed).astype(BF16)


def _in_proj(x2, mod3, pre_norm, w1, wuv, sg_ln_w, sg_ln_b, sg_w, sg_bt, seq, tm):
    t, d = x2.shape
    tiles_per_batch = seq // tm
    tok = lambda i: (i, 0)
    return pl.pallas_call(
        _in_kernel,
        grid=(t // tm,),
        in_specs=[
            pl.BlockSpec((tm, d), tok),
            pl.BlockSpec((1, 6, d), lambda i: (i // tiles_per_batch, 0, 0)),
            _const_spec((1, d)),
            _const_spec(w1.shape),
            _const_spec(wuv.shape),
            _const_spec((1, SG_WIDTH)),
            _const_spec((1, SG_WIDTH)),
            _const_spec(sg_w.shape),
            _const_spec(sg_bt.shape),
        ],
        out_specs=[
            pl.BlockSpec((tm, 3 * DN_WIDTH), tok),
            pl.BlockSpec((tm, DN_WIDTH), tok),
            pl.BlockSpec((tm, LANES), tok),
            pl.BlockSpec((tm, SG_WIDTH), tok),
        ],
        out_shape=[
            jax.ShapeDtypeStruct((t, 3 * DN_WIDTH), F32),
            jax.ShapeDtypeStruct((t, DN_WIDTH), F32),
            jax.ShapeDtypeStruct((t, LANES), F32),
            jax.ShapeDtypeStruct((t, SG_WIDTH), BF16),
        ],
        compiler_params=_params("arbitrary"),
        name="in_proj",
    )(x2, mod3, pre_norm, w1, wuv, sg_ln_w, sg_ln_b, sg_w, sg_bt)


def _unit_lower_inverses(a_list):
    c = a_list[0].shape[0]
    i = lax.broadcasted_iota(I32, (c, c), 0)
    j = lax.broadcasted_iota(I32, (c, c), 1)
    eye = (i == j).astype(F32)
    first = (i == j + 1) & ((i & 1) == 1)
    d_list = [eye - jnp.where(first, a, 0.0) for a in a_list]
    b = 2
    while b < c:
        shift = b.bit_length()
        off = ((i >> shift) == (j >> shift)) & ((i & b) != 0) & ((j & b) == 0)
        a_parts = [jnp.where(off, a, 0.0).astype(BF16) for a in a_list]
        d_parts = [d.astype(BF16) for d in d_list]
        t_list = [_dot(dp, ap) for dp, ap in zip(d_parts, a_parts)]
        d_list = [d - _dot(t.astype(BF16), dp) for d, t, dp in zip(d_list, t_list, d_parts)]
        b *= 2
    return d_list


def _dn_kernel(qkv_ref, z_ref, ab_ref, convw_ref, alog_ref, dtb_ref, nw_ref, y_ref, carry_ref, state_ref):
    nb, rows = qkv_ref.shape[0], qkv_ref.shape[1]
    c = DN_CHUNK
    nc = rows // c
    hd = DN_HEAD_DIM
    heads = [(b, h) for b in range(nb) for h in range(DN_HEADS)]
    probs = [(b, ci, h) for b in range(nb) for ci in range(nc) for h in range(DN_HEADS)]

    @pl.when(pl.program_id(0) == 0)
    def _():
        carry_ref[...] = jnp.zeros_like(carry_ref)
        state_ref[...] = jnp.zeros_like(state_ref)

    ab = ab_ref[...].reshape(nb * rows, LANES)
    g = -jnp.exp(alog_ref[...]) * _softplus(ab + dtb_ref[...])
    beta_all = _sigmoid(ab)
    ri = lax.broadcasted_iota(I32, (nb * rows, nb * rows), 0)
    ci_ = lax.broadcasted_iota(I32, (nb * rows, nb * rows), 1)
    tri = ((ri >= ci_) & ((ri // c) == (ci_ // c))).astype(BF16)
    gh, gm, gl = _split3(g)
    gc = _dot(tri, gh) + (_dot(tri, gm) + _dot(tri, gl))
    gct = gc.T

    i = lax.broadcasted_iota(I32, (c, c), 0)
    j = lax.broadcasted_iota(I32, (c, c), 1)
    causal = i >= j
    strict = i > j
    row8 = lax.broadcasted_iota(I32, (SUBLANES, 3 * DN_WIDTH), 0)

    acts = []
    for b in range(nb):
        xc = qkv_ref[b]
        prev = carry_ref[b]
        acc = xc * convw_ref[CONV_WIDTH - 1:CONV_WIDTH, :]
        for s in range(1, CONV_WIDTH):
            rolled = pltpu.roll(xc, s, axis=0)
            top = jnp.where(row8 < s, pltpu.roll(prev, s, axis=0), rolled[:SUBLANES])
            shifted = jnp.concatenate([top, rolled[SUBLANES:]], axis=0)
            acc = acc + shifted * convw_ref[CONV_WIDTH - 1 - s:CONV_WIDTH - s, :]
        carry_ref[b] = xc[rows - SUBLANES:]
        acts.append(_silu(acc))

    gcs, e_gcs, e_rems, e_lasts = {}, {}, {}, {}
    for b in range(nb):
        for ci in range(nc):
            r0 = b * rows + ci * c
            gcb = gc[r0:r0 + c]
            g_last = gcb[c - 1:c, :]
            gcs[b, ci] = gcb
            e_gcs[b, ci] = jnp.exp(gcb)
            e_rems[b, ci] = jnp.exp(g_last - gcb)
            e_lasts[b, ci] = jnp.exp(g_last)

    qn, kn, knb, kb, vb, decay = {}, {}, {}, {}, {}, {}
    for p in probs:
        b, ci, h = p
        r0 = b * rows + ci * c
        act = acts[b][ci * c:(ci + 1) * c]
        q = act[:, h * hd:(h + 1) * hd]
        k = act[:, DN_WIDTH + h * hd:DN_WIDTH + (h + 1) * hd]
        v = act[:, 2 * DN_WIDTH + h * hd:2 * DN_WIDTH + (h + 1) * hd]
        qn[p] = q * lax.rsqrt(jnp.sum(q * q, axis=-1, keepdims=True) + EPS) * (hd ** -0.5)
        kn[p] = k * lax.rsqrt(jnp.sum(k * k, axis=-1, keepdims=True) + EPS)
        beta = beta_all[r0:r0 + c, DN_HEADS + h:DN_HEADS + h + 1]
        diff = gcs[b, ci][:, h:h + 1] - gct[h:h + 1, r0:r0 + c]
        decay[p] = jnp.where(causal, jnp.exp(jnp.where(causal, diff, 0.0)), 0.0)
        kb[p] = kn[p] * beta
        vb[p] = v * beta
        knb[p] = kn[p].astype(BF16)

    kk = {p: _dot_nt(kb[p].astype(BF16), knb[p]) for p in probs}
    qk = {p: _dot_nt(qn[p].astype(BF16), knb[p]) for p in probs}
    tinv = dict(zip(probs, _unit_lower_inverses([jnp.where(strict, kk[p] * decay[p], 0.0) for p in probs])))
    rhs = {p: jnp.concatenate([vb[p], kb[p] * e_gcs[p[0], p[1]][:, p[2]:p[2] + 1]], axis=1) for p in probs}
    sol = {p: _dot(tinv[p].astype(BF16), rhs[p].astype(BF16)) for p in probs}
    lhs = {p: jnp.concatenate([sol[p][:, hd:], qn[p] * e_gcs[p[0], p[1]][:, p[2]:p[2] + 1]], axis=0).astype(BF16)
           for p in probs}
    qkm = {p: jnp.where(causal, qk[p] * decay[p], 0.0).astype(BF16) for p in probs}
    kd = {p: (kn[p] * e_rems[p[0], p[1]][:, p[2]:p[2] + 1]).astype(BF16) for p in probs}

    state = {bh: state_ref[bh[0] * DN_HEADS + bh[1]] for bh in heads}
    for ci in range(nc):
        cur = [(b, ci, h) for (b, h) in heads]
        ws = {p: _dot(lhs[p], state[p[0], p[2]].astype(BF16)) for p in cur}
        vnb = {p: (sol[p][:, :hd] - ws[p][:c]).astype(BF16) for p in cur}
        o = {p: ws[p][c:] + _dot(qkm[p], vnb[p]) for p in cur}
        upd = {p: _dot_tn(kd[p], vnb[p]) for p in cur}
        for p in cur:
            b, _, h = p
            state[b, h] = state[b, h] * e_lasts[b, ci][:, h:h + 1] + upd[p]
            zh = z_ref[b, ci * c:(ci + 1) * c, h * hd:(h + 1) * hd]
            y_ref[b, ci * c:(ci + 1) * c, h * hd:(h + 1) * hd] = (_rms(o[p]) * nw_ref[...] * _silu(zh)).astype(BF16)
    for bh in heads:
        state_ref[bh[0] * DN_HEADS + bh[1]] = state[bh]


def _delta_net(qkv3, z3, ab3, conv_w, alog_row, dtb_row, dn_norm_w):
    nb, s, _ = qkv3.shape
    c = DN_CHUNK * DN_CHUNKS_PER_STEP
    blk = lambda n: (0, n, 0)
    return pl.pallas_call(
        _dn_kernel,
        grid=(s // c,),
        in_specs=[
            pl.BlockSpec((nb, c, 3 * DN_WIDTH), blk),
            pl.BlockSpec((nb, c, DN_WIDTH), blk),
            pl.BlockSpec((nb, c, LANES), blk),
            _const_spec(conv_w.shape),
            _const_spec((1, LANES)),
            _const_spec((1, LANES)),
            _const_spec((1, DN_HEAD_DIM)),
        ],
        out_specs=pl.BlockSpec((nb, c, DN_WIDTH), blk),
        out_shape=jax.ShapeDtypeStruct((nb, s, DN_WIDTH), BF16),
        scratch_shapes=[
            pltpu.VMEM((nb, SUBLANES, 3 * DN_WIDTH), F32),
            pltpu.VMEM((nb * DN_HEADS, DN_HEAD_DIM, DN_HEAD_DIM), F32),
        ],
        compiler_params=_params("arbitrary"),
        name="delta_net",
    )(qkv3, z3, ab3, conv_w, alog_row, dtb_row, dn_norm_w)


def _mix_kernel(x_ref, ydn_ref, ysg_ref, mod_ref, pn_ref, postn_ref, fpn_ref,
                wbg_ref, bbg_ref, wpd_ref, wps_ref, wout_ref, wrh_ref, wrl_ref, wsg_ref, wsu_ref, wsd_ref,
                x1_ref, hf_ref, sct_ref, sh_ref):
    d = x_ref.shape[1]
    x = x_ref[...]
    mod = mod_ref[0]
    hm = _rms(x) * pn_ref[...] * (1.0 + mod[1:2, :]) + mod[0:1, :]
    gates = _sigmoid(_dot(hm.astype(BF16), wbg_ref[...]) + bbg_ref[...])
    merged = gates[:, :d] * _dot(ydn_ref[...], wpd_ref[...]) + gates[:, d:] * _dot(ysg_ref[...], wps_ref[...])
    y = _dot(merged.astype(BF16), wout_ref[...])
    x1 = x + mod[2:3, :] * (_rms(y) * postn_ref[...])
    x1_ref[...] = x1
    hf = _rms(x1) * fpn_ref[...] * (1.0 + mod[4:5, :]) + mod[3:4, :]
    for j in range(ROW_TILE):
        hf_ref[pl.ds(j, hf.shape[0], stride=ROW_TILE), :] = hf[:, j * LANES:(j + 1) * LANES]
    hh, hl = _split2(hf)
    logits_t = _dot_nt(wrh_ref[...], hh) + (_dot_nt(wrh_ref[...], hl) + _dot_nt(wrl_ref[...], hh))
    sct_ref[...] = _sigmoid(logits_t)
    hid = _silu(_dot(hh, wsg_ref[...])) * _dot(hh, wsu_ref[...])
    sh_ref[...] = _dot(hid.astype(BF16), wsd_ref[...])


def _mix(x2, ydn, ysg, mod3, pre_norm, post_norm, ffn_pre_norm, wbg, bbg, wpd, wps, wout, wrh, wrl, wsg, wsu, wsd, seq, tm):
    t, d = x2.shape
    tiles_per_batch = seq // tm
    tok = lambda i: (i, 0)
    consts = [pre_norm, post_norm, ffn_pre_norm, wbg, bbg, wpd, wps, wout, wrh, wrl, wsg, wsu, wsd]
    return pl.pallas_call(
        _mix_kernel,
        grid=(t // tm,),
        in_specs=[
            pl.BlockSpec((tm, d), tok),
            pl.BlockSpec((tm, DN_WIDTH), tok),
            pl.BlockSpec((tm, SG_WIDTH), tok),
            pl.BlockSpec((1, 6, d), lambda i: (i // tiles_per_batch, 0, 0)),
        ] + [_const_spec(a.shape) for a in consts],
        out_specs=[
            pl.BlockSpec((tm, d), tok),
            pl.BlockSpec((tm * ROW_TILE, LANES), tok),
            pl.BlockSpec((N_EXPERTS, tm), lambda i: (0, i)),
            pl.BlockSpec((tm, d), tok),
        ],
        out_shape=[
            jax.ShapeDtypeStruct((t, d), F32),
            jax.ShapeDtypeStruct((t * ROW_TILE, LANES), F32),
            jax.ShapeDtypeStruct((N_EXPERTS, t), F32),
            jax.ShapeDtypeStruct((t, d), F32),
        ],
        compiler_params=_params("arbitrary"),
        name="mix",
    )(x2, ydn, ysg, mod3, *consts)


def _first_argmax(vals, idx):
    m = jnp.max(vals, axis=0, keepdims=True)
    first = jnp.min(jnp.where(vals == m, idx, jnp.int32(2 ** 30)), axis=0, keepdims=True)
    return m, first


def _route_kernel(sct_ref, bias_ref, idx_ref, wtt_ref, rank_ref, cnt_ref, carry_ref):
    tm = sct_ref.shape[1]

    @pl.when(pl.program_id(0) == 0)
    def _():
        carry_ref[...] = jnp.zeros_like(carry_ref)

    scores = sct_ref[...]
    sel = scores + bias_ref[...]
    erow = lax.broadcasted_iota(I32, (N_EXPERTS, tm), 0)
    grow = lax.broadcasted_iota(I32, (GROUP_SIZE, tm), 0)

    gs = []
    for gidx in range(N_EXPERT_GROUPS):
        sg = sel[gidx * GROUP_SIZE:(gidx + 1) * GROUP_SIZE]
        m1, first = _first_argmax(sg, grow)
        m2 = jnp.max(jnp.where(grow == first, NEG_INF, sg), axis=0, keepdims=True)
        gs.append(m1 + m2)
    gsc = jnp.concatenate(gs, axis=0)
    giota = lax.broadcasted_iota(I32, (N_EXPERT_GROUPS, tm), 0)
    gmask = jnp.zeros((N_EXPERT_GROUPS, tm), F32)
    cur = gsc
    for _ in range(TOPK_GROUPS):
        _, gi = _first_argmax(cur, giota)
        pick = giota == gi
        gmask = jnp.where(pick, 1.0, gmask)
        cur = jnp.where(pick, NEG_INF, cur)
    masked = jnp.concatenate(
        [jnp.where(gmask[gidx:gidx + 1, :] > 0.5, sel[gidx * GROUP_SIZE:(gidx + 1) * GROUP_SIZE], NEG_INF)
         for gidx in range(N_EXPERT_GROUPS)], axis=0)

    cur = masked
    idxs, wts = [], []
    onehot = jnp.zeros((N_EXPERTS, tm), F32)
    for _ in range(TOP_K):
        _, ei = _first_argmax(cur, erow)
        pick = erow == ei
        idxs.append(ei)
        wts.append(jnp.sum(jnp.where(pick, scores, 0.0), axis=0, keepdims=True))
        onehot = jnp.where(pick, 1.0, onehot)
        cur = jnp.where(pick, NEG_INF, cur)
    idx = jnp.concatenate(idxs, axis=0)
    wt = jnp.concatenate(wts, axis=0)
    wt = wt / jnp.sum(wt, axis=0, keepdims=True) * ROUTED_SCALE
    idx_ref[...] = idx
    wpad = jnp.concatenate([wt, jnp.zeros((LANES - TOP_K, tm), F32)], axis=0)
    wtt_ref[...] = wpad.T

    ti = lax.broadcasted_iota(I32, (tm, tm), 0)
    tj = lax.broadcasted_iota(I32, (tm, tm), 1)
    upper = (ti < tj).astype(BF16)
    before = _dot(onehot.astype(BF16), upper) + carry_ref[...]
    rank_ref[...] = jnp.concatenate(
        [jnp.sum(jnp.where(erow == idxs[kk], before, 0.0), axis=0, keepdims=True) for kk in range(TOP_K)],
        axis=0).astype(I32)
    total = carry_ref[...] + jnp.sum(onehot, axis=1, keepdims=True)
    carry_ref[...] = total
    cnt_ref[...] = total


def _route(sct, bias_col, tm):
    e, t = sct.shape
    tile = lambda i: (0, i)
    return pl.pallas_call(
        _route_kernel,
        grid=(t // tm,),
        in_specs=[pl.BlockSpec((e, tm), tile), _const_spec((e, 1))],
        out_specs=[
            pl.BlockSpec((TOP_K, tm), tile),
            pl.BlockSpec((tm, LANES), lambda i: (i, 0)),
            pl.BlockSpec((TOP_K, tm), tile),
            _const_spec((e, 1)),
        ],
        out_shape=[
            jax.ShapeDtypeStruct((TOP_K, t), I32),
            jax.ShapeDtypeStruct((t, LANES), F32),
            jax.ShapeDtypeStruct((TOP_K, t), I32),
            jax.ShapeDtypeStruct((e, 1), F32),
        ],
        scratch_shapes=[pltpu.VMEM((e, 1), F32)],
        compiler_params=_params("arbitrary"),
        name="route",
    )(sct, bias_col)


def _dest_kernel(cnt_ref, idx_ref, rank_ref, dest_ref, meta_ref):
    tm = idx_ref.shape[1]
    e = N_EXPERTS
    cnt = cnt_ref[...]
    padded = jnp.floor((cnt + (MOE_BLOCK - 1)) * (1.0 / MOE_BLOCK)) * MOE_BLOCK
    pw = jnp.broadcast_to(padded, (e, LANES))
    ri = lax.broadcasted_iota(I32, (e, e), 0)
    ci = lax.broadcasted_iota(I32, (e, e), 1)
    lower = (ri >= ci).astype(BF16)
    ph, pm, plo = _split3(pw)
    pends = _dot(lower, ph) + (_dot(lower, pm) + _dot(lower, plo))
    pstart = pends - pw
    erow = lax.broadcasted_iota(I32, (e, tm), 0)
    idx = idx_ref[...]
    rows = [jnp.sum(jnp.where(erow == idx[kk:kk + 1, :], pstart[:, 0:1], 0.0), axis=0, keepdims=True)
            for kk in range(TOP_K)]
    dest_ref[...] = jnp.concatenate(rows, axis=0).astype(I32) + rank_ref[...]
    lane = lax.broadcasted_iota(I32, (e, LANES), 1)
    blocks = jnp.where(lane == 0, pstart, pw) * (1.0 / MOE_BLOCK)
    pads = jnp.where(lane == 2, pstart + cnt, pw - cnt)
    meta_ref[...] = jnp.where(lane < 2, blocks, pads).astype(I32)


def _dest(cnt, idx, rank, tm):
    e = cnt.shape[0]
    t = idx.shape[1]
    tile = lambda i: (0, i)
    return pl.pallas_call(
        _dest_kernel,
        grid=(t // tm,),
        in_specs=[_const_spec((e, 1)), pl.BlockSpec((TOP_K, tm), tile), pl.BlockSpec((TOP_K, tm), tile)],
        out_specs=[pl.BlockSpec((TOP_K, tm), tile), _const_spec((e, LANES))],
        out_shape=[jax.ShapeDtypeStruct((TOP_K, t), I32), jax.ShapeDtypeStruct((e, LANES), I32)],
        compiler_params=_params("arbitrary"),
        name="dest",
    )(cnt, idx, rank)


def _row_copy(src, dst, sem):
    return pltpu.make_async_copy(src, dst, sem)


def _rows(first, count=1):
    return pl.ds(pl.multiple_of(first * ROW_TILE, ROW_TILE), count * ROW_TILE)


PAD_PIECES = tuple(MOE_BLOCK >> s for s in range(1, MOE_BLOCK.bit_length()))


def _zero_unassigned_rows(pad_start_ref, pad_rows_ref, xs_out, zero_ref, sem):
    zero_ref[...] = jnp.zeros_like(zero_ref)
    ne = pad_start_ref.shape[0]
    total = xs_out.shape[0] // (MOE_BLOCK * ROW_TILE)

    def piece(rows, pos):
        return pltpu.make_async_copy(zero_ref.at[_rows(0, rows)], xs_out.at[_rows(pos, rows)], sem)

    def pads(wait):
        def per_expert(ex, carry):
            pos = pad_start_ref[ex]
            pad = pad_rows_ref[ex]
            for rows in PAD_PIECES:
                @pl.when((pad & rows) != 0)
                def _():
                    piece(rows, pos).wait() if wait else piece(rows, pos).start()
                pos = pos + (pad & rows)
            return carry
        lax.fori_loop(0, ne, per_expert, 0)

    def tail(wait):
        used = (pad_start_ref[ne - 1] + pad_rows_ref[ne - 1]) // MOE_BLOCK

        def per_block(blk, carry):
            cp = piece(MOE_BLOCK, blk * MOE_BLOCK)
            cp.wait() if wait else cp.start()
            return carry
        lax.fori_loop(used, total, per_block, 0)

    pads(False)
    tail(False)
    pads(True)
    tail(True)


def _scatter_kernel(pad_start_ref, pad_rows_ref, dest_hbm, hf_ref, xs_out, dest_smem, zero_ref, sem_idx, sem_rows, sem_zero):
    tm = hf_ref.shape[0] // ROW_TILE
    i = pl.program_id(0)

    @pl.when(i == 0)
    def _():
        _zero_unassigned_rows(pad_start_ref, pad_rows_ref, xs_out, zero_ref, sem_zero)

    idx_copy = pltpu.make_async_copy(dest_hbm.at[i], dest_smem, sem_idx)
    idx_copy.start()
    idx_copy.wait()

    def issue(tok, carry):
        for kk in range(TOP_K):
            slot = dest_smem[kk * tm + tok]
            _row_copy(hf_ref.at[_rows(tok)], xs_out.at[_rows(slot)], sem_rows).start(priority=kk % 2)
        return carry

    lax.fori_loop(0, tm, issue, 0)
    for kk in range(TOP_K):
        _row_copy(hf_ref, xs_out.at[_rows(0, tm)], sem_rows).wait()


def _scatter(pad_start, pad_rows, dest_tiles, hf, n_slots, tm):
    t = hf.shape[0] // ROW_TILE
    n_tiles = t // tm
    return pl.pallas_call(
        _scatter_kernel,
        grid_spec=pltpu.PrefetchScalarGridSpec(
            num_scalar_prefetch=2,
            grid=(n_tiles,),
            in_specs=[pl.BlockSpec(memory_space=pl.ANY), pl.BlockSpec((tm * ROW_TILE, LANES), lambda i, ps, pr: (i, 0))],
            out_specs=pl.BlockSpec(memory_space=pl.ANY),
            scratch_shapes=[
                pltpu.SMEM((TOP_K * tm,), I32),
                pltpu.VMEM((MOE_BLOCK * ROW_TILE, LANES), hf.dtype),
                pltpu.SemaphoreType.DMA,
                pltpu.SemaphoreType.DMA,
                pltpu.SemaphoreType.DMA,
            ],
        ),
        out_shape=jax.ShapeDtypeStruct((n_slots * ROW_TILE, LANES), hf.dtype),
        compiler_params=_params("arbitrary"),
        name="scatter",
    )(pad_start, pad_rows, dest_tiles, hf)


MOE_BUFFERS = 4


def _moe_kernel(bstart_ref, nblk_ref, xs_hbm, wg_ref, wu_ref, wd_ref, ys_hbm,
                xbuf, ybuf, wgb_ref, wub_ref, wdb_ref, sem_in, sem_out):
    ex = pl.program_id(0)
    last = pl.num_programs(0) - 1
    n = nblk_ref[ex]
    b0 = bstart_ref[ex]
    used = bstart_ref[last] + nblk_ref[last]

    def in_copy(g):
        slot = lax.rem(g, MOE_BUFFERS)
        return pltpu.make_async_copy(xs_hbm.at[_rows(g * MOE_BLOCK, MOE_BLOCK)], xbuf.at[slot], sem_in.at[slot])

    def out_copy(g):
        slot = lax.rem(g, MOE_BUFFERS)
        return pltpu.make_async_copy(ybuf.at[slot], ys_hbm.at[_rows(g * MOE_BLOCK, MOE_BLOCK)], sem_out.at[slot])

    @pl.when(ex == 0)
    def _():
        for s in range(MOE_BUFFERS - 1):
            @pl.when(s < used)
            def _():
                in_copy(s).start()

    wgb_ref[...] = wg_ref[0].astype(BF16)
    wub_ref[...] = wu_ref[0].astype(BF16)
    wdb_ref[...] = wd_ref[0].astype(BF16)

    def body(g, carry):
        slot = lax.rem(g, MOE_BUFFERS)
        in_copy(g).wait()

        @pl.when(g + (MOE_BUFFERS - 1) < used)
        def _():
            in_copy(g + (MOE_BUFFERS - 1)).start()

        @pl.when(g >= MOE_BUFFERS)
        def _():
            out_copy(g - MOE_BUFFERS).wait()

        xb = jnp.concatenate([xbuf[slot, pl.ds(j, MOE_BLOCK, stride=ROW_TILE), :] for j in range(ROW_TILE)],
                             axis=1).astype(BF16)
        hid = _silu(_dot(xb, wgb_ref[...])) * _dot(xb, wub_ref[...])
        y = _dot(hid.astype(BF16), wdb_ref[...])
        for j in range(ROW_TILE):
            ybuf[slot, pl.ds(j, MOE_BLOCK, stride=ROW_TILE), :] = y[:, j * LANES:(j + 1) * LANES]
        out_copy(g).start()
        return carry

    lax.fori_loop(b0, b0 + n, body, 0)

    @pl.when(ex == last)
    def _():
        for s in range(MOE_BUFFERS):
            @pl.when(s < used)
            def _():
                out_copy(used - 1 - s).wait()

        total = ys_hbm.shape[0] // (MOE_BLOCK * ROW_TILE)
        ybuf[0] = jnp.zeros(ybuf.shape[1:], ybuf.dtype)

        def tail_copy(blk):
            return pltpu.make_async_copy(ybuf.at[0], ys_hbm.at[_rows(blk * MOE_BLOCK, MOE_BLOCK)], sem_out.at[0])

        def start_tail(blk, carry):
            tail_copy(blk).start()
            return carry

        def wait_tail(blk, carry):
            tail_copy(blk).wait()
            return carry

        lax.fori_loop(used, total, start_tail, 0)
        lax.fori_loop(used, total, wait_tail, 0)


def _moe(bstart, nblk_e, xs, w_gate, w_up, w_down):
    ne, d, eh = w_gate.shape
    wspec = lambda shape: pl.BlockSpec(shape, lambda ex, bs, nb: (ex, 0, 0))
    return pl.pallas_call(
        _moe_kernel,
        grid_spec=pltpu.PrefetchScalarGridSpec(
            num_scalar_prefetch=2,
            grid=(ne,),
            in_specs=[pl.BlockSpec(memory_space=pl.ANY), wspec((1, d, eh)), wspec((1, d, eh)), wspec((1, eh, d))],
            out_specs=pl.BlockSpec(memory_space=pl.ANY),
            scratch_shapes=[
                pltpu.VMEM((MOE_BUFFERS, MOE_BLOCK * ROW_TILE, LANES), F32),
                pltpu.VMEM((MOE_BUFFERS, MOE_BLOCK * ROW_TILE, LANES), F32),
                pltpu.VMEM((d, eh), BF16), pltpu.VMEM((d, eh), BF16), pltpu.VMEM((eh, d), BF16),
                pltpu.SemaphoreType.DMA((MOE_BUFFERS,)),
                pltpu.SemaphoreType.DMA((MOE_BUFFERS,)),
            ],
        ),
        out_shape=jax.ShapeDtypeStruct(xs.shape, F32),
        compiler_params=_params("arbitrary"),
        name="moe",
    )(bstart, nblk_e, xs, w_gate, w_up, w_down)


def _combine_kernel(dest_hbm, ys_hbm, wtt_ref, x1_ref, sh_ref, mod_ref, postn_ref, o_ref, dest_smem0, dest_smem1, buf_ref, sem_idx, sem_rows):
    tc = x1_ref.shape[0]
    i = pl.program_id(0)
    n = pl.num_programs(0)
    dest_smem = (dest_smem0, dest_smem1)

    def idx_copy(tile, slot):
        return pltpu.make_async_copy(dest_hbm.at[tile], dest_smem[slot], sem_idx.at[slot])

    def issue_rows(slot):
        def issue(tok, carry):
            for kk in range(TOP_K):
                row = dest_smem[slot][kk * tc + tok]
                _row_copy(ys_hbm.at[_rows(row)], buf_ref.at[slot, kk, _rows(tok)],
                          sem_rows.at[slot]).start(priority=kk % 2)
            return carry
        lax.fori_loop(0, tc, issue, 0)

    @pl.when(i == 0)
    def _():
        idx_copy(0, 0).start()
        idx_copy(0, 0).wait()

        @pl.when(n > 1)
        def _():
            idx_copy(1, 1).start()
        issue_rows(0)

    def step(cur):
        nxt = 1 - cur

        @pl.when(i + 1 < n)
        def _():
            idx_copy(i + 1, nxt).wait()

            @pl.when(i + 2 < n)
            def _():
                idx_copy(i + 2, cur).start()
            issue_rows(nxt)

        for kk in range(TOP_K):
            _row_copy(ys_hbm.at[_rows(0, tc)], buf_ref.at[cur, kk], sem_rows.at[cur]).wait()

        wtt = wtt_ref[...]
        wk = [jnp.broadcast_to(wtt[:, kk:kk + 1], (tc, LANES)) for kk in range(TOP_K)]
        cols = []
        for j in range(ROW_TILE):
            lane_group = pl.ds(j, tc, stride=ROW_TILE)
            acc = buf_ref[cur, 0, lane_group, :] * wk[0]
            for kk in range(1, TOP_K):
                acc = acc + buf_ref[cur, kk, lane_group, :] * wk[kk]
            cols.append(acc)
        y = jnp.concatenate(cols, axis=1) + sh_ref[...]
        mod = mod_ref[0]
        o_ref[...] = x1_ref[...] + mod[5:6, :] * (_rms(y) * postn_ref[...])

    parity = lax.rem(i, 2)
    for cur in range(2):
        @pl.when(parity == cur)
        def _():
            step(cur)


def _combine(dest_tiles, ys, wtt, x1, shared, mod3, ffn_post_norm, seq, tc):
    t, d = x1.shape
    tiles_per_batch = seq // tc
    tok = lambda i: (i, 0)
    return pl.pallas_call(
        _combine_kernel,
        grid=(t // tc,),
        in_specs=[
            pl.BlockSpec(memory_space=pl.ANY),
            pl.BlockSpec(memory_space=pl.ANY),
            pl.BlockSpec((tc, LANES), tok),
            pl.BlockSpec((tc, d), tok),
            pl.BlockSpec((tc, d), tok),
            pl.BlockSpec((1, 6, d), lambda i: (i // tiles_per_batch, 0, 0)),
            _const_spec((1, d)),
        ],
        out_specs=pl.BlockSpec((tc, d), tok),
        out_shape=jax.ShapeDtypeStruct((t, d), F32),
        scratch_shapes=[
            pltpu.SMEM((TOP_K * tc,), I32),
            pltpu.SMEM((TOP_K * tc,), I32),
            pltpu.VMEM((2, TOP_K, tc * ROW_TILE, LANES), F32),
            pltpu.SemaphoreType.DMA((2,)),
            pltpu.SemaphoreType.DMA((2,)),
        ],
        compiler_params=_params("arbitrary"),
        name="combine",
    )(dest_tiles, ys, wtt, x1, shared, mod3, ffn_post_norm)


def _tile(n, want):
    t = min(n, want)
    assert n % t == 0
    return t


def _dest_tiles(dest, tile):
    k, t = dest.shape
    return dest.reshape(k, t // tile, tile).transpose(1, 0, 2).reshape(t // tile, k * tile)


def kernel(x, c, ada_w, ada_b, mix_pre_norm, mix_post_norm, w_in, conv_w, a_log, dt_bias, dn_norm_w, sg_ln_w, sg_ln_b, sg_w, sg_b, w_branch_gate, b_branch_gate, w_proj_dn, w_proj_sg, w_out, ffn_pre_norm, ffn_post_norm, w_router, router_bias, w_exp_gate, w_exp_up, w_exp_down, w_sh_gate, w_sh_up, w_sh_down):
    nb, seq, d = x.shape
    depth = ada_w.shape[0]
    t = nb * seq
    tm = _tile(seq, 512)
    tr = _tile(t, 512)
    tsc = _tile(t, 512)
    tcm = _tile(seq, 256)
    nblk = -(-t * TOP_K // MOE_BLOCK) + N_EXPERTS
    row = lambda v: v.reshape(1, -1)
    pad_lanes = lambda v: jnp.pad(v.astype(F32), (0, LANES - v.shape[0])).reshape(1, LANES)

    x2 = x.reshape(t, d)
    for l in range(depth):
        mod3 = _ada(c, ada_w[l], ada_b[l]).reshape(nb, 6, d)

        wi = w_in[l]
        qkvz, ab_cols, uv = wi[:, :4 * DN_WIDTH], wi[:, 4 * DN_WIDTH:4 * DN_WIDTH + 2 * DN_HEADS], wi[:, 4 * DN_WIDTH + 2 * DN_HEADS:]
        w1 = jnp.concatenate([qkvz, ab_cols, jnp.zeros((d, LANES - 2 * DN_HEADS), wi.dtype)], axis=1).astype(BF16)
        qkv, z, ab, ysg = _in_proj(x2, mod3, row(mix_pre_norm[l]), w1, uv.astype(BF16), row(sg_ln_w[l]), row(sg_ln_b[l]),
                                   sg_w[l], sg_b[l].T, seq, tm)

        ydn = _delta_net(qkv.reshape(nb, seq, -1), z.reshape(nb, seq, -1), ab.reshape(nb, seq, -1), conv_w[l],
                         pad_lanes(a_log[l]), pad_lanes(dt_bias[l]), row(dn_norm_w[l]))

        wr_t = w_router[l].T
        wrh = wr_t.astype(BF16)
        wrl = (wr_t - wrh.astype(F32)).astype(BF16)
        x1, hf, sct, shared = _mix(
            x2, ydn.reshape(t, -1), ysg, mod3, row(mix_pre_norm[l]), row(mix_post_norm[l]), row(ffn_pre_norm[l]),
            w_branch_gate[l].astype(BF16), row(b_branch_gate[l]), w_proj_dn[l].astype(BF16), w_proj_sg[l].astype(BF16),
            w_out[l].astype(BF16), wrh, wrl, w_sh_gate[l].astype(BF16), w_sh_up[l].astype(BF16), w_sh_down[l].astype(BF16),
            seq, tm)

        idx, wtt, rank, cnt = _route(sct, router_bias[l].reshape(-1, 1), tr)
        dest, meta = _dest(cnt, idx, rank, tr)
        xs = _scatter(meta[:, 2], meta[:, 3], _dest_tiles(dest, tsc), hf, nblk * MOE_BLOCK, tsc)
        ys = _moe(meta[:, 0], meta[:, 1], xs, w_exp_gate[l], w_exp_up[l], w_exp_down[l])
        x2 = _combine(_dest_tiles(dest, tcm), ys, wtt, x1, shared, mod3, row(ffn_post_norm[l]), seq, tcm)
    return x2.reshape(nb, seq, d)
```

```python
import functools

import jax
import jax.numpy as jnp
from jax import lax
from jax.experimental import pallas as pl
from jax.experimental.pallas import tpu as pltpu

F32 = jnp.float32
BF16 = jnp.bfloat16
I32 = jnp.int32

D_MODEL = 1024
DN_HEADS = 4
DN_HEAD_DIM = 128
DN_WIDTH = DN_HEADS * DN_HEAD_DIM
DN_CHUNK = 64
DN_CHUNKS_PER_STEP = 2
CONV_WIDTH = 4
SG_GROUPS = 4
SG_GROUP_DIM = 128
SG_WIDTH = SG_GROUPS * SG_GROUP_DIM
SG_CHUNK = 128
N_EXPERTS = 256
N_EXPERT_GROUPS = 8
GROUP_SIZE = N_EXPERTS // N_EXPERT_GROUPS
TOPK_GROUPS = 4
TOP_K = 8
EXPERT_HIDDEN = 256
ROUTED_SCALE = 2.5
MOE_BLOCK = 256
EPS = 1e-6

LANES = 128
SUBLANES = 8
ROW_TILE = SUBLANES
VMEM_LIMIT = 56 * 1024 * 1024
NEG_INF = float("-inf")


def _dot(a, b):
    return jnp.dot(a, b, preferred_element_type=F32)


def _dot_nt(a, b):
    return lax.dot_general(a, b, (((1,), (1,)), ((), ())), preferred_element_type=F32)


def _dot_tn(a, b):
    return lax.dot_general(a, b, (((0,), (0,)), ((), ())), preferred_element_type=F32)


def _split2(x):
    hi = x.astype(BF16)
    lo = (x - hi.astype(F32)).astype(BF16)
    return hi, lo


def _split3(x):
    hi = x.astype(BF16)
    r = x - hi.astype(F32)
    mid = r.astype(BF16)
    lo = (r - mid.astype(F32)).astype(BF16)
    return hi, mid, lo


def _dot_hp(a, b, dot=_dot):
    ah, al = _split2(a)
    bh, bl = _split2(b)
    return dot(ah, bh) + (dot(ah, bl) + dot(al, bh))


def _sigmoid(x):
    return 1.0 / (1.0 + jnp.exp(-x))


def _silu(x):
    return x * _sigmoid(x)


def _gelu(x):
    return 0.5 * x * (1.0 + lax.erf(x * (2.0 ** -0.5)))


def _softplus(x):
    return jnp.maximum(x, 0.0) + jnp.log1p(jnp.exp(-jnp.abs(x)))


def _rms(x):
    return x * lax.rsqrt(jnp.mean(x * x, axis=-1, keepdims=True) + EPS)


def _params(*sem):
    return pltpu.CompilerParams(dimension_semantics=sem, vmem_limit_bytes=VMEM_LIMIT)


def _const_spec(shape):
    nd = len(shape)
    return pl.BlockSpec(shape, lambda *_: (0,) * nd)


def _ada_kernel(c_ref, w_ref, b_ref, o_ref):
    cs = _silu(c_ref[...])
    o_ref[...] = _dot_hp(cs, w_ref[...]) + b_ref[...]


def _ada(c, ada_w, ada_b):
    b, d = c.shape
    n = ada_w.shape[1]
    tn = d
    return pl.pallas_call(
        _ada_kernel,
        grid=(n // tn,),
        in_specs=[_const_spec((b, d)), pl.BlockSpec((d, tn), lambda j: (0, j)), pl.BlockSpec((1, tn), lambda j: (0, j))],
        out_specs=pl.BlockSpec((b, tn), lambda j: (0, j)),
        out_shape=jax.ShapeDtypeStruct((b, n), F32),
        compiler_params=_params("arbitrary"),
        name="ada",
    )(c, ada_w, ada_b.reshape(1, n))


W1_COLS = 4 * DN_WIDTH + LANES


def _in_kernel(x_ref, mod_ref, pn_ref, w1_ref, wuv_ref, lnw_ref, lnb_ref, sgw_ref, sgbt_ref,
               qkv_ref, z_ref, ab_ref, ysg_ref):
    tm = x_ref.shape[0]
    mod = mod_ref[0]
    hm = _rms(x_ref[...]) * pn_ref[...] * (1.0 + mod[1:2, :]) + mod[0:1, :]
    hb = hm.astype(BF16)
    p1 = _dot(hb, w1_ref[...])
    qkv_ref[...] = p1[:, :3 * DN_WIDTH]
    z_ref[...] = p1[:, 3 * DN_WIDTH:4 * DN_WIDTH]
    ab_ref[...] = p1[:, 4 * DN_WIDTH:]
    uv = _dot(hb, wuv_ref[...])
    u = _gelu(uv[:, :SG_WIDTH])
    vg = _gelu(uv[:, SG_WIDTH:])
    mu = jnp.mean(vg, axis=-1, keepdims=True)
    dv = vg - mu
    var = jnp.mean(dv * dv, axis=-1, keepdims=True)
    vgn = (dv * lax.rsqrt(var + EPS) * lnw_ref[...] + lnb_ref[...]).astype(BF16)
    row = lax.broadcasted_iota(I32, (SG_CHUNK, SG_CHUNK), 0)
    col = lax.broadcasted_iota(I32, (SG_CHUNK, SG_CHUNK), 1)
    tril = row >= col
    for g in range(SG_GROUPS):
        wg = jnp.where(tril, sgw_ref[g], 0.0).astype(BF16)
        bg = sgbt_ref[:, g:g + 1]
        cs = slice(g * SG_GROUP_DIM, (g + 1) * SG_GROUP_DIM)
        for n in range(tm // SG_CHUNK):
            rs = slice(n * SG_CHUNK, (n + 1) * SG_CHUNK)
            mixed = _dot(wg, vgn[rs, cs]) + bg
            ysg_ref[rs, cs] = (u[rs, cs] * mixed).astype(BF16)


def _in_proj(x2, mod3, pre_norm, w1, wuv, sg_ln_w, sg_ln_b, sg_w, sg_bt, seq, tm):
    t, d = x2.shape
    tiles_per_batch = seq // tm
    tok = lambda i: (i, 0)
    return pl.pallas_call(
        _in_kernel,
        grid=(t // tm,),
        in_specs=[
            pl.BlockSpec((tm, d), tok),
            pl.BlockSpec((1, 6, d), lambda i: (i // tiles_per_batch, 0, 0)),
            _const_spec((1, d)),
            _const_spec(w1.shape),
            _const_spec(wuv.shape),
            _const_spec((1, SG_WIDTH)),
            _const_spec((1, SG_WIDTH)),
            _const_spec(sg_w.shape),
            _const_spec(sg_bt.shape),
        ],
        out_specs=[
            pl.BlockSpec((tm, 3 * DN_WIDTH), tok),
            pl.BlockSpec((tm, DN_WIDTH), tok),
            pl.BlockSpec((tm, LANES), tok),
            pl.BlockSpec((tm, SG_WIDTH), tok),
        ],
        out_shape=[
            jax.ShapeDtypeStruct((t, 3 * DN_WIDTH), F32),
            jax.ShapeDtypeStruct((t, DN_WIDTH), F32),
            jax.ShapeDtypeStruct((t, LANES), F32),
            jax.ShapeDtypeStruct((t, SG_WIDTH), BF16),
        ],
        compiler_params=_params("arbitrary"),
        name="in_proj",
    )(x2, mod3, pre_norm, w1, wuv, sg_ln_w, sg_ln_b, sg_w, sg_bt)


def _unit_lower_inverses(a_list):
    c = a_list[0].shape[0]
    i = lax.broadcasted_iota(I32, (c, c), 0)
    j = lax.broadcasted_iota(I32, (c, c), 1)
    eye = (i == j).astype(F32)
    first = (i == j + 1) & ((i & 1) == 1)
    d_list = [eye - jnp.where(first, a, 0.0) for a in a_list]
    b = 2
    while b < c:
        shift = b.bit_length()
        off = ((i >> shift) == (j >> shift)) & ((i & b) != 0) & ((j & b) == 0)
        a_parts = [jnp.where(off, a, 0.0).astype(BF16) for a in a_list]
        d_parts = [d.astype(BF16) for d in d_list]
        t_list = [_dot(dp, ap) for dp, ap in zip(d_parts, a_parts)]
        d_list = [d - _dot(t.astype(BF16), dp) for d, t, dp in zip(d_list, t_list, d_parts)]
        b *= 2
    return d_list


def _dn_kernel(qkv_ref, z_ref, ab_ref, convw_ref, alog_ref, dtb_ref, nw_ref, y_ref, carry_ref, state_ref):
    nb, rows = qkv_ref.shape[0], qkv_ref.shape[1]
    c = DN_CHUNK
    nc = rows // c
    hd = DN_HEAD_DIM
    heads = [(b, h) for b in range(nb) for h in range(DN_HEADS)]
    probs = [(b, ci, h) for b in range(nb) for ci in range(nc) for h in range(DN_HEADS)]

    @pl.when(pl.program_id(0) == 0)
    def _():
        carry_ref[...] = jnp.zeros_like(carry_ref)
        state_ref[...] = jnp.zeros_like(state_ref)

    ab = ab_ref[...].reshape(nb * rows, LANES)
    g = -jnp.exp(alog_ref[...]) * _softplus(ab + dtb_ref[...])
    beta_all = _sigmoid(ab)
    ri = lax.broadcasted_iota(I32, (nb * rows, nb * rows), 0)
    ci_ = lax.broadcasted_iota(I32, (nb * rows, nb * rows), 1)
    tri = ((ri >= ci_) & ((ri // c) == (ci_ // c))).astype(BF16)
    gh, gm, gl = _split3(g)
    gc = _dot(tri, gh) + (_dot(tri, gm) + _dot(tri, gl))
    gct = gc.T

    i = lax.broadcasted_iota(I32, (c, c), 0)
    j = lax.broadcasted_iota(I32, (c, c), 1)
    causal = i >= j
    strict = i > j
    row8 = lax.broadcasted_iota(I32, (SUBLANES, 3 * DN_WIDTH), 0)

    acts = []
    for b in range(nb):
        xc = qkv_ref[b]
        prev = carry_ref[b]
        acc = xc * convw_ref[CONV_WIDTH - 1:CONV_WIDTH, :]
        for s in range(1, CONV_WIDTH):
            rolled = pltpu.roll(xc, s, axis=0)
            top = jnp.where(row8 < s, pltpu.roll(prev, s, axis=0), rolled[:SUBLANES])
            shifted = jnp.concatenate([top, rolled[SUBLANES:]], axis=0)
            acc = acc + shifted * convw_ref[CONV_WIDTH - 1 - s:CONV_WIDTH - s, :]
        carry_ref[b] = xc[rows - SUBLANES:]
        acts.append(_silu(acc))

    gcs, e_gcs, e_rems, e_lasts = {}, {}, {}, {}
    for b in range(nb):
        for ci in range(nc):
            r0 = b * rows + ci * c
            gcb = gc[r0:r0 + c]
            g_last = gcb[c - 1:c, :]
            gcs[b, ci] = gcb
            e_gcs[b, ci] = jnp.exp(gcb)
            e_rems[b, ci] = jnp.exp(g_last - gcb)
            e_lasts[b, ci] = jnp.exp(g_last)

    qn, kn, knb, kb, vb, decay = {}, {}, {}, {}, {}, {}
    for p in probs:
        b, ci, h = p
        r0 = b * rows + ci * c
        act = acts[b][ci * c:(ci + 1) * c]
        q = act[:, h * hd:(h + 1) * hd]
        k = act[:, DN_WIDTH + h * hd:DN_WIDTH + (h + 1) * hd]
        v = act[:, 2 * DN_WIDTH + h * hd:2 * DN_WIDTH + (h + 1) * hd]
        qn[p] = q * lax.rsqrt(jnp.sum(q * q, axis=-1, keepdims=True) + EPS) * (hd ** -0.5)
        kn[p] = k * lax.rsqrt(jnp.sum(k * k, axis=-1, keepdims=True) + EPS)
        beta = beta_all[r0:r0 + c, DN_HEADS + h:DN_HEADS + h + 1]
        diff = gcs[b, ci][:, h:h + 1] - gct[h:h + 1, r0:r0 + c]
        decay[p] = jnp.where(causal, jnp.exp(jnp.where(causal, diff, 0.0)), 0.0)
        kb[p] = kn[p] * beta
        vb[p] = v * beta
        knb[p] = kn[p].astype(BF16)

    kk = {p: _dot_nt(kb[p].astype(BF16), knb[p]) for p in probs}
    qk = {p: _dot_nt(qn[p].astype(BF16), knb[p]) for p in probs}
    tinv = dict(zip(probs, _unit_lower_inverses([jnp.where(strict, kk[p] * decay[p], 0.0) for p in probs])))
    rhs = {p: jnp.concatenate([vb[p], kb[p] * e_gcs[p[0], p[1]][:, p[2]:p[2] + 1]], axis=1) for p in probs}
    sol = {p: _dot(tinv[p].astype(BF16), rhs[p].astype(BF16)) for p in probs}
    lhs = {p: jnp.concatenate([sol[p][:, hd:], qn[p] * e_gcs[p[0], p[1]][:, p[2]:p[2] + 1]], axis=0).astype(BF16)
           for p in probs}
    qkm = {p: jnp.where(causal, qk[p] * decay[p], 0.0).astype(BF16) for p in probs}
    kd = {p: (kn[p] * e_rems[p[0], p[1]][:, p[2]:p[2] + 1]).astype(BF16) for p in probs}

    state = {bh: state_ref[bh[0] * DN_HEADS + bh[1]] for bh in heads}
    for ci in range(nc):
        cur = [(b, ci, h) for (b, h) in heads]
        ws = {p: _dot(lhs[p], state[p[0], p[2]].astype(BF16)) for p in cur}
        vnb = {p: (sol[p][:, :hd] - ws[p][:c]).astype(BF16) for p in cur}
        o = {p: ws[p][c:] + _dot(qkm[p], vnb[p]) for p in cur}
        upd = {p: _dot_tn(kd[p], vnb[p]) for p in cur}
        for p in cur:
            b, _, h = p
            state[b, h] = state[b, h] * e_lasts[b, ci][:, h:h + 1] + upd[p]
            zh = z_ref[b, ci * c:(ci + 1) * c, h * hd:(h + 1) * hd]
            y_ref[b, ci * c:(ci + 1) * c, h * hd:(h + 1) * hd] = (_rms(o[p]) * nw_ref[...] * _silu(zh)).astype(BF16)
    for bh in heads:
        state_ref[bh[0] * DN_HEADS + bh[1]] = state[bh]


def _delta_net(qkv3, z3, ab3, conv_w, alog_row, dtb_row, dn_norm_w):
    nb, s, _ = qkv3.shape
    c = DN_CHUNK * DN_CHUNKS_PER_STEP
    blk = lambda n: (0, n, 0)
    return pl.pallas_call(
        _dn_kernel,
        grid=(s // c,),
        in_specs=[
            pl.BlockSpec((nb, c, 3 * DN_WIDTH), blk),
            pl.BlockSpec((nb, c, DN_WIDTH), blk),
            pl.BlockSpec((nb, c, LANES), blk),
            _const_spec(conv_w.shape),
            _const_spec((1, LANES)),
            _const_spec((1, LANES)),
            _const_spec((1, DN_HEAD_DIM)),
        ],
        out_specs=pl.BlockSpec((nb, c, DN_WIDTH), blk),
        out_shape=jax.ShapeDtypeStruct((nb, s, DN_WIDTH), BF16),
        scratch_shapes=[
            pltpu.VMEM((nb, SUBLANES, 3 * DN_WIDTH), F32),
            pltpu.VMEM((nb * DN_HEADS, DN_HEAD_DIM, DN_HEAD_DIM), F32),
        ],
        compiler_params=_params("arbitrary"),
        name="delta_net",
    )(qkv3, z3, ab3, conv_w, alog_row, dtb_row, dn_norm_w)


def _mix_kernel(x_ref, ydn_ref, ysg_ref, mod_ref, pn_ref, postn_ref, fpn_ref,
                wbg_ref, bbg_ref, wpd_ref, wps_ref, wout_ref, wrh_ref, wrl_ref,
                x1_ref, hf_ref, sct_ref):
    d = x_ref.shape[1]
    x = x_ref[...]
    mod = mod_ref[0]
    hm = _rms(x) * pn_ref[...] * (1.0 + mod[1:2, :]) + mod[0:1, :]
    gates = _sigmoid(_dot(hm.astype(BF16), wbg_ref[...]) + bbg_ref[...])
    merged = gates[:, :d] * _dot(ydn_ref[...], wpd_ref[...]) + gates[:, d:] * _dot(ysg_ref[...], wps_ref[...])
    y = _dot(merged.astype(BF16), wout_ref[...])
    x1 = x + mod[2:3, :] * (_rms(y) * postn_ref[...])
    x1_ref[...] = x1
    hf = _rms(x1) * fpn_ref[...] * (1.0 + mod[4:5, :]) + mod[3:4, :]
    for j in range(ROW_TILE):
        hf_ref[pl.ds(j, hf.shape[0], stride=ROW_TILE), :] = hf[:, j * LANES:(j + 1) * LANES]
    hh, hl = _split2(hf)
    logits_t = _dot_nt(wrh_ref[...], hh) + (_dot_nt(wrh_ref[...], hl) + _dot_nt(wrl_ref[...], hh))
    sct_ref[...] = _sigmoid(logits_t)


def _mix(x2, ydn, ysg, mod3, pre_norm, post_norm, ffn_pre_norm, wbg, bbg, wpd, wps, wout, wrh, wrl, seq, tm):
    t, d = x2.shape
    tiles_per_batch = seq // tm
    tok = lambda i: (i, 0)
    consts = [pre_norm, post_norm, ffn_pre_norm, wbg, bbg, wpd, wps, wout, wrh, wrl]
    return pl.pallas_call(
        _mix_kernel,
        grid=(t // tm,),
        in_specs=[
            pl.BlockSpec((tm, d), tok),
            pl.BlockSpec((tm, DN_WIDTH), tok),
            pl.BlockSpec((tm, SG_WIDTH), tok),
            pl.BlockSpec((1, 6, d), lambda i: (i // tiles_per_batch, 0, 0)),
        ] + [_const_spec(a.shape) for a in consts],
        out_specs=[
            pl.BlockSpec((tm, d), tok),
            pl.BlockSpec((tm * ROW_TILE, LANES), tok),
            pl.BlockSpec((N_EXPERTS, tm), lambda i: (0, i)),
        ],
        out_shape=[
            jax.ShapeDtypeStruct((t, d), F32),
            jax.ShapeDtypeStruct((t * ROW_TILE, LANES), F32),
            jax.ShapeDtypeStruct((N_EXPERTS, t), F32),
        ],
        compiler_params=_params("arbitrary"),
        name="mix",
    )(x2, ydn, ysg, mod3, *consts)


def _first_argmax(vals, idx):
    m = jnp.max(vals, axis=0, keepdims=True)
    first = jnp.min(jnp.where(vals == m, idx, jnp.int32(2 ** 30)), axis=0, keepdims=True)
    return m, first


def _route_kernel(sct_ref, bias_ref, idx_ref, wtt_ref, rank_ref, cnt_ref, carry_ref):
    tm = sct_ref.shape[1]

    @pl.when(pl.program_id(0) == 0)
    def _():
        carry_ref[...] = jnp.zeros_like(carry_ref)

    scores = sct_ref[...]
    sel = scores + bias_ref[...]
    erow = lax.broadcasted_iota(I32, (N_EXPERTS, tm), 0)
    grow = lax.broadcasted_iota(I32, (GROUP_SIZE, tm), 0)

    gs = []
    for gidx in range(N_EXPERT_GROUPS):
        sg = sel[gidx * GROUP_SIZE:(gidx + 1) * GROUP_SIZE]
        m1, first = _first_argmax(sg, grow)
        m2 = jnp.max(jnp.where(grow == first, NEG_INF, sg), axis=0, keepdims=True)
        gs.append(m1 + m2)
    gsc = jnp.concatenate(gs, axis=0)
    giota = lax.broadcasted_iota(I32, (N_EXPERT_GROUPS, tm), 0)
    gmask = jnp.zeros((N_EXPERT_GROUPS, tm), F32)
    cur = gsc
    for _ in range(TOPK_GROUPS):
        _, gi = _first_argmax(cur, giota)
        pick = giota == gi
        gmask = jnp.where(pick, 1.0, gmask)
        cur = jnp.where(pick, NEG_INF, cur)
    masked = jnp.concatenate(
        [jnp.where(gmask[gidx:gidx + 1, :] > 0.5, sel[gidx * GROUP_SIZE:(gidx + 1) * GROUP_SIZE], NEG_INF)
         for gidx in range(N_EXPERT_GROUPS)], axis=0)

    cur = masked
    idxs, wts = [], []
    onehot = jnp.zeros((N_EXPERTS, tm), F32)
    for _ in range(TOP_K):
        _, ei = _first_argmax(cur, erow)
        pick = erow == ei
        idxs.append(ei)
        wts.append(jnp.sum(jnp.where(pick, scores, 0.0), axis=0, keepdims=True))
        onehot = jnp.where(pick, 1.0, onehot)
        cur = jnp.where(pick, NEG_INF, cur)
    idx = jnp.concatenate(idxs, axis=0)
    wt = jnp.concatenate(wts, axis=0)
    wt = wt / jnp.sum(wt, axis=0, keepdims=True) * ROUTED_SCALE
    idx_ref[...] = idx
    wpad = jnp.concatenate([wt, jnp.zeros((LANES - TOP_K, tm), F32)], axis=0)
    wtt_ref[...] = wpad.T

    ti = lax.broadcasted_iota(I32, (tm, tm), 0)
    tj = lax.broadcasted_iota(I32, (tm, tm), 1)
    upper = (ti < tj).astype(BF16)
    before = _dot(onehot.astype(BF16), upper) + carry_ref[...]
    rank_ref[...] = jnp.concatenate(
        [jnp.sum(jnp.where(erow == idxs[kk], before, 0.0), axis=0, keepdims=True) for kk in range(TOP_K)],
        axis=0).astype(I32)
    total = carry_ref[...] + jnp.sum(onehot, axis=1, keepdims=True)
    carry_ref[...] = total
    cnt_ref[...] = total


def _route(sct, bias_col, tm):
    e, t = sct.shape
    tile = lambda i: (0, i)
    return pl.pallas_call(
        _route_kernel,
        grid=(t // tm,),
        in_specs=[pl.BlockSpec((e, tm), tile), _const_spec((e, 1))],
        out_specs=[
            pl.BlockSpec((TOP_K, tm), tile),
            pl.BlockSpec((tm, LANES), lambda i: (i, 0)),
            pl.BlockSpec((TOP_K, tm), tile),
            _const_spec((e, 1)),
        ],
        out_shape=[
            jax.ShapeDtypeStruct((TOP_K, t), I32),
            jax.ShapeDtypeStruct((t, LANES), F32),
            jax.ShapeDtypeStruct((TOP_K, t), I32),
            jax.ShapeDtypeStruct((e, 1), F32),
        ],
        scratch_shapes=[pltpu.VMEM((e, 1), F32)],
        compiler_params=_params("arbitrary"),
        name="route",
    )(sct, bias_col)


def _dest_kernel(cnt_ref, idx_ref, rank_ref, dest_ref, meta_ref):
    tm = idx_ref.shape[1]
    e = N_EXPERTS
    cnt = cnt_ref[...]
    padded = jnp.floor((cnt + (MOE_BLOCK - 1)) * (1.0 / MOE_BLOCK)) * MOE_BLOCK
    pw = jnp.broadcast_to(padded, (e, LANES))
    ri = lax.broadcasted_iota(I32, (e, e), 0)
    ci = lax.broadcasted_iota(I32, (e, e), 1)
    lower = (ri >= ci).astype(BF16)
    ph, pm, plo = _split3(pw)
    pends = _dot(lower, ph) + (_dot(lower, pm) + _dot(lower, plo))
    pstart = pends - pw
    erow = lax.broadcasted_iota(I32, (e, tm), 0)
    idx = idx_ref[...]
    rows = [jnp.sum(jnp.where(erow == idx[kk:kk + 1, :], pstart[:, 0:1], 0.0), axis=0, keepdims=True)
            for kk in range(TOP_K)]
    dest_ref[...] = jnp.concatenate(rows, axis=0).astype(I32) + rank_ref[...]
    lane = lax.broadcasted_iota(I32, (e, LANES), 1)
    blocks = jnp.where(lane == 0, pstart, pw) * (1.0 / MOE_BLOCK)
    pads = jnp.where(lane == 2, pstart + cnt, pw - cnt)
    meta_ref[...] = jnp.where(lane < 2, blocks, pads).astype(I32)


def _dest(cnt, idx, rank, tm):
    e = cnt.shape[0]
    t = idx.shape[1]
    tile = lambda i: (0, i)
    return pl.pallas_call(
        _dest_kernel,
        grid=(t // tm,),
        in_specs=[_const_spec((e, 1)), pl.BlockSpec((TOP_K, tm), tile), pl.BlockSpec((TOP_K, tm), tile)],
        out_specs=[pl.BlockSpec((TOP_K, tm), tile), _const_spec((e, LANES))],
        out_shape=[jax.ShapeDtypeStruct((TOP_K, t), I32), jax.ShapeDtypeStruct((e, LANES), I32)],
        compiler_params=_params("arbitrary"),
        name="dest",
    )(cnt, idx, rank)


def _row_copy(src, dst, sem):
    return pltpu.make_async_copy(src, dst, sem)


def _rows(first, count=1):
    return pl.ds(pl.multiple_of(first * ROW_TILE, ROW_TILE), count * ROW_TILE)


PAD_PIECES = tuple(MOE_BLOCK >> s for s in range(1, MOE_BLOCK.bit_length()))


def _zero_unassigned_rows(pad_start_ref, pad_rows_ref, xs_out, zero_ref, sem):
    zero_ref[...] = jnp.zeros_like(zero_ref)
    ne = pad_start_ref.shape[0]
    total = xs_out.shape[0] // (MOE_BLOCK * ROW_TILE)

    def piece(rows, pos):
        return pltpu.make_async_copy(zero_ref.at[_rows(0, rows)], xs_out.at[_rows(pos, rows)], sem)

    def pads(wait):
        def per_expert(ex, carry):
            pos = pad_start_ref[ex]
            pad = pad_rows_ref[ex]
            for rows in PAD_PIECES:
                @pl.when((pad & rows) != 0)
                def _():
                    piece(rows, pos).wait() if wait else piece(rows, pos).start()
                pos = pos + (pad & rows)
            return carry
        lax.fori_loop(0, ne, per_expert, 0)

    def tail(wait):
        used = (pad_start_ref[ne - 1] + pad_rows_ref[ne - 1]) // MOE_BLOCK

        def per_block(blk, carry):
            cp = piece(MOE_BLOCK, blk * MOE_BLOCK)
            cp.wait() if wait else cp.start()
            return carry
        lax.fori_loop(used, total, per_block, 0)

    pads(False)
    tail(False)
    pads(True)
    tail(True)


def _scatter_kernel(pad_start_ref, pad_rows_ref, dest_hbm, hf_ref, wsg_ref, wsu_ref, wsd_ref, xs_out, sh_ref,
                    dest_smem, zero_ref, sem_idx, sem_rows, sem_zero):
    tm = hf_ref.shape[0] // ROW_TILE
    i = pl.program_id(0)

    @pl.when(i == 0)
    def _():
        _zero_unassigned_rows(pad_start_ref, pad_rows_ref, xs_out, zero_ref, sem_zero)

    idx_copy = pltpu.make_async_copy(dest_hbm.at[i], dest_smem, sem_idx)
    idx_copy.start()
    idx_copy.wait()

    def issue(tok, carry):
        for kk in range(TOP_K):
            slot = dest_smem[kk * tm + tok]
            _row_copy(hf_ref.at[_rows(tok)], xs_out.at[_rows(slot)], sem_rows).start(priority=kk % 2)
        return carry

    lax.fori_loop(0, tm, issue, 0)

    hb = jnp.concatenate([hf_ref[pl.ds(j, tm, stride=ROW_TILE), :] for j in range(ROW_TILE)], axis=1).astype(BF16)
    hid = _silu(_dot(hb, wsg_ref[...])) * _dot(hb, wsu_ref[...])
    sh_ref[...] = _dot(hid.astype(BF16), wsd_ref[...])

    for kk in range(TOP_K):
        _row_copy(hf_ref, xs_out.at[_rows(0, tm)], sem_rows).wait()


def _scatter(pad_start, pad_rows, dest_tiles, hf, wsg, wsu, wsd, n_slots, tm):
    t = hf.shape[0] // ROW_TILE
    d = wsg.shape[0]
    const = lambda a: pl.BlockSpec(a.shape, lambda i, ps, pr: (0, 0))
    n_tiles = t // tm
    return pl.pallas_call(
        _scatter_kernel,
        grid_spec=pltpu.PrefetchScalarGridSpec(
            num_scalar_prefetch=2,
            grid=(n_tiles,),
            in_specs=[pl.BlockSpec(memory_space=pl.ANY), pl.BlockSpec((tm * ROW_TILE, LANES), lambda i, ps, pr: (i, 0)),
                      const(wsg), const(wsu), const(wsd)],
            out_specs=[pl.BlockSpec(memory_space=pl.ANY), pl.BlockSpec((tm, d), lambda i, ps, pr: (i, 0))],
            scratch_shapes=[
                pltpu.SMEM((TOP_K * tm,), I32),
                pltpu.VMEM((MOE_BLOCK * ROW_TILE, LANES), hf.dtype),
                pltpu.SemaphoreType.DMA,
                pltpu.SemaphoreType.DMA,
                pltpu.SemaphoreType.DMA,
            ],
        ),
        out_shape=[jax.ShapeDtypeStruct((n_slots * ROW_TILE, LANES), hf.dtype), jax.ShapeDtypeStruct((t, d), F32)],
        compiler_params=_params("arbitrary"),
        name="scatter",
    )(pad_start, pad_rows, dest_tiles, hf, wsg, wsu, wsd)


MOE_BUFFERS = 6


def _moe_kernel(bstart_ref, nblk_ref, xs_hbm, wg_ref, wu_ref, wd_ref, ys_hbm,
                xbuf, ybuf, wgb_ref, wub_ref, wdb_ref, sem_in, sem_out):
    ex = pl.program_id(0)
    last = pl.num_programs(0) - 1
    n = nblk_ref[ex]
    b0 = bstart_ref[ex]
    used = bstart_ref[last] + nblk_ref[last]

    def in_copy(g):
        slot = lax.rem(g, MOE_BUFFERS)
        return pltpu.make_async_copy(xs_hbm.at[_rows(g * MOE_BLOCK, MOE_BLOCK)], xbuf.at[slot], sem_in.at[slot])

    def out_copy(g):
        slot = lax.rem(g, MOE_BUFFERS)
        return pltpu.make_async_copy(ybuf.at[slot], ys_hbm.at[_rows(g * MOE_BLOCK, MOE_BLOCK)], sem_out.at[slot])

    @pl.when(ex == 0)
    def _():
        for s in range(MOE_BUFFERS - 1):
            @pl.when(s < used)
            def _():
                in_copy(s).start()

    wgb_ref[...] = wg_ref[0].astype(BF16)
    wub_ref[...] = wu_ref[0].astype(BF16)
    wdb_ref[...] = wd_ref[0].astype(BF16)

    def body(g, carry):
        slot = lax.rem(g, MOE_BUFFERS)
        in_copy(g).wait()

        @pl.when(g + (MOE_BUFFERS - 1) < used)
        def _():
            in_copy(g + (MOE_BUFFERS - 1)).start()

        @pl.when(g >= MOE_BUFFERS)
        def _():
            out_copy(g - MOE_BUFFERS).wait()

        xb = jnp.concatenate([xbuf[slot, pl.ds(j, MOE_BLOCK, stride=ROW_TILE), :] for j in range(ROW_TILE)],
                             axis=1).astype(BF16)
        hid = _silu(_dot(xb, wgb_ref[...])) * _dot(xb, wub_ref[...])
        y = _dot(hid.astype(BF16), wdb_ref[...])
        for j in range(ROW_TILE):
            ybuf[slot, pl.ds(j, MOE_BLOCK, stride=ROW_TILE), :] = y[:, j * LANES:(j + 1) * LANES]
        out_copy(g).start()
        return carry

    lax.fori_loop(b0, b0 + n, body, 0)

    @pl.when(ex == last)
    def _():
        for s in range(MOE_BUFFERS):
            @pl.when(s < used)
            def _():
                out_copy(used - 1 - s).wait()

        total = ys_hbm.shape[0] // (MOE_BLOCK * ROW_TILE)
        ybuf[0] = jnp.zeros(ybuf.shape[1:], ybuf.dtype)

        def tail_copy(blk):
            return pltpu.make_async_copy(ybuf.at[0], ys_hbm.at[_rows(blk * MOE_BLOCK, MOE_BLOCK)], sem_out.at[0])

        def start_tail(blk, carry):
            tail_copy(blk).start()
            return carry

        def wait_tail(blk, carry):
            tail_copy(blk).wait()
            return carry

        lax.fori_loop(used, total, start_tail, 0)
        lax.fori_loop(used, total, wait_tail, 0)


def _moe(bstart, nblk_e, xs, w_gate, w_up, w_down):
    ne, d, eh = w_gate.shape
    wspec = lambda shape: pl.BlockSpec(shape, lambda ex, bs, nb: (ex, 0, 0))
    return pl.pallas_call(
        _moe_kernel,
        grid_spec=pltpu.PrefetchScalarGridSpec(
            num_scalar_prefetch=2,
            grid=(ne,),
            in_specs=[pl.BlockSpec(memory_space=pl.ANY), wspec((1, d, eh)), wspec((1, d, eh)), wspec((1, eh, d))],
            out_specs=pl.BlockSpec(memory_space=pl.ANY),
            scratch_shapes=[
                pltpu.VMEM((MOE_BUFFERS, MOE_BLOCK * ROW_TILE, LANES), F32),
                pltpu.VMEM((MOE_BUFFERS, MOE_BLOCK * ROW_TILE, LANES), F32),
                pltpu.VMEM((d, eh), BF16), pltpu.VMEM((d, eh), BF16), pltpu.VMEM((eh, d), BF16),
                pltpu.SemaphoreType.DMA((MOE_BUFFERS,)),
                pltpu.SemaphoreType.DMA((MOE_BUFFERS,)),
            ],
        ),
        out_shape=jax.ShapeDtypeStruct(xs.shape, F32),
        compiler_params=_params("arbitrary"),
        name="moe",
    )(bstart, nblk_e, xs, w_gate, w_up, w_down)


def _combine_kernel(dest_hbm, ys_hbm, wtt_ref, x1_ref, sh_ref, mod_ref, postn_ref, o_ref, dest_smem0, dest_smem1, buf_ref, sem_idx, sem_rows):
    tc = x1_ref.shape[0]
    i = pl.program_id(0)
    n = pl.num_programs(0)
    dest_smem = (dest_smem0, dest_smem1)

    def idx_copy(tile, slot):
        return pltpu.make_async_copy(dest_hbm.at[tile], dest_smem[slot], sem_idx.at[slot])

    def issue_rows(slot):
        def issue(tok, carry):
            for kk in range(TOP_K):
                row = dest_smem[slot][kk * tc + tok]
                _row_copy(ys_hbm.at[_rows(row)], buf_ref.at[slot, kk, _rows(tok)],
                          sem_rows.at[slot]).start(priority=kk % 2)
            return carry
        lax.fori_loop(0, tc, issue, 0)

    @pl.when(i == 0)
    def _():
        idx_copy(0, 0).start()
        idx_copy(0, 0).wait()

        @pl.when(n > 1)
        def _():
            idx_copy(1, 1).start()
        issue_rows(0)

    def step(cur):
        nxt = 1 - cur

        @pl.when(i + 1 < n)
        def _():
            idx_copy(i + 1, nxt).wait()

            @pl.when(i + 2 < n)
            def _():
                idx_copy(i + 2, cur).start()
            issue_rows(nxt)

        for kk in range(TOP_K):
            _row_copy(ys_hbm.at[_rows(0, tc)], buf_ref.at[cur, kk], sem_rows.at[cur]).wait()

        wtt = wtt_ref[...]
        wk = [jnp.broadcast_to(wtt[:, kk:kk + 1], (tc, LANES)) for kk in range(TOP_K)]
        cols = []
        for j in range(ROW_TILE):
            lane_group = pl.ds(j, tc, stride=ROW_TILE)
            acc = buf_ref[cur, 0, lane_group, :] * wk[0]
            for kk in range(1, TOP_K):
                acc = acc + buf_ref[cur, kk, lane_group, :] * wk[kk]
            cols.append(acc)
        y = jnp.concatenate(cols, axis=1) + sh_ref[...]
        mod = mod_ref[0]
        o_ref[...] = x1_ref[...] + mod[5:6, :] * (_rms(y) * postn_ref[...])

    parity = lax.rem(i, 2)
    for cur in range(2):
        @pl.when(parity == cur)
        def _():
            step(cur)


def _combine(dest_tiles, ys, wtt, x1, shared, mod3, ffn_post_norm, seq, tc):
    t, d = x1.shape
    tiles_per_batch = seq // tc
    tok = lambda i: (i, 0)
    return pl.pallas_call(
        _combine_kernel,
        grid=(t // tc,),
        in_specs=[
            pl.BlockSpec(memory_space=pl.ANY),
            pl.BlockSpec(memory_space=pl.ANY),
            pl.BlockSpec((tc, LANES), tok),
            pl.BlockSpec((tc, d), tok),
            pl.BlockSpec((tc, d), tok),
            pl.BlockSpec((1, 6, d), lambda i: (i // tiles_per_batch, 0, 0)),
            _const_spec((1, d)),
        ],
        out_specs=pl.BlockSpec((tc, d), tok),
        out_shape=jax.ShapeDtypeStruct((t, d), F32),
        scratch_shapes=[
            pltpu.SMEM((TOP_K * tc,), I32),
            pltpu.SMEM((TOP_K * tc,), I32),
            pltpu.VMEM((2, TOP_K, tc * ROW_TILE, LANES), F32),
            pltpu.SemaphoreType.DMA((2,)),
            pltpu.SemaphoreType.DMA((2,)),
        ],
        compiler_params=_params("arbitrary"),
        name="combine",
    )(dest_tiles, ys, wtt, x1, shared, mod3, ffn_post_norm)


def _tile(n, want):
    t = min(n, want)
    assert n % t == 0
    return t


def _dest_tiles(dest, tile):
    k, t = dest.shape
    return dest.reshape(k, t // tile, tile).transpose(1, 0, 2).reshape(t // tile, k * tile)


def kernel(x, c, ada_w, ada_b, mix_pre_norm, mix_post_norm, w_in, conv_w, a_log, dt_bias, dn_norm_w, sg_ln_w, sg_ln_b, sg_w, sg_b, w_branch_gate, b_branch_gate, w_proj_dn, w_proj_sg, w_out, ffn_pre_norm, ffn_post_norm, w_router, router_bias, w_exp_gate, w_exp_up, w_exp_down, w_sh_gate, w_sh_up, w_sh_down):
    nb, seq, d = x.shape
    depth = ada_w.shape[0]
    t = nb * seq
    tm = _tile(seq, 512)
    tr = _tile(t, 512)
    tsc = _tile(t, 512)
    tcm = _tile(seq, 256)
    nblk = -(-t * TOP_K // MOE_BLOCK) + N_EXPERTS
    row = lambda v: v.reshape(1, -1)
    pad_lanes = lambda v: jnp.pad(v.astype(F32), (0, LANES - v.shape[0])).reshape(1, LANES)

    x2 = x.reshape(t, d)
    for l in range(depth):
        mod3 = _ada(c, ada_w[l], ada_b[l]).reshape(nb, 6, d)

        wi = w_in[l]
        qkvz, ab_cols, uv = wi[:, :4 * DN_WIDTH], wi[:, 4 * DN_WIDTH:4 * DN_WIDTH + 2 * DN_HEADS], wi[:, 4 * DN_WIDTH + 2 * DN_HEADS:]
        w1 = jnp.concatenate([qkvz, ab_cols, jnp.zeros((d, LANES - 2 * DN_HEADS), wi.dtype)], axis=1).astype(BF16)
        qkv, z, ab, ysg = _in_proj(x2, mod3, row(mix_pre_norm[l]), w1, uv.astype(BF16), row(sg_ln_w[l]), row(sg_ln_b[l]),
                                   sg_w[l], sg_b[l].T, seq, tm)

        ydn = _delta_net(qkv.reshape(nb, seq, -1), z.reshape(nb, seq, -1), ab.reshape(nb, seq, -1), conv_w[l],
                         pad_lanes(a_log[l]), pad_lanes(dt_bias[l]), row(dn_norm_w[l]))

        wr_t = w_router[l].T
        wrh = wr_t.astype(BF16)
        wrl = (wr_t - wrh.astype(F32)).astype(BF16)
        x1, hf, sct = _mix(
            x2, ydn.reshape(t, -1), ysg, mod3, row(mix_pre_norm[l]), row(mix_post_norm[l]), row(ffn_pre_norm[l]),
            w_branch_gate[l].astype(BF16), row(b_branch_gate[l]), w_proj_dn[l].astype(BF16), w_proj_sg[l].astype(BF16),
            w_out[l].astype(BF16), wrh, wrl, seq, tm)

        idx, wtt, rank, cnt = _route(sct, router_bias[l].reshape(-1, 1), tr)
        dest, meta = _dest(cnt, idx, rank, tr)
        xs, shared = _scatter(meta[:, 2], meta[:, 3], _dest_tiles(dest, tsc), hf, w_sh_gate[l].astype(BF16),
                              w_sh_up[l].astype(BF16), w_sh_down[l].astype(BF16), nblk * MOE_BLOCK, tsc)
        ys = _moe(meta[:, 0], meta[:, 1], xs, w_exp_gate[l], w_exp_up[l], w_exp_down[l])
        x2 = _combine(_dest_tiles(dest, tcm), ys, wtt, x1, shared, mod3, row(ffn_post_norm[l]), seq, tcm)
    return x2.reshape(nb, seq, d)
```

```python
import functools

import jax
import jax.numpy as jnp
from jax import lax
from jax.experimental import pallas as pl
from jax.experimental.pallas import tpu as pltpu

F32 = jnp.float32
BF16 = jnp.bfloat16
I32 = jnp.int32

D_MODEL = 1024
DN_HEADS = 4
DN_HEAD_DIM = 128
DN_WIDTH = DN_HEADS * DN_HEAD_DIM
DN_CHUNK = 64
DN_CHUNKS_PER_STEP = 4
CONV_WIDTH = 4
SG_GROUPS = 4
SG_GROUP_DIM = 128
SG_WIDTH = SG_GROUPS * SG_GROUP_DIM
SG_CHUNK = 128
N_EXPERTS = 256
N_EXPERT_GROUPS = 8
GROUP_SIZE = N_EXPERTS // N_EXPERT_GROUPS
TOPK_GROUPS = 4
TOP_K = 8
EXPERT_HIDDEN = 256
ROUTED_SCALE = 2.5
MOE_BLOCK = 256
EPS = 1e-6

LANES = 128
SUBLANES = 8
ROW_TILE = SUBLANES
VMEM_LIMIT = 56 * 1024 * 1024
NEG_INF = float("-inf")


def _dot(a, b):
    return jnp.dot(a, b, preferred_element_type=F32)


def _dot_nt(a, b):
    return lax.dot_general(a, b, (((1,), (1,)), ((), ())), preferred_element_type=F32)


def _dot_tn(a, b):
    return lax.dot_general(a, b, (((0,), (0,)), ((), ())), preferred_element_type=F32)


def _split2(x):
    hi = x.astype(BF16)
    lo = (x - hi.astype(F32)).astype(BF16)
    return hi, lo


def _split3(x):
    hi = x.astype(BF16)
    r = x - hi.astype(F32)
    mid = r.astype(BF16)
    lo = (r - mid.astype(F32)).astype(BF16)
    return hi, mid, lo


def _dot_hp(a, b, dot=_dot):
    ah, al = _split2(a)
    bh, bl = _split2(b)
    return dot(ah, bh) + (dot(ah, bl) + dot(al, bh))


def _sigmoid(x):
    return 1.0 / (1.0 + jnp.exp(-x))


def _silu(x):
    return x * _sigmoid(x)


def _gelu(x):
    return 0.5 * x * (1.0 + lax.erf(x * (2.0 ** -0.5)))


def _softplus(x):
    return jnp.maximum(x, 0.0) + jnp.log1p(jnp.exp(-jnp.abs(x)))


def _rms(x):
    return x * lax.rsqrt(jnp.mean(x * x, axis=-1, keepdims=True) + EPS)


def _params(*sem):
    return pltpu.CompilerParams(dimension_semantics=sem, vmem_limit_bytes=VMEM_LIMIT)


def _const_spec(shape):
    nd = len(shape)
    return pl.BlockSpec(shape, lambda *_: (0,) * nd)


def _weight_spec(shape):
    nd = len(shape)
    return pl.BlockSpec(shape, lambda *_: (0,) * nd, pipeline_mode=pl.Buffered(1))


def _ada_kernel(c_ref, w_ref, b_ref, o_ref):
    cs = _silu(c_ref[...])
    o_ref[...] = _dot_hp(cs, w_ref[...]) + b_ref[...]


def _ada(c, ada_w, ada_b):
    b, d = c.shape
    n = ada_w.shape[1]
    tn = d
    return pl.pallas_call(
        _ada_kernel,
        grid=(n // tn,),
        in_specs=[_const_spec((b, d)), pl.BlockSpec((d, tn), lambda j: (0, j)), pl.BlockSpec((1, tn), lambda j: (0, j))],
        out_specs=pl.BlockSpec((b, tn), lambda j: (0, j)),
        out_shape=jax.ShapeDtypeStruct((b, n), F32),
        compiler_params=_params("arbitrary"),
        name="ada",
    )(c, ada_w, ada_b.reshape(1, n))


W1_COLS = 4 * DN_WIDTH + LANES


def _in_kernel(x_ref, mod_ref, pn_ref, w1_ref, wuv_ref, lnw_ref, lnb_ref, sgw_ref, sgbt_ref,
               qkv_ref, z_ref, ab_ref, ysg_ref):
    tm = x_ref.shape[0]
    mod = mod_ref[0]
    hm = _rms(x_ref[...]) * pn_ref[...] * (1.0 + mod[1:2, :]) + mod[0:1, :]
    hb = hm.astype(BF16)
    p1 = _dot(hb, w1_ref[...])
    qkv_ref[...] = p1[:, :3 * DN_WIDTH]
    z_ref[...] = p1[:, 3 * DN_WIDTH:4 * DN_WIDTH]
    ab_ref[...] = p1[:, 4 * DN_WIDTH:]
    uv = _dot(hb, wuv_ref[...])
    u = _gelu(uv[:, :SG_WIDTH])
    vg = _gelu(uv[:, SG_WIDTH:])
    mu = jnp.mean(vg, axis=-1, keepdims=True)
    dv = vg - mu
    var = jnp.mean(dv * dv, axis=-1, keepdims=True)
    vgn = (dv * lax.rsqrt(var + EPS) * lnw_ref[...] + lnb_ref[...]).astype(BF16)
    row = lax.broadcasted_iota(I32, (SG_CHUNK, SG_CHUNK), 0)
    col = lax.broadcasted_iota(I32, (SG_CHUNK, SG_CHUNK), 1)
    tril = row >= col
    for g in range(SG_GROUPS):
        wg = jnp.where(tril, sgw_ref[g], 0.0).astype(BF16)
        bg = sgbt_ref[:, g:g + 1]
        cs = slice(g * SG_GROUP_DIM, (g + 1) * SG_GROUP_DIM)
        for n in range(tm // SG_CHUNK):
            rs = slice(n * SG_CHUNK, (n + 1) * SG_CHUNK)
            mixed = _dot(wg, vgn[rs, cs]) + bg
            ysg_ref[rs, cs] = (u[rs, cs] * mixed).astype(BF16)


def _in_proj(x2, mod3, pre_norm, w1, wuv, sg_ln_w, sg_ln_b, sg_w, sg_bt, seq, tm):
    t, d = x2.shape
    tiles_per_batch = seq // tm
    tok = lambda i: (i, 0)
    return pl.pallas_call(
        _in_kernel,
        grid=(t // tm,),
        in_specs=[
            pl.BlockSpec((tm, d), tok),
            pl.BlockSpec((1, 6, d), lambda i: (i // tiles_per_batch, 0, 0)),
            _const_spec((1, d)),
            _weight_spec(w1.shape),
            _weight_spec(wuv.shape),
            _const_spec((1, SG_WIDTH)),
            _const_spec((1, SG_WIDTH)),
            _const_spec(sg_w.shape),
            _const_spec(sg_bt.shape),
        ],
        out_specs=[
            pl.BlockSpec((tm, 3 * DN_WIDTH), tok),
            pl.BlockSpec((tm, DN_WIDTH), tok),
            pl.BlockSpec((tm, LANES), tok),
            pl.BlockSpec((tm, SG_WIDTH), tok),
        ],
        out_shape=[
            jax.ShapeDtypeStruct((t, 3 * DN_WIDTH), F32),
            jax.ShapeDtypeStruct((t, DN_WIDTH), F32),
            jax.ShapeDtypeStruct((t, LANES), F32),
            jax.ShapeDtypeStruct((t, SG_WIDTH), BF16),
        ],
        compiler_params=_params("arbitrary"),
        name="in_proj",
    )(x2, mod3, pre_norm, w1, wuv, sg_ln_w, sg_ln_b, sg_w, sg_bt)


def _unit_lower_inverses(a_list):
    c = a_list[0].shape[0]
    i = lax.broadcasted_iota(I32, (c, c), 0)
    j = lax.broadcasted_iota(I32, (c, c), 1)
    eye = (i == j).astype(F32)
    first = (i == j + 1) & ((i & 1) == 1)
    d_list = [eye - jnp.where(first, a, 0.0) for a in a_list]
    b = 2
    while b < c:
        shift = b.bit_length()
        off = ((i >> shift) == (j >> shift)) & ((i & b) != 0) & ((j & b) == 0)
        a_parts = [jnp.where(off, a, 0.0).astype(BF16) for a in a_list]
        d_parts = [d.astype(BF16) for d in d_list]
        t_list = [_dot(dp, ap) for dp, ap in zip(d_parts, a_parts)]
        d_list = [d - _dot(t.astype(BF16), dp) for d, t, dp in zip(d_list, t_list, d_parts)]
        b *= 2
    return d_list


def _dn_kernel(qkv_ref, z_ref, ab_ref, convw_ref, alog_ref, dtb_ref, nw_ref, y_ref, carry_ref, state_ref):
    nb, rows = qkv_ref.shape[0], qkv_ref.shape[1]
    c = DN_CHUNK
    nc = rows // c
    hd = DN_HEAD_DIM
    heads = [(b, h) for b in range(nb) for h in range(DN_HEADS)]
    probs = [(b, ci, h) for b in range(nb) for ci in range(nc) for h in range(DN_HEADS)]

    @pl.when(pl.program_id(0) == 0)
    def _():
        carry_ref[...] = jnp.zeros_like(carry_ref)
        state_ref[...] = jnp.zeros_like(state_ref)

    ab = ab_ref[...].reshape(nb * rows, LANES)
    g = -jnp.exp(alog_ref[...]) * _softplus(ab + dtb_ref[...])
    beta_all = _sigmoid(ab)
    ri = lax.broadcasted_iota(I32, (nb * rows, nb * rows), 0)
    ci_ = lax.broadcasted_iota(I32, (nb * rows, nb * rows), 1)
    tri = ((ri >= ci_) & ((ri // c) == (ci_ // c))).astype(BF16)
    gh, gm, gl = _split3(g)
    gc = _dot(tri, gh) + (_dot(tri, gm) + _dot(tri, gl))
    gct = gc.T

    i = lax.broadcasted_iota(I32, (c, c), 0)
    j = lax.broadcasted_iota(I32, (c, c), 1)
    causal = i >= j
    strict = i > j
    row8 = lax.broadcasted_iota(I32, (SUBLANES, 3 * DN_WIDTH), 0)

    acts = []
    for b in range(nb):
        xc = qkv_ref[b]
        prev = carry_ref[b]
        acc = xc * convw_ref[CONV_WIDTH - 1:CONV_WIDTH, :]
        for s in range(1, CONV_WIDTH):
            rolled = pltpu.roll(xc, s, axis=0)
            top = jnp.where(row8 < s, pltpu.roll(prev, s, axis=0), rolled[:SUBLANES])
            shifted = jnp.concatenate([top, rolled[SUBLANES:]], axis=0)
            acc = acc + shifted * convw_ref[CONV_WIDTH - 1 - s:CONV_WIDTH - s, :]
        carry_ref[b] = xc[rows - SUBLANES:]
        acts.append(_silu(acc))

    gcs, e_gcs, e_rems, e_lasts = {}, {}, {}, {}
    for b in range(nb):
        for ci in range(nc):
            r0 = b * rows + ci * c
            gcb = gc[r0:r0 + c]
            g_last = gcb[c - 1:c, :]
            gcs[b, ci] = gcb
            e_gcs[b, ci] = jnp.exp(gcb)
            e_rems[b, ci] = jnp.exp(g_last - gcb)
            e_lasts[b, ci] = jnp.exp(g_last)

    qn, kn, knb, kb, vb, decay = {}, {}, {}, {}, {}, {}
    for p in probs:
        b, ci, h = p
        r0 = b * rows + ci * c
        act = acts[b][ci * c:(ci + 1) * c]
        q = act[:, h * hd:(h + 1) * hd]
        k = act[:, DN_WIDTH + h * hd:DN_WIDTH + (h + 1) * hd]
        v = act[:, 2 * DN_WIDTH + h * hd:2 * DN_WIDTH + (h + 1) * hd]
        qn[p] = q * (lax.rsqrt(jnp.sum(q * q, axis=-1, keepdims=True) + EPS) * (hd ** -0.5))
        kn[p] = k * lax.rsqrt(jnp.sum(k * k, axis=-1, keepdims=True) + EPS)
        beta = beta_all[r0:r0 + c, DN_HEADS + h:DN_HEADS + h + 1]
        diff = gcs[b, ci][:, h:h + 1] - gct[h:h + 1, r0:r0 + c]
        decay[p] = jnp.where(causal, jnp.exp(jnp.where(causal, diff, 0.0)), 0.0)
        kb[p] = kn[p] * beta
        vb[p] = v * beta
        knb[p] = kn[p].astype(BF16)

    kk = {p: _dot_nt(kb[p].astype(BF16), knb[p]) for p in probs}
    qk = {p: _dot_nt(qn[p].astype(BF16), knb[p]) for p in probs}
    tinv = dict(zip(probs, _unit_lower_inverses([jnp.where(strict, kk[p] * decay[p], 0.0) for p in probs])))
    rhs = {p: jnp.concatenate([vb[p], kb[p] * e_gcs[p[0], p[1]][:, p[2]:p[2] + 1]], axis=1) for p in probs}
    sol = {p: _dot(tinv[p].astype(BF16), rhs[p].astype(BF16)) for p in probs}
    lhs = {p: jnp.concatenate([sol[p][:, hd:], qn[p] * e_gcs[p[0], p[1]][:, p[2]:p[2] + 1]], axis=0).astype(BF16)
           for p in probs}
    qkm = {p: jnp.where(causal, qk[p] * decay[p], 0.0).astype(BF16) for p in probs}
    kd = {p: (kn[p] * e_rems[p[0], p[1]][:, p[2]:p[2] + 1]).astype(BF16) for p in probs}

    state = {bh: state_ref[bh[0] * DN_HEADS + bh[1]] for bh in heads}
    for ci in range(nc):
        cur = [(b, ci, h) for (b, h) in heads]
        ws = {p: _dot(lhs[p], state[p[0], p[2]].astype(BF16)) for p in cur}
        vnb = {p: (sol[p][:, :hd] - ws[p][:c]).astype(BF16) for p in cur}
        o = {p: ws[p][c:] + _dot(qkm[p], vnb[p]) for p in cur}
        upd = {p: _dot_tn(kd[p], vnb[p]) for p in cur}
        for p in cur:
            b, _, h = p
            state[b, h] = state[b, h] * e_lasts[b, ci][:, h:h + 1] + upd[p]
            zh = z_ref[b, ci * c:(ci + 1) * c, h * hd:(h + 1) * hd]
            y_ref[b, ci * c:(ci + 1) * c, h * hd:(h + 1) * hd] = (_rms(o[p]) * nw_ref[...] * _silu(zh)).astype(BF16)
    for bh in heads:
        state_ref[bh[0] * DN_HEADS + bh[1]] = state[bh]


def _delta_net(qkv3, z3, ab3, conv_w, alog_row, dtb_row, dn_norm_w):
    nb, s, _ = qkv3.shape
    c = DN_CHUNK * DN_CHUNKS_PER_STEP
    blk = lambda n: (0, n, 0)
    return pl.pallas_call(
        _dn_kernel,
        grid=(s // c,),
        in_specs=[
            pl.BlockSpec((nb, c, 3 * DN_WIDTH), blk),
            pl.BlockSpec((nb, c, DN_WIDTH), blk),
            pl.BlockSpec((nb, c, LANES), blk),
            _const_spec(conv_w.shape),
            _const_spec((1, LANES)),
            _const_spec((1, LANES)),
            _const_spec((1, DN_HEAD_DIM)),
        ],
        out_specs=pl.BlockSpec((nb, c, DN_WIDTH), blk),
        out_shape=jax.ShapeDtypeStruct((nb, s, DN_WIDTH), BF16),
        scratch_shapes=[
            pltpu.VMEM((nb, SUBLANES, 3 * DN_WIDTH), F32),
            pltpu.VMEM((nb * DN_HEADS, DN_HEAD_DIM, DN_HEAD_DIM), F32),
        ],
        compiler_params=_params("arbitrary"),
        name="delta_net",
    )(qkv3, z3, ab3, conv_w, alog_row, dtb_row, dn_norm_w)


def _mix_kernel(x_ref, ydn_ref, ysg_ref, mod_ref, pn_ref, postn_ref, fpn_ref,
                wbg_ref, bbg_ref, wpd_ref, wps_ref, wout_ref, wrh_ref, wrl_ref,
                x1_ref, hf_ref, sct_ref):
    d = x_ref.shape[1]
    x = x_ref[...]
    mod = mod_ref[0]
    hm = _rms(x) * pn_ref[...] * (1.0 + mod[1:2, :]) + mod[0:1, :]
    gates = _sigmoid(_dot(hm.astype(BF16), wbg_ref[...]) + bbg_ref[...])
    merged = gates[:, :d] * _dot(ydn_ref[...], wpd_ref[...]) + gates[:, d:] * _dot(ysg_ref[...], wps_ref[...])
    y = _dot(merged.astype(BF16), wout_ref[...])
    x1 = x + mod[2:3, :] * (_rms(y) * postn_ref[...])
    x1_ref[...] = x1
    hf = _rms(x1) * fpn_ref[...] * (1.0 + mod[4:5, :]) + mod[3:4, :]
    for j in range(ROW_TILE):
        hf_ref[pl.ds(j, hf.shape[0], stride=ROW_TILE), :] = hf[:, j * LANES:(j + 1) * LANES]
    hh, hl = _split2(hf)
    logits_t = _dot_nt(wrh_ref[...], hh) + (_dot_nt(wrh_ref[...], hl) + _dot_nt(wrl_ref[...], hh))
    sct_ref[...] = _sigmoid(logits_t)


def _mix(x2, ydn, ysg, mod3, pre_norm, post_norm, ffn_pre_norm, wbg, bbg, wpd, wps, wout, wrh, wrl, seq, tm):
    t, d = x2.shape
    tiles_per_batch = seq // tm
    tok = lambda i: (i, 0)
    consts = [pre_norm, post_norm, ffn_pre_norm, wbg, bbg, wpd, wps, wout, wrh, wrl]
    return pl.pallas_call(
        _mix_kernel,
        grid=(t // tm,),
        in_specs=[
            pl.BlockSpec((tm, d), tok),
            pl.BlockSpec((tm, DN_WIDTH), tok),
            pl.BlockSpec((tm, SG_WIDTH), tok),
            pl.BlockSpec((1, 6, d), lambda i: (i // tiles_per_batch, 0, 0)),
        ] + [_weight_spec(a.shape) for a in consts],
        out_specs=[
            pl.BlockSpec((tm, d), tok),
            pl.BlockSpec((tm * ROW_TILE, LANES), tok),
            pl.BlockSpec((N_EXPERTS, tm), lambda i: (0, i)),
        ],
        out_shape=[
            jax.ShapeDtypeStruct((t, d), F32),
            jax.ShapeDtypeStruct((t * ROW_TILE, LANES), F32),
            jax.ShapeDtypeStruct((N_EXPERTS, t), F32),
        ],
        compiler_params=_params("arbitrary"),
        name="mix",
    )(x2, ydn, ysg, mod3, *consts)


def _first_argmax(vals, idx):
    m = jnp.max(vals, axis=0, keepdims=True)
    first = jnp.min(jnp.where(vals == m, idx, jnp.int32(2 ** 30)), axis=0, keepdims=True)
    return m, first


def _route_kernel(sct_ref, bias_ref, idx_ref, wtt_ref, rank_ref, cnt_ref, carry_ref):
    tm = sct_ref.shape[1]

    @pl.when(pl.program_id(0) == 0)
    def _():
        carry_ref[...] = jnp.zeros_like(carry_ref)

    scores = sct_ref[...]
    sel = scores + bias_ref[...]
    erow = lax.broadcasted_iota(I32, (N_EXPERTS, tm), 0)
    grow = lax.broadcasted_iota(I32, (GROUP_SIZE, tm), 0)

    gs = []
    for gidx in range(N_EXPERT_GROUPS):
        sg = sel[gidx * GROUP_SIZE:(gidx + 1) * GROUP_SIZE]
        m1, first = _first_argmax(sg, grow)
        m2 = jnp.max(jnp.where(grow == first, NEG_INF, sg), axis=0, keepdims=True)
        gs.append(m1 + m2)
    gsc = jnp.concatenate(gs, axis=0)
    giota = lax.broadcasted_iota(I32, (N_EXPERT_GROUPS, tm), 0)
    gmask = jnp.zeros((N_EXPERT_GROUPS, tm), F32)
    cur = gsc
    for _ in range(TOPK_GROUPS):
        _, gi = _first_argmax(cur, giota)
        pick = giota == gi
        gmask = jnp.where(pick, 1.0, gmask)
        cur = jnp.where(pick, NEG_INF, cur)
    masked = jnp.concatenate(
        [jnp.where(gmask[gidx:gidx + 1, :] > 0.5, sel[gidx * GROUP_SIZE:(gidx + 1) * GROUP_SIZE], NEG_INF)
         for gidx in range(N_EXPERT_GROUPS)], axis=0)

    cur = masked
    idxs, wts = [], []
    onehot = jnp.zeros((N_EXPERTS, tm), F32)
    for _ in range(TOP_K):
        _, ei = _first_argmax(cur, erow)
        pick = erow == ei
        idxs.append(ei)
        wts.append(jnp.sum(jnp.where(pick, scores, 0.0), axis=0, keepdims=True))
        onehot = jnp.where(pick, 1.0, onehot)
        cur = jnp.where(pick, NEG_INF, cur)
    idx = jnp.concatenate(idxs, axis=0)
    wt = jnp.concatenate(wts, axis=0)
    wt = wt / jnp.sum(wt, axis=0, keepdims=True) * ROUTED_SCALE
    idx_ref[...] = idx
    wpad = jnp.concatenate([wt, jnp.zeros((LANES - TOP_K, tm), F32)], axis=0)
    wtt_ref[...] = wpad.T

    ti = lax.broadcasted_iota(I32, (tm, tm), 0)
    tj = lax.broadcasted_iota(I32, (tm, tm), 1)
    upper = (ti < tj).astype(BF16)
    before = _dot(onehot.astype(BF16), upper) + carry_ref[...]
    rank_ref[...] = jnp.concatenate(
        [jnp.sum(jnp.where(erow == idxs[kk], before, 0.0), axis=0, keepdims=True) for kk in range(TOP_K)],
        axis=0).astype(I32)
    total = carry_ref[...] + jnp.sum(onehot, axis=1, keepdims=True)
    carry_ref[...] = total
    cnt_ref[...] = total


def _route(sct, bias_col, tm):
    e, t = sct.shape
    tile = lambda i: (0, i)
    return pl.pallas_call(
        _route_kernel,
        grid=(t // tm,),
        in_specs=[pl.BlockSpec((e, tm), tile), _const_spec((e, 1))],
        out_specs=[
            pl.BlockSpec((TOP_K, tm), tile),
            pl.BlockSpec((tm, LANES), lambda i: (i, 0)),
            pl.BlockSpec((TOP_K, tm), tile),
            _const_spec((e, 1)),
        ],
        out_shape=[
            jax.ShapeDtypeStruct((TOP_K, t), I32),
            jax.ShapeDtypeStruct((t, LANES), F32),
            jax.ShapeDtypeStruct((TOP_K, t), I32),
            jax.ShapeDtypeStruct((e, 1), F32),
        ],
        scratch_shapes=[pltpu.VMEM((e, 1), F32)],
        compiler_params=_params("arbitrary"),
        name="route",
    )(sct, bias_col)


def _dest_kernel(cnt_ref, idx_ref, rank_ref, dest_ref, meta_ref):
    tm = idx_ref.shape[1]
    e = N_EXPERTS
    cnt = cnt_ref[...]
    padded = jnp.floor((cnt + (MOE_BLOCK - 1)) * (1.0 / MOE_BLOCK)) * MOE_BLOCK
    pw = jnp.broadcast_to(padded, (e, LANES))
    ri = lax.broadcasted_iota(I32, (e, e), 0)
    ci = lax.broadcasted_iota(I32, (e, e), 1)
    lower = (ri >= ci).astype(BF16)
    ph, pm, plo = _split3(pw)
    pends = _dot(lower, ph) + (_dot(lower, pm) + _dot(lower, plo))
    pstart = pends - pw
    erow = lax.broadcasted_iota(I32, (e, tm), 0)
    idx = idx_ref[...]
    rows = [jnp.sum(jnp.where(erow == idx[kk:kk + 1, :], pstart[:, 0:1], 0.0), axis=0, keepdims=True)
            for kk in range(TOP_K)]
    dest_ref[...] = jnp.concatenate(rows, axis=0).astype(I32) + rank_ref[...]
    lane = lax.broadcasted_iota(I32, (e, LANES), 1)
    blocks = jnp.where(lane == 0, pstart, pw) * (1.0 / MOE_BLOCK)
    pads = jnp.where(lane == 2, pstart + cnt, pw - cnt)
    meta_ref[...] = jnp.where(lane < 2, blocks, pads).astype(I32)


def _dest(cnt, idx, rank, tm):
    e = cnt.shape[0]
    t = idx.shape[1]
    tile = lambda i: (0, i)
    return pl.pallas_call(
        _dest_kernel,
        grid=(t // tm,),
        in_specs=[_const_spec((e, 1)), pl.BlockSpec((TOP_K, tm), tile), pl.BlockSpec((TOP_K, tm), tile)],
        out_specs=[pl.BlockSpec((TOP_K, tm), tile), _const_spec((e, LANES))],
        out_shape=[jax.ShapeDtypeStruct((TOP_K, t), I32), jax.ShapeDtypeStruct((e, LANES), I32)],
        compiler_params=_params("arbitrary"),
        name="dest",
    )(cnt, idx, rank)


def _row_copy(src, dst, sem):
    return pltpu.make_async_copy(src, dst, sem)


def _rows(first, count=1):
    return pl.ds(pl.multiple_of(first * ROW_TILE, ROW_TILE), count * ROW_TILE)


PAD_PIECES = tuple(MOE_BLOCK >> s for s in range(1, MOE_BLOCK.bit_length()))


def _zero_unassigned_rows(pad_start_ref, pad_rows_ref, xs_out, zero_ref, sem):
    zero_ref[...] = jnp.zeros_like(zero_ref)
    ne = pad_start_ref.shape[0]
    total = xs_out.shape[0] // (MOE_BLOCK * ROW_TILE)

    def piece(rows, pos):
        return pltpu.make_async_copy(zero_ref.at[_rows(0, rows)], xs_out.at[_rows(pos, rows)], sem)

    def pads(wait):
        def per_expert(ex, carry):
            pos = pad_start_ref[ex]
            pad = pad_rows_ref[ex]
            for rows in PAD_PIECES:
                @pl.when((pad & rows) != 0)
                def _():
                    piece(rows, pos).wait() if wait else piece(rows, pos).start()
                pos = pos + (pad & rows)
            return carry
        lax.fori_loop(0, ne, per_expert, 0)

    def tail(wait):
        used = (pad_start_ref[ne - 1] + pad_rows_ref[ne - 1]) // MOE_BLOCK

        def per_block(blk, carry):
            cp = piece(MOE_BLOCK, blk * MOE_BLOCK)
            cp.wait() if wait else cp.start()
            return carry
        lax.fori_loop(used, total, per_block, 0)

    pads(False)
    tail(False)
    pads(True)
    tail(True)


def _scatter_kernel(pad_start_ref, pad_rows_ref, dest_hbm, hf_ref, wsg_ref, wsu_ref, wsd_ref, xs_out, sh_ref,
                    dest_smem, zero_ref, sem_idx, sem_rows, sem_zero):
    tm = hf_ref.shape[0] // ROW_TILE
    i = pl.program_id(0)

    @pl.when(i == 0)
    def _():
        _zero_unassigned_rows(pad_start_ref, pad_rows_ref, xs_out, zero_ref, sem_zero)

    idx_copy = pltpu.make_async_copy(dest_hbm.at[i], dest_smem, sem_idx)
    idx_copy.start()
    idx_copy.wait()

    def issue(tok, carry):
        for kk in range(TOP_K):
            slot = dest_smem[kk * tm + tok]
            _row_copy(hf_ref.at[_rows(tok)], xs_out.at[_rows(slot)], sem_rows).start(priority=kk % 2)
        return carry

    lax.fori_loop(0, tm, issue, 0)

    hb = jnp.concatenate([hf_ref[pl.ds(j, tm, stride=ROW_TILE), :] for j in range(ROW_TILE)], axis=1).astype(BF16)
    hid = _silu(_dot(hb, wsg_ref[...])) * _dot(hb, wsu_ref[...])
    sh_ref[...] = _dot(hid.astype(BF16), wsd_ref[...])

    for kk in range(TOP_K):
        _row_copy(hf_ref, xs_out.at[_rows(0, tm)], sem_rows).wait()


def _scatter(pad_start, pad_rows, dest_tiles, hf, wsg, wsu, wsd, n_slots, tm):
    t = hf.shape[0] // ROW_TILE
    d = wsg.shape[0]
    const = lambda a: pl.BlockSpec(a.shape, lambda i, ps, pr: (0, 0))
    n_tiles = t // tm
    return pl.pallas_call(
        _scatter_kernel,
        grid_spec=pltpu.PrefetchScalarGridSpec(
            num_scalar_prefetch=2,
            grid=(n_tiles,),
            in_specs=[pl.BlockSpec(memory_space=pl.ANY), pl.BlockSpec((tm * ROW_TILE, LANES), lambda i, ps, pr: (i, 0)),
                      const(wsg), const(wsu), const(wsd)],
            out_specs=[pl.BlockSpec(memory_space=pl.ANY), pl.BlockSpec((tm, d), lambda i, ps, pr: (i, 0))],
            scratch_shapes=[
                pltpu.SMEM((TOP_K * tm,), I32),
                pltpu.VMEM((MOE_BLOCK * ROW_TILE, LANES), hf.dtype),
                pltpu.SemaphoreType.DMA,
                pltpu.SemaphoreType.DMA,
                pltpu.SemaphoreType.DMA,
            ],
        ),
        out_shape=[jax.ShapeDtypeStruct((n_slots * ROW_TILE, LANES), hf.dtype), jax.ShapeDtypeStruct((t, d), F32)],
        compiler_params=_params("arbitrary"),
        name="scatter",
    )(pad_start, pad_rows, dest_tiles, hf, wsg, wsu, wsd)


MOE_BUFFERS = 6


def _moe_kernel(bstart_ref, nblk_ref, xs_hbm, wg_ref, wu_ref, wd_ref, ys_hbm,
                xbuf, ybuf, wgb_ref, wub_ref, wdb_ref, sem_in, sem_out):
    ex = pl.program_id(0)
    last = pl.num_programs(0) - 1
    n = nblk_ref[ex]
    b0 = bstart_ref[ex]
    used = bstart_ref[last] + nblk_ref[last]

    def in_copy(g):
        slot = lax.rem(g, MOE_BUFFERS)
        return pltpu.make_async_copy(xs_hbm.at[_rows(g * MOE_BLOCK, MOE_BLOCK)], xbuf.at[slot], sem_in.at[slot])

    def out_copy(g):
        slot = lax.rem(g, MOE_BUFFERS)
        return pltpu.make_async_copy(ybuf.at[slot], ys_hbm.at[_rows(g * MOE_BLOCK, MOE_BLOCK)], sem_out.at[slot])

    @pl.when(ex == 0)
    def _():
        for s in range(MOE_BUFFERS - 1):
            @pl.when(s < used)
            def _():
                in_copy(s).start()

    wgb_ref[...] = wg_ref[0].astype(BF16)
    wub_ref[...] = wu_ref[0].astype(BF16)
    wdb_ref[...] = wd_ref[0].astype(BF16)

    def body(g, carry):
        slot = lax.rem(g, MOE_BUFFERS)
        in_copy(g).wait()

        @pl.when(g + (MOE_BUFFERS - 1) < used)
        def _():
            in_copy(g + (MOE_BUFFERS - 1)).start()

        @pl.when(g >= MOE_BUFFERS)
        def _():
            out_copy(g - MOE_BUFFERS).wait()

        xb = jnp.concatenate([xbuf[slot, pl.ds(j, MOE_BLOCK, stride=ROW_TILE), :] for j in range(ROW_TILE)],
                             axis=1).astype(BF16)
        hid = _silu(_dot(xb, wgb_ref[...])) * _dot(xb, wub_ref[...])
        y = _dot(hid.astype(BF16), wdb_ref[...])
        for j in range(ROW_TILE):
            ybuf[slot, pl.ds(j, MOE_BLOCK, stride=ROW_TILE), :] = y[:, j * LANES:(j + 1) * LANES]
        out_copy(g).start()
        return carry

    lax.fori_loop(b0, b0 + n, body, 0)

    @pl.when(ex == last)
    def _():
        for s in range(MOE_BUFFERS):
            @pl.when(s < used)
            def _():
                out_copy(used - 1 - s).wait()

        total = ys_hbm.shape[0] // (MOE_BLOCK * ROW_TILE)
        ybuf[0] = jnp.zeros(ybuf.shape[1:], ybuf.dtype)

        def tail_copy(blk):
            return pltpu.make_async_copy(ybuf.at[0], ys_hbm.at[_rows(blk * MOE_BLOCK, MOE_BLOCK)], sem_out.at[0])

        def start_tail(blk, carry):
            tail_copy(blk).start()
            return carry

        def wait_tail(blk, carry):
            tail_copy(blk).wait()
            return carry

        lax.fori_loop(used, total, start_tail, 0)
        lax.fori_loop(used, total, wait_tail, 0)


def _moe(bstart, nblk_e, xs, w_gate, w_up, w_down):
    ne, d, eh = w_gate.shape
    wspec = lambda shape: pl.BlockSpec(shape, lambda ex, bs, nb: (ex, 0, 0))
    return pl.pallas_call(
        _moe_kernel,
        grid_spec=pltpu.PrefetchScalarGridSpec(
            num_scalar_prefetch=2,
            grid=(ne,),
            in_specs=[pl.BlockSpec(memory_space=pl.ANY), wspec((1, d, eh)), wspec((1, d, eh)), wspec((1, eh, d))],
            out_specs=pl.BlockSpec(memory_space=pl.ANY),
            scratch_shapes=[
                pltpu.VMEM((MOE_BUFFERS, MOE_BLOCK * ROW_TILE, LANES), F32),
                pltpu.VMEM((MOE_BUFFERS, MOE_BLOCK * ROW_TILE, LANES), F32),
                pltpu.VMEM((d, eh), BF16), pltpu.VMEM((d, eh), BF16), pltpu.VMEM((eh, d), BF16),
                pltpu.SemaphoreType.DMA((MOE_BUFFERS,)),
                pltpu.SemaphoreType.DMA((MOE_BUFFERS,)),
            ],
        ),
        out_shape=jax.ShapeDtypeStruct(xs.shape, F32),
        compiler_params=_params("arbitrary"),
        name="moe",
    )(bstart, nblk_e, xs, w_gate, w_up, w_down)


def _combine_kernel(dest_hbm, ys_hbm, wtt_ref, x1_ref, sh_ref, mod_ref, postn_ref, o_ref, dest_smem0, dest_smem1, buf_ref, sem_idx, sem_rows):
    tc = x1_ref.shape[0]
    i = pl.program_id(0)
    n = pl.num_programs(0)
    dest_smem = (dest_smem0, dest_smem1)

    def idx_copy(tile, slot):
        return pltpu.make_async_copy(dest_hbm.at[tile], dest_smem[slot], sem_idx.at[slot])

    def issue_rows(slot):
        def issue(tok, carry):
            for kk in range(TOP_K):
                row = dest_smem[slot][kk * tc + tok]
                _row_copy(ys_hbm.at[_rows(row)], buf_ref.at[slot, kk, _rows(tok)],
                          sem_rows.at[slot]).start(priority=kk % 2)
            return carry
        lax.fori_loop(0, tc, issue, 0)

    @pl.when(i == 0)
    def _():
        idx_copy(0, 0).start()
        idx_copy(0, 0).wait()

        @pl.when(n > 1)
        def _():
            idx_copy(1, 1).start()
        issue_rows(0)

    def step(cur):
        nxt = 1 - cur

        @pl.when(i + 1 < n)
        def _():
            idx_copy(i + 1, nxt).wait()

            @pl.when(i + 2 < n)
            def _():
                idx_copy(i + 2, cur).start()
            issue_rows(nxt)

        for kk in range(TOP_K):
            _row_copy(ys_hbm.at[_rows(0, tc)], buf_ref.at[cur, kk], sem_rows.at[cur]).wait()

        wtt = wtt_ref[...]
        wk = [jnp.broadcast_to(wtt[:, kk:kk + 1], (tc, LANES)) for kk in range(TOP_K)]
        cols = []
        for j in range(ROW_TILE):
            lane_group = pl.ds(j, tc, stride=ROW_TILE)
            acc = buf_ref[cur, 0, lane_group, :] * wk[0]
            for kk in range(1, TOP_K):
                acc = acc + buf_ref[cur, kk, lane_group, :] * wk[kk]
            cols.append(acc)
        y = jnp.concatenate(cols, axis=1) + sh_ref[...]
        mod = mod_ref[0]
        o_ref[...] = x1_ref[...] + mod[5:6, :] * (_rms(y) * postn_ref[...])

    parity = lax.rem(i, 2)
    for cur in range(2):
        @pl.when(parity == cur)
        def _():
            step(cur)


def _combine(dest_tiles, ys, wtt, x1, shared, mod3, ffn_post_norm, seq, tc):
    t, d = x1.shape
    tiles_per_batch = seq // tc
    tok = lambda i: (i, 0)
    return pl.pallas_call(
        _combine_kernel,
        grid=(t // tc,),
        in_specs=[
            pl.BlockSpec(memory_space=pl.ANY),
            pl.BlockSpec(memory_space=pl.ANY),
            pl.BlockSpec((tc, LANES), tok),
            pl.BlockSpec((tc, d), tok),
            pl.BlockSpec((tc, d), tok),
            pl.BlockSpec((1, 6, d), lambda i: (i // tiles_per_batch, 0, 0)),
            _const_spec((1, d)),
        ],
        out_specs=pl.BlockSpec((tc, d), tok),
        out_shape=jax.ShapeDtypeStruct((t, d), F32),
        scratch_shapes=[
            pltpu.SMEM((TOP_K * tc,), I32),
            pltpu.SMEM((TOP_K * tc,), I32),
            pltpu.VMEM((2, TOP_K, tc * ROW_TILE, LANES), F32),
            pltpu.SemaphoreType.DMA((2,)),
            pltpu.SemaphoreType.DMA((2,)),
        ],
        compiler_params=_params("arbitrary"),
        name="combine",
    )(dest_tiles, ys, wtt, x1, shared, mod3, ffn_post_norm)


def _tile(n, want):
    t = min(n, want)
    assert n % t == 0
    return t


def _dest_tiles(dest, tile):
    k, t = dest.shape
    return dest.reshape(k, t // tile, tile).transpose(1, 0, 2).reshape(t // tile, k * tile)


def kernel(x, c, ada_w, ada_b, mix_pre_norm, mix_post_norm, w_in, conv_w, a_log, dt_bias, dn_norm_w, sg_ln_w, sg_ln_b, sg_w, sg_b, w_branch_gate, b_branch_gate, w_proj_dn, w_proj_sg, w_out, ffn_pre_norm, ffn_post_norm, w_router, router_bias, w_exp_gate, w_exp_up, w_exp_down, w_sh_gate, w_sh_up, w_sh_down):
    nb, seq, d = x.shape
    depth = ada_w.shape[0]
    t = nb * seq
    tm = _tile(seq, 1024)
    tr = _tile(t, 512)
    tsc = _tile(t, 512)
    tcm = _tile(seq, 256)
    nblk = -(-t * TOP_K // MOE_BLOCK) + N_EXPERTS
    row = lambda v: v.reshape(1, -1)
    pad_lanes = lambda v: jnp.pad(v.astype(F32), (0, LANES - v.shape[0])).reshape(1, LANES)

    x2 = x.reshape(t, d)
    for l in range(depth):
        mod3 = _ada(c, ada_w[l], ada_b[l]).reshape(nb, 6, d)

        wi = w_in[l]
        qkvz, ab_cols, uv = wi[:, :4 * DN_WIDTH], wi[:, 4 * DN_WIDTH:4 * DN_WIDTH + 2 * DN_HEADS], wi[:, 4 * DN_WIDTH + 2 * DN_HEADS:]
        w1 = jnp.concatenate([qkvz, ab_cols, jnp.zeros((d, LANES - 2 * DN_HEADS), wi.dtype)], axis=1).astype(BF16)
        qkv, z, ab, ysg = _in_proj(x2, mod3, row(mix_pre_norm[l]), w1, uv.astype(BF16), row(sg_ln_w[l]), row(sg_ln_b[l]),
                                   sg_w[l], sg_b[l].T, seq, tm)

        ydn = _delta_net(qkv.reshape(nb, seq, -1), z.reshape(nb, seq, -1), ab.reshape(nb, seq, -1), conv_w[l],
                         pad_lanes(a_log[l]), pad_lanes(dt_bias[l]), row(dn_norm_w[l]))

        wr_t = w_router[l].T
        wrh = wr_t.astype(BF16)
        wrl = (wr_t - wrh.astype(F32)).astype(BF16)
        x1, hf, sct = _mix(
            x2, ydn.reshape(t, -1), ysg, mod3, row(mix_pre_norm[l]), row(mix_post_norm[l]), row(ffn_pre_norm[l]),
            w_branch_gate[l].astype(BF16), row(b_branch_gate[l]), w_proj_dn[l].astype(BF16), w_proj_sg[l].astype(BF16),
            w_out[l].astype(BF16), wrh, wrl, seq, tm)

        idx, wtt, rank, cnt = _route(sct, router_bias[l].reshape(-1, 1), tr)
        dest, meta = _dest(cnt, idx, rank, tr)
        xs, shared = _scatter(meta[:, 2], meta[:, 3], _dest_tiles(dest, tsc), hf, w_sh_gate[l].astype(BF16),
                              w_sh_up[l].astype(BF16), w_sh_down[l].astype(BF16), nblk * MOE_BLOCK, tsc)
        ys = _moe(meta[:, 0], meta[:, 1], xs, w_exp_gate[l], w_exp_up[l], w_exp_down[l])
        x2 = _combine(_dest_tiles(dest, tcm), ys, wtt, x1, shared, mod3, row(ffn_post_norm[l]), seq, tcm)
    return x2.reshape(nb, seq, d)
```

```python
import functools

import jax
import jax.numpy as jnp
from jax import lax
from jax.experimental import pallas as pl
from jax.experimental.pallas import tpu as pltpu

F32 = jnp.float32
BF16 = jnp.bfloat16
I32 = jnp.int32

D_MODEL = 1024
DN_HEADS = 4
DN_HEAD_DIM = 128
DN_WIDTH = DN_HEADS * DN_HEAD_DIM
DN_CHUNK = 64
DN_CHUNKS_PER_STEP = 4
CONV_WIDTH = 4
SG_GROUPS = 4
SG_GROUP_DIM = 128
SG_WIDTH = SG_GROUPS * SG_GROUP_DIM
SG_CHUNK = 128
N_EXPERTS = 256
N_EXPERT_GROUPS = 8
GROUP_SIZE = N_EXPERTS // N_EXPERT_GROUPS
TOPK_GROUPS = 4
TOP_K = 8
EXPERT_HIDDEN = 256
ROUTED_SCALE = 2.5
MOE_BLOCK = 256
EPS = 1e-6

LANES = 128
SUBLANES = 8
DENSE_TILE = 1024
ROUTE_TILE = 512
SCATTER_TILE = 512
COMBINE_TILE = 256
ROW_TILE = SUBLANES
VMEM_LIMIT = 56 * 1024 * 1024
NEG_INF = float("-inf")


def _dot(a, b):
    return jnp.dot(a, b, preferred_element_type=F32)


def _dot_nt(a, b):
    return lax.dot_general(a, b, (((1,), (1,)), ((), ())), preferred_element_type=F32)


def _dot_tn(a, b):
    return lax.dot_general(a, b, (((0,), (0,)), ((), ())), preferred_element_type=F32)


def _split2(x):
    hi = x.astype(BF16)
    lo = (x - hi.astype(F32)).astype(BF16)
    return hi, lo


def _split3(x):
    hi = x.astype(BF16)
    r = x - hi.astype(F32)
    mid = r.astype(BF16)
    lo = (r - mid.astype(F32)).astype(BF16)
    return hi, mid, lo


def _dot_hp(a, b, dot=_dot):
    ah, al = _split2(a)
    bh, bl = _split2(b)
    return dot(ah, bh) + (dot(ah, bl) + dot(al, bh))


def _sigmoid(x):
    return 1.0 / (1.0 + jnp.exp(-x))


def _silu(x):
    return x * _sigmoid(x)


def _gelu(x):
    return 0.5 * x * (1.0 + lax.erf(x * (2.0 ** -0.5)))


def _softplus(x):
    return jnp.maximum(x, 0.0) + jnp.log1p(jnp.exp(-jnp.abs(x)))


def _rms(x):
    return x * lax.rsqrt(jnp.mean(x * x, axis=-1, keepdims=True) + EPS)


def _params(*sem):
    return pltpu.CompilerParams(dimension_semantics=sem, vmem_limit_bytes=VMEM_LIMIT)


def _const_spec(shape):
    nd = len(shape)
    return pl.BlockSpec(shape, lambda *_: (0,) * nd)


def _weight_spec(shape):
    nd = len(shape)
    return pl.BlockSpec(shape, lambda *_: (0,) * nd, pipeline_mode=pl.Buffered(1))


def _ada_kernel(c_ref, w_ref, b_ref, o_ref):
    cs = _silu(c_ref[...])
    o_ref[...] = _dot_hp(cs, w_ref[...]) + b_ref[...]


def _ada(c, ada_w, ada_b):
    b, d = c.shape
    n = ada_w.shape[1]
    tn = d
    return pl.pallas_call(
        _ada_kernel,
        grid=(n // tn,),
        in_specs=[_const_spec((b, d)), pl.BlockSpec((d, tn), lambda j: (0, j)), pl.BlockSpec((1, tn), lambda j: (0, j))],
        out_specs=pl.BlockSpec((b, tn), lambda j: (0, j)),
        out_shape=jax.ShapeDtypeStruct((b, n), F32),
        compiler_params=_params("arbitrary"),
        name="ada",
    )(c, ada_w, ada_b.reshape(1, n))


W1_COLS = 4 * DN_WIDTH + LANES


def _in_kernel(x_ref, mod_ref, pn_ref, w1_ref, wuv_ref, lnw_ref, lnb_ref, sgw_ref, sgbt_ref,
               qkv_ref, z_ref, ab_ref, ysg_ref):
    tm = x_ref.shape[0]
    mod = mod_ref[0]
    hm = _rms(x_ref[...]) * pn_ref[...] * (1.0 + mod[1:2, :]) + mod[0:1, :]
    hb = hm.astype(BF16)
    p1 = _dot(hb, w1_ref[...])
    qkv_ref[...] = p1[:, :3 * DN_WIDTH]
    z_ref[...] = p1[:, 3 * DN_WIDTH:4 * DN_WIDTH]
    ab_ref[...] = p1[:, 4 * DN_WIDTH:]
    uv = _dot(hb, wuv_ref[...])
    u = _gelu(uv[:, :SG_WIDTH])
    vg = _gelu(uv[:, SG_WIDTH:])
    mu = jnp.mean(vg, axis=-1, keepdims=True)
    dv = vg - mu
    var = jnp.mean(dv * dv, axis=-1, keepdims=True)
    vgn = (dv * lax.rsqrt(var + EPS) * lnw_ref[...] + lnb_ref[...]).astype(BF16)
    row = lax.broadcasted_iota(I32, (SG_CHUNK, SG_CHUNK), 0)
    col = lax.broadcasted_iota(I32, (SG_CHUNK, SG_CHUNK), 1)
    tril = row >= col
    for g in range(SG_GROUPS):
        wg = jnp.where(tril, sgw_ref[g], 0.0).astype(BF16)
        bg = sgbt_ref[:, g:g + 1]
        cs = slice(g * SG_GROUP_DIM, (g + 1) * SG_GROUP_DIM)
        for n in range(tm // SG_CHUNK):
            rs = slice(n * SG_CHUNK, (n + 1) * SG_CHUNK)
            mixed = _dot(wg, vgn[rs, cs]) + bg
            ysg_ref[rs, cs] = (u[rs, cs] * mixed).astype(BF16)


def _in_proj(x2, mod3, pre_norm, w1, wuv, sg_ln_w, sg_ln_b, sg_w, sg_bt, seq, tm):
    t, d = x2.shape
    tiles_per_batch = seq // tm
    tok = lambda i: (i, 0)
    return pl.pallas_call(
        _in_kernel,
        grid=(t // tm,),
        in_specs=[
            pl.BlockSpec((tm, d), tok),
            pl.BlockSpec((1, 6, d), lambda i: (i // tiles_per_batch, 0, 0)),
            _const_spec((1, d)),
            _weight_spec(w1.shape),
            _weight_spec(wuv.shape),
            _const_spec((1, SG_WIDTH)),
            _const_spec((1, SG_WIDTH)),
            _const_spec(sg_w.shape),
            _const_spec(sg_bt.shape),
        ],
        out_specs=[
            pl.BlockSpec((tm, 3 * DN_WIDTH), tok),
            pl.BlockSpec((tm, DN_WIDTH), tok),
            pl.BlockSpec((tm, LANES), tok),
            pl.BlockSpec((tm, SG_WIDTH), tok),
        ],
        out_shape=[
            jax.ShapeDtypeStruct((t, 3 * DN_WIDTH), F32),
            jax.ShapeDtypeStruct((t, DN_WIDTH), F32),
            jax.ShapeDtypeStruct((t, LANES), F32),
            jax.ShapeDtypeStruct((t, SG_WIDTH), BF16),
        ],
        compiler_params=_params("arbitrary"),
        name="in_proj",
    )(x2, mod3, pre_norm, w1, wuv, sg_ln_w, sg_ln_b, sg_w, sg_bt)


def _unit_lower_inverses(a_list):
    c = a_list[0].shape[0]
    i = lax.broadcasted_iota(I32, (c, c), 0)
    j = lax.broadcasted_iota(I32, (c, c), 1)
    eye = (i == j).astype(F32)
    first = (i == j + 1) & ((i & 1) == 1)
    d_list = [eye - jnp.where(first, a, 0.0) for a in a_list]
    b = 2
    while b < c:
        shift = b.bit_length()
        off = ((i >> shift) == (j >> shift)) & ((i & b) != 0) & ((j & b) == 0)
        a_parts = [jnp.where(off, a, 0.0).astype(BF16) for a in a_list]
        d_parts = [d.astype(BF16) for d in d_list]
        t_list = [_dot(dp, ap) for dp, ap in zip(d_parts, a_parts)]
        d_list = [d - _dot(t.astype(BF16), dp) for d, t, dp in zip(d_list, t_list, d_parts)]
        b *= 2
    return d_list


def _dn_kernel(qkv_ref, z_ref, ab_ref, convw_ref, alog_ref, dtb_ref, nw_ref, y_ref, carry_ref, state_ref):
    nb, rows = qkv_ref.shape[0], qkv_ref.shape[1]
    c = DN_CHUNK
    nc = rows // c
    hd = DN_HEAD_DIM
    heads = [(b, h) for b in range(nb) for h in range(DN_HEADS)]
    probs = [(b, ci, h) for b in range(nb) for ci in range(nc) for h in range(DN_HEADS)]

    @pl.when(pl.program_id(0) == 0)
    def _():
        carry_ref[...] = jnp.zeros_like(carry_ref)
        state_ref[...] = jnp.zeros_like(state_ref)

    ab = ab_ref[...].reshape(nb * rows, LANES)
    g = -jnp.exp(alog_ref[...]) * _softplus(ab + dtb_ref[...])
    beta_all = _sigmoid(ab)
    ri = lax.broadcasted_iota(I32, (nb * rows, nb * rows), 0)
    ci_ = lax.broadcasted_iota(I32, (nb * rows, nb * rows), 1)
    tri = ((ri >= ci_) & ((ri // c) == (ci_ // c))).astype(BF16)
    gh, gm, gl = _split3(g)
    gc = _dot(tri, gh) + (_dot(tri, gm) + _dot(tri, gl))
    gct = gc.T

    i = lax.broadcasted_iota(I32, (c, c), 0)
    j = lax.broadcasted_iota(I32, (c, c), 1)
    causal = i >= j
    strict = i > j
    row8 = lax.broadcasted_iota(I32, (SUBLANES, 3 * DN_WIDTH), 0)

    acts = []
    for b in range(nb):
        xc = qkv_ref[b]
        prev = carry_ref[b]
        acc = xc * convw_ref[CONV_WIDTH - 1:CONV_WIDTH, :]
        for s in range(1, CONV_WIDTH):
            rolled = pltpu.roll(xc, s, axis=0)
            top = jnp.where(row8 < s, pltpu.roll(prev, s, axis=0), rolled[:SUBLANES])
            shifted = jnp.concatenate([top, rolled[SUBLANES:]], axis=0)
            acc = acc + shifted * convw_ref[CONV_WIDTH - 1 - s:CONV_WIDTH - s, :]
        carry_ref[b] = xc[rows - SUBLANES:]
        acts.append(_silu(acc))

    gcs, e_gcs, e_rems, e_lasts = {}, {}, {}, {}
    for b in range(nb):
        for ci in range(nc):
            r0 = b * rows + ci * c
            gcb = gc[r0:r0 + c]
            g_last = gcb[c - 1:c, :]
            gcs[b, ci] = gcb
            e_gcs[b, ci] = jnp.exp(gcb)
            e_rems[b, ci] = jnp.exp(g_last - gcb)
            e_lasts[b, ci] = jnp.exp(g_last)

    qn, kn, knb, kb, vb, decay = {}, {}, {}, {}, {}, {}
    for p in probs:
        b, ci, h = p
        r0 = b * rows + ci * c
        act = acts[b][ci * c:(ci + 1) * c]
        q = act[:, h * hd:(h + 1) * hd]
        k = act[:, DN_WIDTH + h * hd:DN_WIDTH + (h + 1) * hd]
        v = act[:, 2 * DN_WIDTH + h * hd:2 * DN_WIDTH + (h + 1) * hd]
        qn[p] = q * (lax.rsqrt(jnp.sum(q * q, axis=-1, keepdims=True) + EPS) * (hd ** -0.5))
        kn[p] = k * lax.rsqrt(jnp.sum(k * k, axis=-1, keepdims=True) + EPS)
        beta = beta_all[r0:r0 + c, DN_HEADS + h:DN_HEADS + h + 1]
        diff = gcs[b, ci][:, h:h + 1] - gct[h:h + 1, r0:r0 + c]
        decay[p] = jnp.where(causal, jnp.exp(jnp.where(causal, diff, 0.0)), 0.0)
        kb[p] = kn[p] * beta
        vb[p] = v * beta
        knb[p] = kn[p].astype(BF16)

    kk = {p: _dot_nt(kb[p].astype(BF16), knb[p]) for p in probs}
    qk = {p: _dot_nt(qn[p].astype(BF16), knb[p]) for p in probs}
    tinv = dict(zip(probs, _unit_lower_inverses([jnp.where(strict, kk[p] * decay[p], 0.0) for p in probs])))
    rhs = {p: jnp.concatenate([vb[p], kb[p] * e_gcs[p[0], p[1]][:, p[2]:p[2] + 1]], axis=1) for p in probs}
    sol = {p: _dot(tinv[p].astype(BF16), rhs[p].astype(BF16)) for p in probs}
    lhs = {p: jnp.concatenate([sol[p][:, hd:], qn[p] * e_gcs[p[0], p[1]][:, p[2]:p[2] + 1]], axis=0).astype(BF16)
           for p in probs}
    qkm = {p: jnp.where(causal, qk[p] * decay[p], 0.0).astype(BF16) for p in probs}
    kd = {p: (kn[p] * e_rems[p[0], p[1]][:, p[2]:p[2] + 1]).astype(BF16) for p in probs}

    state = {bh: state_ref[bh[0] * DN_HEADS + bh[1]] for bh in heads}
    for ci in range(nc):
        cur = [(b, ci, h) for (b, h) in heads]
        ws = {p: _dot(lhs[p], state[p[0], p[2]].astype(BF16)) for p in cur}
        vnb = {p: (sol[p][:, :hd] - ws[p][:c]).astype(BF16) for p in cur}
        o = {p: ws[p][c:] + _dot(qkm[p], vnb[p]) for p in cur}
        upd = {p: _dot_tn(kd[p], vnb[p]) for p in cur}
        for p in cur:
            b, _, h = p
            state[b, h] = state[b, h] * e_lasts[b, ci][:, h:h + 1] + upd[p]
            zh = z_ref[b, ci * c:(ci + 1) * c, h * hd:(h + 1) * hd]
            y_ref[b, ci * c:(ci + 1) * c, h * hd:(h + 1) * hd] = (_rms(o[p]) * nw_ref[...] * _silu(zh)).astype(BF16)
    for bh in heads:
        state_ref[bh[0] * DN_HEADS + bh[1]] = state[bh]


def _delta_net(qkv3, z3, ab3, conv_w, alog_row, dtb_row, dn_norm_w):
    nb, s, _ = qkv3.shape
    c = DN_CHUNK * DN_CHUNKS_PER_STEP
    blk = lambda n: (0, n, 0)
    return pl.pallas_call(
        _dn_kernel,
        grid=(s // c,),
        in_specs=[
            pl.BlockSpec((nb, c, 3 * DN_WIDTH), blk),
            pl.BlockSpec((nb, c, DN_WIDTH), blk),
            pl.BlockSpec((nb, c, LANES), blk),
            _const_spec(conv_w.shape),
            _const_spec((1, LANES)),
            _const_spec((1, LANES)),
            _const_spec((1, DN_HEAD_DIM)),
        ],
        out_specs=pl.BlockSpec((nb, c, DN_WIDTH), blk),
        out_shape=jax.ShapeDtypeStruct((nb, s, DN_WIDTH), BF16),
        scratch_shapes=[
            pltpu.VMEM((nb, SUBLANES, 3 * DN_WIDTH), F32),
            pltpu.VMEM((nb * DN_HEADS, DN_HEAD_DIM, DN_HEAD_DIM), F32),
        ],
        compiler_params=_params("arbitrary"),
        name="delta_net",
    )(qkv3, z3, ab3, conv_w, alog_row, dtb_row, dn_norm_w)


def _mix_kernel(x_ref, ydn_ref, ysg_ref, mod_ref, pn_ref, postn_ref, fpn_ref,
                wbg_ref, bbg_ref, wpd_ref, wps_ref, wout_ref, wrh_ref, wrl_ref,
                x1_ref, hf_ref, sct_ref):
    d = x_ref.shape[1]
    x = x_ref[...]
    mod = mod_ref[0]
    hm = _rms(x) * pn_ref[...] * (1.0 + mod[1:2, :]) + mod[0:1, :]
    gates = _sigmoid(_dot(hm.astype(BF16), wbg_ref[...]) + bbg_ref[...])
    merged = gates[:, :d] * _dot(ydn_ref[...], wpd_ref[...]) + gates[:, d:] * _dot(ysg_ref[...], wps_ref[...])
    y = _dot(merged.astype(BF16), wout_ref[...])
    x1 = x + mod[2:3, :] * (_rms(y) * postn_ref[...])
    x1_ref[...] = x1
    hf = _rms(x1) * fpn_ref[...] * (1.0 + mod[4:5, :]) + mod[3:4, :]
    for j in range(ROW_TILE):
        hf_ref[pl.ds(j, hf.shape[0], stride=ROW_TILE), :] = hf[:, j * LANES:(j + 1) * LANES]
    hh, hl = _split2(hf)
    logits_t = _dot_nt(wrh_ref[...], hh) + (_dot_nt(wrh_ref[...], hl) + _dot_nt(wrl_ref[...], hh))
    sct_ref[...] = _sigmoid(logits_t)


def _mix(x2, ydn, ysg, mod3, pre_norm, post_norm, ffn_pre_norm, wbg, bbg, wpd, wps, wout, wrh, wrl, seq, tm):
    t, d = x2.shape
    tiles_per_batch = seq // tm
    tok = lambda i: (i, 0)
    consts = [pre_norm, post_norm, ffn_pre_norm, wbg, bbg, wpd, wps, wout, wrh, wrl]
    return pl.pallas_call(
        _mix_kernel,
        grid=(t // tm,),
        in_specs=[
            pl.BlockSpec((tm, d), tok),
            pl.BlockSpec((tm, DN_WIDTH), tok),
            pl.BlockSpec((tm, SG_WIDTH), tok),
            pl.BlockSpec((1, 6, d), lambda i: (i // tiles_per_batch, 0, 0)),
        ] + [_weight_spec(a.shape) for a in consts],
        out_specs=[
            pl.BlockSpec((tm, d), tok),
            pl.BlockSpec((tm * ROW_TILE, LANES), tok),
            pl.BlockSpec((N_EXPERTS, tm), lambda i: (0, i)),
        ],
        out_shape=[
            jax.ShapeDtypeStruct((t, d), F32),
            jax.ShapeDtypeStruct((t * ROW_TILE, LANES), F32),
            jax.ShapeDtypeStruct((N_EXPERTS, t), F32),
        ],
        compiler_params=_params("arbitrary"),
        name="mix",
    )(x2, ydn, ysg, mod3, *consts)


def _first_argmax(vals, idx):
    m = jnp.max(vals, axis=0, keepdims=True)
    first = jnp.min(jnp.where(vals == m, idx, jnp.int32(2 ** 30)), axis=0, keepdims=True)
    return m, first


def _route_kernel(sct_ref, bias_ref, idx_ref, wtt_ref, rank_ref, cnt_ref, carry_ref):
    tm = sct_ref.shape[1]

    @pl.when(pl.program_id(0) == 0)
    def _():
        carry_ref[...] = jnp.zeros_like(carry_ref)

    scores = sct_ref[...]
    sel = scores + bias_ref[...]
    erow = lax.broadcasted_iota(I32, (N_EXPERTS, tm), 0)
    grow = lax.broadcasted_iota(I32, (GROUP_SIZE, tm), 0)

    gs = []
    for gidx in range(N_EXPERT_GROUPS):
        sg = sel[gidx * GROUP_SIZE:(gidx + 1) * GROUP_SIZE]
        m1, first = _first_argmax(sg, grow)
        m2 = jnp.max(jnp.where(grow == first, NEG_INF, sg), axis=0, keepdims=True)
        gs.append(m1 + m2)
    gsc = jnp.concatenate(gs, axis=0)
    giota = lax.broadcasted_iota(I32, (N_EXPERT_GROUPS, tm), 0)
    gmask = jnp.zeros((N_EXPERT_GROUPS, tm), F32)
    cur = gsc
    for _ in range(TOPK_GROUPS):
        _, gi = _first_argmax(cur, giota)
        pick = giota == gi
        gmask = jnp.where(pick, 1.0, gmask)
        cur = jnp.where(pick, NEG_INF, cur)
    masked = jnp.concatenate(
        [jnp.where(gmask[gidx:gidx + 1, :] > 0.5, sel[gidx * GROUP_SIZE:(gidx + 1) * GROUP_SIZE], NEG_INF)
         for gidx in range(N_EXPERT_GROUPS)], axis=0)

    cur = masked
    idxs, wts = [], []
    onehot = jnp.zeros((N_EXPERTS, tm), F32)
    for _ in range(TOP_K):
        _, ei = _first_argmax(cur, erow)
        pick = erow == ei
        idxs.append(ei)
        wts.append(jnp.sum(jnp.where(pick, scores, 0.0), axis=0, keepdims=True))
        onehot = jnp.where(pick, 1.0, onehot)
        cur = jnp.where(pick, NEG_INF, cur)
    idx = jnp.concatenate(idxs, axis=0)
    wt = jnp.concatenate(wts, axis=0)
    wt = wt / jnp.sum(wt, axis=0, keepdims=True) * ROUTED_SCALE
    idx_ref[...] = idx
    wpad = jnp.concatenate([wt, jnp.zeros((LANES - TOP_K, tm), F32)], axis=0)
    wtt_ref[...] = wpad.T

    ti = lax.broadcasted_iota(I32, (tm, tm), 0)
    tj = lax.broadcasted_iota(I32, (tm, tm), 1)
    upper = (ti < tj).astype(BF16)
    before = _dot(onehot.astype(BF16), upper) + carry_ref[...]
    rank_ref[...] = jnp.concatenate(
        [jnp.sum(jnp.where(erow == idxs[kk], before, 0.0), axis=0, keepdims=True) for kk in range(TOP_K)],
        axis=0).astype(I32)
    total = carry_ref[...] + jnp.sum(onehot, axis=1, keepdims=True)
    carry_ref[...] = total
    cnt_ref[...] = total


def _route(sct, bias_col, tm):
    e, t = sct.shape
    tile = lambda i: (0, i)
    return pl.pallas_call(
        _route_kernel,
        grid=(t // tm,),
        in_specs=[pl.BlockSpec((e, tm), tile), _const_spec((e, 1))],
        out_specs=[
            pl.BlockSpec((TOP_K, tm), tile),
            pl.BlockSpec((tm, LANES), lambda i: (i, 0)),
            pl.BlockSpec((TOP_K, tm), tile),
            _const_spec((e, 1)),
        ],
        out_shape=[
            jax.ShapeDtypeStruct((TOP_K, t), I32),
            jax.ShapeDtypeStruct((t, LANES), F32),
            jax.ShapeDtypeStruct((TOP_K, t), I32),
            jax.ShapeDtypeStruct((e, 1), F32),
        ],
        scratch_shapes=[pltpu.VMEM((e, 1), F32)],
        compiler_params=_params("arbitrary"),
        name="route",
    )(sct, bias_col)


def _dest_kernel(cnt_ref, idx_ref, rank_ref, dest_ref, meta_ref, pstart_ref):
    tm = idx_ref.shape[1]
    e = N_EXPERTS

    @pl.when(pl.program_id(0) == 0)
    def _():
        cnt = cnt_ref[...]
        padded = jnp.floor((cnt + (MOE_BLOCK - 1)) * (1.0 / MOE_BLOCK)) * MOE_BLOCK
        pw = jnp.broadcast_to(padded, (e, LANES))
        ri = lax.broadcasted_iota(I32, (e, e), 0)
        ci = lax.broadcasted_iota(I32, (e, e), 1)
        lower = (ri >= ci).astype(BF16)
        ph, pm, plo = _split3(pw)
        pends = _dot(lower, ph) + (_dot(lower, pm) + _dot(lower, plo))
        pstart = pends - pw
        pstart_ref[...] = pstart[:, 0:1]
        lane = lax.broadcasted_iota(I32, (e, LANES), 1)
        blocks = jnp.where(lane == 0, pstart, pw) * (1.0 / MOE_BLOCK)
        pads = jnp.where(lane == 2, pstart + cnt, pw - cnt)
        meta_ref[...] = jnp.where(lane < 2, blocks, pads).astype(I32)

    pstart_col = pstart_ref[...]
    erow = lax.broadcasted_iota(I32, (e, tm), 0)
    idx = idx_ref[...]
    rows = [jnp.sum(jnp.where(erow == idx[kk:kk + 1, :], pstart_col, 0.0), axis=0, keepdims=True)
            for kk in range(TOP_K)]
    dest_ref[...] = jnp.concatenate(rows, axis=0).astype(I32) + rank_ref[...]


def _dest(cnt, idx, rank, tm):
    e = cnt.shape[0]
    t = idx.shape[1]
    tile = lambda i: (0, i)
    return pl.pallas_call(
        _dest_kernel,
        grid=(t // tm,),
        in_specs=[_const_spec((e, 1)), pl.BlockSpec((TOP_K, tm), tile), pl.BlockSpec((TOP_K, tm), tile)],
        out_specs=[pl.BlockSpec((TOP_K, tm), tile), _const_spec((e, LANES))],
        out_shape=[jax.ShapeDtypeStruct((TOP_K, t), I32), jax.ShapeDtypeStruct((e, LANES), I32)],
        scratch_shapes=[pltpu.VMEM((e, 1), F32)],
        compiler_params=_params("arbitrary"),
        name="dest",
    )(cnt, idx, rank)


def _row_copy(src, dst, sem):
    return pltpu.make_async_copy(src, dst, sem)


def _rows(first, count=1):
    return pl.ds(pl.multiple_of(first * ROW_TILE, ROW_TILE), count * ROW_TILE)


PAD_PIECES = tuple(MOE_BLOCK >> s for s in range(1, MOE_BLOCK.bit_length()))


def _zero_unassigned_rows(pad_start_ref, pad_rows_ref, xs_out, zero_ref, sem):
    zero_ref[...] = jnp.zeros_like(zero_ref)
    ne = pad_start_ref.shape[0]
    total = xs_out.shape[0] // (MOE_BLOCK * ROW_TILE)

    def piece(rows, pos):
        return pltpu.make_async_copy(zero_ref.at[_rows(0, rows)], xs_out.at[_rows(pos, rows)], sem)

    def pads(wait):
        def per_expert(ex, carry):
            pos = pad_start_ref[ex]
            pad = pad_rows_ref[ex]
            for rows in PAD_PIECES:
                @pl.when((pad & rows) != 0)
                def _():
                    piece(rows, pos).wait() if wait else piece(rows, pos).start()
                pos = pos + (pad & rows)
            return carry
        lax.fori_loop(0, ne, per_expert, 0)

    def tail(wait):
        used = (pad_start_ref[ne - 1] + pad_rows_ref[ne - 1]) // MOE_BLOCK

        def per_block(blk, carry):
            cp = piece(MOE_BLOCK, blk * MOE_BLOCK)
            cp.wait() if wait else cp.start()
            return carry
        lax.fori_loop(used, total, per_block, 0)

    pads(False)
    tail(False)
    pads(True)
    tail(True)


def _scatter_kernel(pad_start_ref, pad_rows_ref, dest_hbm, hf_ref, wsg_ref, wsu_ref, wsd_ref, xs_out, sh_ref,
                    dest_smem, zero_ref, sem_idx, sem_rows, sem_zero):
    tm = hf_ref.shape[0] // ROW_TILE
    i = pl.program_id(0)

    @pl.when(i == 0)
    def _():
        _zero_unassigned_rows(pad_start_ref, pad_rows_ref, xs_out, zero_ref, sem_zero)

    idx_copy = pltpu.make_async_copy(dest_hbm.at[i], dest_smem, sem_idx)
    idx_copy.start()
    idx_copy.wait()

    def issue(tok, carry):
        for kk in range(TOP_K):
            slot = dest_smem[kk * tm + tok]
            _row_copy(hf_ref.at[_rows(tok)], xs_out.at[_rows(slot)], sem_rows).start(priority=kk % 2)
        return carry

    lax.fori_loop(0, tm, issue, 0)

    hb = jnp.concatenate([hf_ref[pl.ds(j, tm, stride=ROW_TILE), :] for j in range(ROW_TILE)], axis=1).astype(BF16)
    hid = _silu(_dot(hb, wsg_ref[...])) * _dot(hb, wsu_ref[...])
    sh_ref[...] = _dot(hid.astype(BF16), wsd_ref[...])

    for kk in range(TOP_K):
        _row_copy(hf_ref, xs_out.at[_rows(0, tm)], sem_rows).wait()


def _scatter(pad_start, pad_rows, dest_tiles, hf, wsg, wsu, wsd, n_slots, tm):
    t = hf.shape[0] // ROW_TILE
    d = wsg.shape[0]
    const = lambda a: pl.BlockSpec(a.shape, lambda i, ps, pr: (0, 0))
    n_tiles = t // tm
    return pl.pallas_call(
        _scatter_kernel,
        grid_spec=pltpu.PrefetchScalarGridSpec(
            num_scalar_prefetch=2,
            grid=(n_tiles,),
            in_specs=[pl.BlockSpec(memory_space=pl.ANY), pl.BlockSpec((tm * ROW_TILE, LANES), lambda i, ps, pr: (i, 0)),
                      const(wsg), const(wsu), const(wsd)],
            out_specs=[pl.BlockSpec(memory_space=pl.ANY), pl.BlockSpec((tm, d), lambda i, ps, pr: (i, 0))],
            scratch_shapes=[
                pltpu.SMEM((TOP_K * tm,), I32),
                pltpu.VMEM((MOE_BLOCK * ROW_TILE, LANES), hf.dtype),
                pltpu.SemaphoreType.DMA,
                pltpu.SemaphoreType.DMA,
                pltpu.SemaphoreType.DMA,
            ],
        ),
        out_shape=[jax.ShapeDtypeStruct((n_slots * ROW_TILE, LANES), hf.dtype), jax.ShapeDtypeStruct((t, d), F32)],
        compiler_params=_params("arbitrary"),
        name="scatter",
    )(pad_start, pad_rows, dest_tiles, hf, wsg, wsu, wsd)


MOE_BUFFERS = 8


def _moe_kernel(bstart_ref, nblk_ref, xs_hbm, wg_ref, wu_ref, wd_ref, ys_hbm,
                xbuf, ybuf, wgb_ref, wub_ref, wdb_ref, sem_in, sem_out):
    ex = pl.program_id(0)
    last = pl.num_programs(0) - 1
    n = nblk_ref[ex]
    b0 = bstart_ref[ex]
    used = bstart_ref[last] + nblk_ref[last]

    def in_copy(g):
        slot = lax.rem(g, MOE_BUFFERS)
        return pltpu.make_async_copy(xs_hbm.at[_rows(g * MOE_BLOCK, MOE_BLOCK)], xbuf.at[slot], sem_in.at[slot])

    def out_copy(g):
        slot = lax.rem(g, MOE_BUFFERS)
        return pltpu.make_async_copy(ybuf.at[slot], ys_hbm.at[_rows(g * MOE_BLOCK, MOE_BLOCK)], sem_out.at[slot])

    @pl.when(ex == 0)
    def _():
        for s in range(MOE_BUFFERS - 1):
            @pl.when(s < used)
            def _():
                in_copy(s).start()

    wgb_ref[...] = wg_ref[0].astype(BF16)
    wub_ref[...] = wu_ref[0].astype(BF16)
    wdb_ref[...] = wd_ref[0].astype(BF16)

    def body(g, carry):
        slot = lax.rem(g, MOE_BUFFERS)
        in_copy(g).wait()

        @pl.when(g + (MOE_BUFFERS - 1) < used)
        def _():
            in_copy(g + (MOE_BUFFERS - 1)).start()

        @pl.when(g >= MOE_BUFFERS)
        def _():
            out_copy(g - MOE_BUFFERS).wait()

        xb = jnp.concatenate([xbuf[slot, pl.ds(j, MOE_BLOCK, stride=ROW_TILE), :] for j in range(ROW_TILE)],
                             axis=1).astype(BF16)
        hid = _silu(_dot(xb, wgb_ref[...])) * _dot(xb, wub_ref[...])
        y = _dot(hid.astype(BF16), wdb_ref[...])
        for j in range(ROW_TILE):
            ybuf[slot, pl.ds(j, MOE_BLOCK, stride=ROW_TILE), :] = y[:, j * LANES:(j + 1) * LANES]
        out_copy(g).start()
        return carry

    lax.fori_loop(b0, b0 + n, body, 0)

    @pl.when(ex == last)
    def _():
        for s in range(MOE_BUFFERS):
            @pl.when(s < used)
            def _():
                out_copy(used - 1 - s).wait()

        total = ys_hbm.shape[0] // (MOE_BLOCK * ROW_TILE)
        ybuf[0] = jnp.zeros(ybuf.shape[1:], ybuf.dtype)

        def tail_copy(blk):
            return pltpu.make_async_copy(ybuf.at[0], ys_hbm.at[_rows(blk * MOE_BLOCK, MOE_BLOCK)], sem_out.at[0])

        def start_tail(blk, carry):
            tail_copy(blk).start()
            return carry

        def wait_tail(blk, carry):
            tail_copy(blk).wait()
            return carry

        lax.fori_loop(used, total, start_tail, 0)
        lax.fori_loop(used, total, wait_tail, 0)


def _moe(bstart, nblk_e, xs, w_gate, w_up, w_down):
    ne, d, eh = w_gate.shape
    wspec = lambda shape: pl.BlockSpec(shape, lambda ex, bs, nb: (ex, 0, 0))
    return pl.pallas_call(
        _moe_kernel,
        grid_spec=pltpu.PrefetchScalarGridSpec(
            num_scalar_prefetch=2,
            grid=(ne,),
            in_specs=[pl.BlockSpec(memory_space=pl.ANY), wspec((1, d, eh)), wspec((1, d, eh)), wspec((1, eh, d))],
            out_specs=pl.BlockSpec(memory_space=pl.ANY),
            scratch_shapes=[
                pltpu.VMEM((MOE_BUFFERS, MOE_BLOCK * ROW_TILE, LANES), F32),
                pltpu.VMEM((MOE_BUFFERS, MOE_BLOCK * ROW_TILE, LANES), F32),
                pltpu.VMEM((d, eh), BF16), pltpu.VMEM((d, eh), BF16), pltpu.VMEM((eh, d), BF16),
                pltpu.SemaphoreType.DMA((MOE_BUFFERS,)),
                pltpu.SemaphoreType.DMA((MOE_BUFFERS,)),
            ],
        ),
        out_shape=jax.ShapeDtypeStruct(xs.shape, F32),
        compiler_params=_params("arbitrary"),
        name="moe",
    )(bstart, nblk_e, xs, w_gate, w_up, w_down)


def _combine_kernel(dest_hbm, ys_hbm, wtt_ref, x1_ref, sh_ref, mod_ref, postn_ref, o_ref, dest_smem0, dest_smem1, buf_ref, sem_idx, sem_rows):
    tc = x1_ref.shape[0]
    i = pl.program_id(0)
    n = pl.num_programs(0)
    dest_smem = (dest_smem0, dest_smem1)

    def idx_copy(tile, slot):
        return pltpu.make_async_copy(dest_hbm.at[tile], dest_smem[slot], sem_idx.at[slot])

    def issue_rows(slot):
        def issue(tok, carry):
            for kk in range(TOP_K):
                row = dest_smem[slot][kk * tc + tok]
                _row_copy(ys_hbm.at[_rows(row)], buf_ref.at[slot, kk, _rows(tok)],
                          sem_rows.at[slot]).start(priority=kk % 2)
            return carry
        lax.fori_loop(0, tc, issue, 0)

    @pl.when(i == 0)
    def _():
        idx_copy(0, 0).start()
        idx_copy(0, 0).wait()

        @pl.when(n > 1)
        def _():
            idx_copy(1, 1).start()
        issue_rows(0)

    def step(cur):
        nxt = 1 - cur

        @pl.when(i + 1 < n)
        def _():
            idx_copy(i + 1, nxt).wait()

            @pl.when(i + 2 < n)
            def _():
                idx_copy(i + 2, cur).start()
            issue_rows(nxt)

        for kk in range(TOP_K):
            _row_copy(ys_hbm.at[_rows(0, tc)], buf_ref.at[cur, kk], sem_rows.at[cur]).wait()

        wtt = wtt_ref[...]
        wk = [jnp.broadcast_to(wtt[:, kk:kk + 1], (tc, LANES)) for kk in range(TOP_K)]
        cols = []
        for j in range(ROW_TILE):
            lane_group = pl.ds(j, tc, stride=ROW_TILE)
            acc = buf_ref[cur, 0, lane_group, :] * wk[0]
            for kk in range(1, TOP_K):
                acc = acc + buf_ref[cur, kk, lane_group, :] * wk[kk]
            cols.append(acc)
        y = jnp.concatenate(cols, axis=1) + sh_ref[...]
        mod = mod_ref[0]
        o_ref[...] = x1_ref[...] + mod[5:6, :] * (_rms(y) * postn_ref[...])

    parity = lax.rem(i, 2)
    for cur in range(2):
        @pl.when(parity == cur)
        def _():
            step(cur)


def _combine(dest_tiles, ys, wtt, x1, shared, mod3, ffn_post_norm, seq, tc):
    t, d = x1.shape
    tiles_per_batch = seq // tc
    tok = lambda i: (i, 0)
    return pl.pallas_call(
        _combine_kernel,
        grid=(t // tc,),
        in_specs=[
            pl.BlockSpec(memory_space=pl.ANY),
            pl.BlockSpec(memory_space=pl.ANY),
            pl.BlockSpec((tc, LANES), tok),
            pl.BlockSpec((tc, d), tok),
            pl.BlockSpec((tc, d), tok),
            pl.BlockSpec((1, 6, d), lambda i: (i // tiles_per_batch, 0, 0)),
            _const_spec((1, d)),
        ],
        out_specs=pl.BlockSpec((tc, d), tok),
        out_shape=jax.ShapeDtypeStruct((t, d), F32),
        scratch_shapes=[
            pltpu.SMEM((TOP_K * tc,), I32),
            pltpu.SMEM((TOP_K * tc,), I32),
            pltpu.VMEM((2, TOP_K, tc * ROW_TILE, LANES), F32),
            pltpu.SemaphoreType.DMA((2,)),
            pltpu.SemaphoreType.DMA((2,)),
        ],
        compiler_params=_params("arbitrary"),
        name="combine",
    )(dest_tiles, ys, wtt, x1, shared, mod3, ffn_post_norm)


def _tile(n, want):
    t = min(n, want)
    assert n % t == 0
    return t


def _dest_tiles(dest, tile):
    k, t = dest.shape
    return dest.reshape(k, t // tile, tile).transpose(1, 0, 2).reshape(t // tile, k * tile)


def kernel(x, c, ada_w, ada_b, mix_pre_norm, mix_post_norm, w_in, conv_w, a_log, dt_bias, dn_norm_w, sg_ln_w, sg_ln_b, sg_w, sg_b, w_branch_gate, b_branch_gate, w_proj_dn, w_proj_sg, w_out, ffn_pre_norm, ffn_post_norm, w_router, router_bias, w_exp_gate, w_exp_up, w_exp_down, w_sh_gate, w_sh_up, w_sh_down):
    nb, seq, d = x.shape
    depth = ada_w.shape[0]
    t = nb * seq
    tm = _tile(seq, DENSE_TILE)
    tr = _tile(t, ROUTE_TILE)
    tsc = _tile(t, SCATTER_TILE)
    tcm = _tile(seq, COMBINE_TILE)
    nblk = -(-t * TOP_K // MOE_BLOCK) + N_EXPERTS
    row = lambda v: v.reshape(1, -1)
    pad_lanes = lambda v: jnp.pad(v.astype(F32), (0, LANES - v.shape[0])).reshape(1, LANES)

    x2 = x.reshape(t, d)
    for l in range(depth):
        mod3 = _ada(c, ada_w[l], ada_b[l]).reshape(nb, 6, d)

        wi = w_in[l]
        qkvz, ab_cols, uv = wi[:, :4 * DN_WIDTH], wi[:, 4 * DN_WIDTH:4 * DN_WIDTH + 2 * DN_HEADS], wi[:, 4 * DN_WIDTH + 2 * DN_HEADS:]
        w1 = jnp.concatenate([qkvz, ab_cols, jnp.zeros((d, LANES - 2 * DN_HEADS), wi.dtype)], axis=1).astype(BF16)
        qkv, z, ab, ysg = _in_proj(x2, mod3, row(mix_pre_norm[l]), w1, uv.astype(BF16), row(sg_ln_w[l]), row(sg_ln_b[l]),
                                   sg_w[l], sg_b[l].T, seq, tm)

        ydn = _delta_net(qkv.reshape(nb, seq, -1), z.reshape(nb, seq, -1), ab.reshape(nb, seq, -1), conv_w[l],
                         pad_lanes(a_log[l]), pad_lanes(dt_bias[l]), row(dn_norm_w[l]))

        wr_t = w_router[l].T
        wrh = wr_t.astype(BF16)
        wrl = (wr_t - wrh.astype(F32)).astype(BF16)
        x1, hf, sct = _mix(
            x2, ydn.reshape(t, -1), ysg, mod3, row(mix_pre_norm[l]), row(mix_post_norm[l]), row(ffn_pre_norm[l]),
            w_branch_gate[l].astype(BF16), row(b_branch_gate[l]), w_proj_dn[l].astype(BF16), w_proj_sg[l].astype(BF16),
            w_out[l].astype(BF16), wrh, wrl, seq, tm)

        idx, wtt, rank, cnt = _route(sct, router_bias[l].reshape(-1, 1), tr)
        dest, meta = _dest(cnt, idx, rank, tr)
        xs, shared = _scatter(meta[:, 2], meta[:, 3], _dest_tiles(dest, tsc), hf, w_sh_gate[l].astype(BF16),
                              w_sh_up[l].astype(BF16), w_sh_down[l].astype(BF16), nblk * MOE_BLOCK, tsc)
        ys = _moe(meta[:, 0], meta[:, 1], xs, w_exp_gate[l], w_exp_up[l], w_exp_down[l])
        x2 = _combine(_dest_tiles(dest, tcm), ys, wtt, x1, shared, mod3, row(ffn_post_norm[l]), seq, tcm)
    return x2.reshape(nb, seq, d)
```

```python
import jax
import jax.numpy as jnp
from jax import lax
from jax.experimental import pallas as pl
from jax.experimental.pallas import tpu as pltpu

F32 = jnp.float32
BF16 = jnp.bfloat16
I32 = jnp.int32

D_MODEL = 1024
DN_HEADS = 4
DN_HEAD_DIM = 128
DN_WIDTH = DN_HEADS * DN_HEAD_DIM
DN_CHUNK = 64
DN_CHUNKS_PER_STEP = 4
CONV_WIDTH = 4
SG_GROUPS = 4
SG_GROUP_DIM = 128
SG_WIDTH = SG_GROUPS * SG_GROUP_DIM
SG_CHUNK = 128
N_EXPERTS = 256
N_EXPERT_GROUPS = 8
GROUP_SIZE = N_EXPERTS // N_EXPERT_GROUPS
TOPK_GROUPS = 4
TOP_K = 8
ROUTED_SCALE = 2.5
MOE_BLOCK = 256
EPS = 1e-6

LANES = 128
SUBLANES = 8
DENSE_TILE = 1024
ROUTE_TILE = 512
SCATTER_TILE = 512
COMBINE_TILE = 256
ROW_TILE = SUBLANES
VMEM_LIMIT = 56 * 1024 * 1024
NEG_INF = float("-inf")


def _dot(a, b):
    return jnp.dot(a, b, preferred_element_type=F32)


def _dot_nt(a, b):
    return lax.dot_general(a, b, (((1,), (1,)), ((), ())), preferred_element_type=F32)


def _dot_tn(a, b):
    return lax.dot_general(a, b, (((0,), (0,)), ((), ())), preferred_element_type=F32)


def _split2(x):
    hi = x.astype(BF16)
    lo = (x - hi.astype(F32)).astype(BF16)
    return hi, lo


def _split3(x):
    hi = x.astype(BF16)
    r = x - hi.astype(F32)
    mid = r.astype(BF16)
    lo = (r - mid.astype(F32)).astype(BF16)
    return hi, mid, lo


def _dot_hp(a, b, dot=_dot):
    ah, al = _split2(a)
    bh, bl = _split2(b)
    return dot(ah, bh) + (dot(ah, bl) + dot(al, bh))


def _sigmoid(x):
    return 1.0 / (1.0 + jnp.exp(-x))


def _silu(x):
    return x * _sigmoid(x)


def _gelu(x):
    return 0.5 * x * (1.0 + lax.erf(x * (2.0 ** -0.5)))


def _softplus(x):
    return jnp.maximum(x, 0.0) + jnp.log1p(jnp.exp(-jnp.abs(x)))


def _rms(x):
    return x * lax.rsqrt(jnp.mean(x * x, axis=-1, keepdims=True) + EPS)


def _params(*sem):
    return pltpu.CompilerParams(dimension_semantics=sem, vmem_limit_bytes=VMEM_LIMIT)


def _const_spec(shape):
    nd = len(shape)
    return pl.BlockSpec(shape, lambda *_: (0,) * nd)


def _weight_spec(shape):
    nd = len(shape)
    return pl.BlockSpec(shape, lambda *_: (0,) * nd, pipeline_mode=pl.Buffered(1))


def _ada_kernel(c_ref, w_ref, b_ref, o_ref):
    cs = _silu(c_ref[...])
    o_ref[...] = _dot_hp(cs, w_ref[...]) + b_ref[...]


def _ada(c, ada_w, ada_b):
    b, d = c.shape
    n = ada_w.shape[1]
    tn = d
    return pl.pallas_call(
        _ada_kernel,
        grid=(n // tn,),
        in_specs=[_const_spec((b, d)), pl.BlockSpec((d, tn), lambda j: (0, j)), pl.BlockSpec((1, tn), lambda j: (0, j))],
        out_specs=pl.BlockSpec((b, tn), lambda j: (0, j)),
        out_shape=jax.ShapeDtypeStruct((b, n), F32),
        compiler_params=_params("arbitrary"),
        name="ada",
    )(c, ada_w, ada_b.reshape(1, n))


def _in_kernel(x_ref, mod_ref, pn_ref, w1_ref, wuv_ref, lnw_ref, lnb_ref, sgw_ref, sgbt_ref,
               qkv_ref, z_ref, ab_ref, ysg_ref):
    tm = x_ref.shape[0]
    mod = mod_ref[0]
    hm = _rms(x_ref[...]) * pn_ref[...] * (1.0 + mod[1:2, :]) + mod[0:1, :]
    hb = hm.astype(BF16)
    p1 = _dot(hb, w1_ref[...])
    qkv_ref[...] = p1[:, :3 * DN_WIDTH]
    z_ref[...] = p1[:, 3 * DN_WIDTH:4 * DN_WIDTH]
    ab_ref[...] = p1[:, 4 * DN_WIDTH:]
    uv = _dot(hb, wuv_ref[...])
    u = _gelu(uv[:, :SG_WIDTH])
    vg = _gelu(uv[:, SG_WIDTH:])
    mu = jnp.mean(vg, axis=-1, keepdims=True)
    dv = vg - mu
    var = jnp.mean(dv * dv, axis=-1, keepdims=True)
    vgn = (dv * lax.rsqrt(var + EPS) * lnw_ref[...] + lnb_ref[...]).astype(BF16)
    row = lax.broadcasted_iota(I32, (SG_CHUNK, SG_CHUNK), 0)
    col = lax.broadcasted_iota(I32, (SG_CHUNK, SG_CHUNK), 1)
    tril = row >= col
    for g in range(SG_GROUPS):
        wg = jnp.where(tril, sgw_ref[g], 0.0).astype(BF16)
        bg = sgbt_ref[:, g:g + 1]
        cs = slice(g * SG_GROUP_DIM, (g + 1) * SG_GROUP_DIM)
        for n in range(tm // SG_CHUNK):
            rs = slice(n * SG_CHUNK, (n + 1) * SG_CHUNK)
            mixed = _dot(wg, vgn[rs, cs]) + bg
            ysg_ref[rs, cs] = (u[rs, cs] * mixed).astype(BF16)


def _in_proj(x2, mod3, pre_norm, w1, wuv, sg_ln_w, sg_ln_b, sg_w, sg_bt, seq, tm):
    t, d = x2.shape
    tiles_per_batch = seq // tm
    tok = lambda i: (i, 0)
    return pl.pallas_call(
        _in_kernel,
        grid=(t // tm,),
        in_specs=[
            pl.BlockSpec((tm, d), tok),
            pl.BlockSpec((1, 6, d), lambda i: (i // tiles_per_batch, 0, 0)),
            _const_spec((1, d)),
            _weight_spec(w1.shape),
            _weight_spec(wuv.shape),
            _const_spec((1, SG_WIDTH)),
            _const_spec((1, SG_WIDTH)),
            _const_spec(sg_w.shape),
            _const_spec(sg_bt.shape),
        ],
        out_specs=[
            pl.BlockSpec((tm, 3 * DN_WIDTH), tok),
            pl.BlockSpec((tm, DN_WIDTH), tok),
            pl.BlockSpec((tm, LANES), tok),
            pl.BlockSpec((tm, SG_WIDTH), tok),
        ],
        out_shape=[
            jax.ShapeDtypeStruct((t, 3 * DN_WIDTH), F32),
            jax.ShapeDtypeStruct((t, DN_WIDTH), F32),
            jax.ShapeDtypeStruct((t, LANES), F32),
            jax.ShapeDtypeStruct((t, SG_WIDTH), BF16),
        ],
        compiler_params=_params("arbitrary"),
        name="in_proj",
    )(x2, mod3, pre_norm, w1, wuv, sg_ln_w, sg_ln_b, sg_w, sg_bt)


def _unit_lower_inverses(a_list):
    c = a_list[0].shape[0]
    i = lax.broadcasted_iota(I32, (c, c), 0)
    j = lax.broadcasted_iota(I32, (c, c), 1)
    eye = (i == j).astype(F32)
    first = (i == j + 1) & ((i & 1) == 1)
    d_list = [eye - jnp.where(first, a, 0.0) for a in a_list]
    b = 2
    while b < c:
        shift = b.bit_length()
        off = ((i >> shift) == (j >> shift)) & ((i & b) != 0) & ((j & b) == 0)
        a_parts = [jnp.where(off, a, 0.0).astype(BF16) for a in a_list]
        d_parts = [d.astype(BF16) for d in d_list]
        t_list = [_dot(dp, ap) for dp, ap in zip(d_parts, a_parts)]
        d_list = [d - _dot(t.astype(BF16), dp) for d, t, dp in zip(d_list, t_list, d_parts)]
        b *= 2
    return d_list


def _dn_kernel(qkv_ref, z_ref, ab_ref, convw_ref, alog_ref, dtb_ref, nw_ref, y_ref, carry_ref, state_ref, tri_ref):
    nb, rows = qkv_ref.shape[0], qkv_ref.shape[1]
    c = DN_CHUNK
    nc = rows // c
    hd = DN_HEAD_DIM
    heads = [(b, h) for b in range(nb) for h in range(DN_HEADS)]
    probs = [(b, ci, h) for b in range(nb) for ci in range(nc) for h in range(DN_HEADS)]

    @pl.when(pl.program_id(0) == 0)
    def _():
        carry_ref[...] = jnp.zeros_like(carry_ref)
        state_ref[...] = jnp.zeros_like(state_ref)
        ri = lax.broadcasted_iota(I32, tri_ref.shape, 0)
        ci_ = lax.broadcasted_iota(I32, tri_ref.shape, 1)
        tri_ref[...] = ((ri >= ci_) & ((ri // c) == (ci_ // c))).astype(BF16)

    ab = ab_ref[...].reshape(nb * rows, LANES)
    g = -jnp.exp(alog_ref[...]) * _softplus(ab + dtb_ref[...])
    beta_all = _sigmoid(ab)
    tri = tri_ref[...]
    gh, gm, gl = _split3(g)
    gc = _dot(tri, gh) + (_dot(tri, gm) + _dot(tri, gl))
    gct = gc.T

    i = lax.broadcasted_iota(I32, (c, c), 0)
    j = lax.broadcasted_iota(I32, (c, c), 1)
    causal = i >= j
    strict = i > j
    row8 = lax.broadcasted_iota(I32, (SUBLANES, 3 * DN_WIDTH), 0)

    acts = []
    for b in range(nb):
        xc = qkv_ref[b]
        prev = carry_ref[b]
        acc = xc * convw_ref[CONV_WIDTH - 1:CONV_WIDTH, :]
        for s in range(1, CONV_WIDTH):
            rolled = pltpu.roll(xc, s, axis=0)
            top = jnp.where(row8 < s, pltpu.roll(prev, s, axis=0), rolled[:SUBLANES])
            shifted = jnp.concatenate([top, rolled[SUBLANES:]], axis=0)
            acc = acc + shifted * convw_ref[CONV_WIDTH - 1 - s:CONV_WIDTH - s, :]
        carry_ref[b] = xc[rows - SUBLANES:]
        acts.append(_silu(acc))

    gcs, e_gcs, e_rems, e_lasts = {}, {}, {}, {}
    for b in range(nb):
        for ci in range(nc):
            r0 = b * rows + ci * c
            gcb = gc[r0:r0 + c]
            g_last = gcb[c - 1:c, :]
            gcs[b, ci] = gcb
            e_gcs[b, ci] = jnp.exp(gcb)
            e_rems[b, ci] = jnp.exp(g_last - gcb)
            e_lasts[b, ci] = jnp.exp(g_last)

    qn, kn, knb, kb, vb, decay = {}, {}, {}, {}, {}, {}
    for p in probs:
        b, ci, h = p
        r0 = b * rows + ci * c
        act = acts[b][ci * c:(ci + 1) * c]
        q = act[:, h * hd:(h + 1) * hd]
        k = act[:, DN_WIDTH + h * hd:DN_WIDTH + (h + 1) * hd]
        v = act[:, 2 * DN_WIDTH + h * hd:2 * DN_WIDTH + (h + 1) * hd]
        qn[p] = q * (lax.rsqrt(jnp.sum(q * q, axis=-1, keepdims=True) + EPS) * (hd ** -0.5))
        kn[p] = k * lax.rsqrt(jnp.sum(k * k, axis=-1, keepdims=True) + EPS)
        beta = beta_all[r0:r0 + c, DN_HEADS + h:DN_HEADS + h + 1]
        diff = gcs[b, ci][:, h:h + 1] - gct[h:h + 1, r0:r0 + c]
        decay[p] = jnp.where(causal, jnp.exp(jnp.where(causal, diff, 0.0)), 0.0)
        kb[p] = kn[p] * beta
        vb[p] = v * beta
        knb[p] = kn[p].astype(BF16)

    kk = {p: _dot_nt(kb[p].astype(BF16), knb[p]) for p in probs}
    qk = {p: _dot_nt(qn[p].astype(BF16), knb[p]) for p in probs}
    tinv = dict(zip(probs, _unit_lower_inverses([jnp.where(strict, kk[p] * decay[p], 0.0) for p in probs])))
    rhs = {p: jnp.concatenate([vb[p], kb[p] * e_gcs[p[0], p[1]][:, p[2]:p[2] + 1]], axis=1) for p in probs}
    sol = {p: _dot(tinv[p].astype(BF16), rhs[p].astype(BF16)) for p in probs}
    lhs = {p: jnp.concatenate([sol[p][:, hd:], qn[p] * e_gcs[p[0], p[1]][:, p[2]:p[2] + 1]], axis=0).astype(BF16)
           for p in probs}
    qkm = {p: jnp.where(causal, qk[p] * decay[p], 0.0).astype(BF16) for p in probs}
    kd = {p: (kn[p] * e_rems[p[0], p[1]][:, p[2]:p[2] + 1]).astype(BF16) for p in probs}

    state = {bh: state_ref[bh[0] * DN_HEADS + bh[1]] for bh in heads}
    for ci in range(nc):
        cur = [(b, ci, h) for (b, h) in heads]
        ws = {p: _dot(lhs[p], state[p[0], p[2]].astype(BF16)) for p in cur}
        vnb = {p: (sol[p][:, :hd] - ws[p][:c]).astype(BF16) for p in cur}
        o = {p: ws[p][c:] + _dot(qkm[p], vnb[p]) for p in cur}
        upd = {p: _dot_tn(kd[p], vnb[p]) for p in cur}
        for p in cur:
            b, _, h = p
            state[b, h] = state[b, h] * e_lasts[b, ci][:, h:h + 1] + upd[p]
            zh = z_ref[b, ci * c:(ci + 1) * c, h * hd:(h + 1) * hd]
            y_ref[b, ci * c:(ci + 1) * c, h * hd:(h + 1) * hd] = (_rms(o[p]) * nw_ref[...] * _silu(zh)).astype(BF16)
    for bh in heads:
        state_ref[bh[0] * DN_HEADS + bh[1]] = state[bh]


def _delta_net(qkv3, z3, ab3, conv_w, alog_row, dtb_row, dn_norm_w):
    nb, s, _ = qkv3.shape
    c = DN_CHUNK * DN_CHUNKS_PER_STEP
    blk = lambda n: (0, n, 0)
    return pl.pallas_call(
        _dn_kernel,
        grid=(s // c,),
        in_specs=[
            pl.BlockSpec((nb, c, 3 * DN_WIDTH), blk),
            pl.BlockSpec((nb, c, DN_WIDTH), blk),
            pl.BlockSpec((nb, c, LANES), blk),
            _const_spec(conv_w.shape),
            _const_spec((1, LANES)),
            _const_spec((1, LANES)),
            _const_spec((1, DN_HEAD_DIM)),
        ],
        out_specs=pl.BlockSpec((nb, c, DN_WIDTH), blk),
        out_shape=jax.ShapeDtypeStruct((nb, s, DN_WIDTH), BF16),
        scratch_shapes=[
            pltpu.VMEM((nb, SUBLANES, 3 * DN_WIDTH), F32),
            pltpu.VMEM((nb * DN_HEADS, DN_HEAD_DIM, DN_HEAD_DIM), F32),
            pltpu.VMEM((nb * c, nb * c), BF16),
        ],
        compiler_params=_params("arbitrary"),
        name="delta_net",
    )(qkv3, z3, ab3, conv_w, alog_row, dtb_row, dn_norm_w)


def _mix_kernel(x_ref, ydn_ref, ysg_ref, mod_ref, pn_ref, postn_ref, fpn_ref,
                wbg_ref, bbg_ref, wpd_ref, wps_ref, wout_ref, wrh_ref, wrl_ref,
                x1_ref, hf_ref, sct_ref):
    d = x_ref.shape[1]
    x = x_ref[...]
    mod = mod_ref[0]
    hm = _rms(x) * pn_ref[...] * (1.0 + mod[1:2, :]) + mod[0:1, :]
    gates = _sigmoid(_dot(hm.astype(BF16), wbg_ref[...]) + bbg_ref[...])
    merged = gates[:, :d] * _dot(ydn_ref[...], wpd_ref[...]) + gates[:, d:] * _dot(ysg_ref[...], wps_ref[...])
    y = _dot(merged.astype(BF16), wout_ref[...])
    x1 = x + mod[2:3, :] * (_rms(y) * postn_ref[...])
    x1_ref[...] = x1
    hf = _rms(x1) * fpn_ref[...] * (1.0 + mod[4:5, :]) + mod[3:4, :]
    for j in range(ROW_TILE):
        hf_ref[pl.ds(j, hf.shape[0], stride=ROW_TILE), :] = hf[:, j * LANES:(j + 1) * LANES]
    hh, hl = _split2(hf)
    logits_t = _dot_nt(wrh_ref[...], hh) + (_dot_nt(wrh_ref[...], hl) + _dot_nt(wrl_ref[...], hh))
    sct_ref[...] = _sigmoid(logits_t)


def _mix(x2, ydn, ysg, mod3, pre_norm, post_norm, ffn_pre_norm, wbg, bbg, wpd, wps, wout, wrh, wrl, seq, tm):
    t, d = x2.shape
    tiles_per_batch = seq // tm
    tok = lambda i: (i, 0)
    consts = [pre_norm, post_norm, ffn_pre_norm, wbg, bbg, wpd, wps, wout, wrh, wrl]
    return pl.pallas_call(
        _mix_kernel,
        grid=(t // tm,),
        in_specs=[
            pl.BlockSpec((tm, d), tok),
            pl.BlockSpec((tm, DN_WIDTH), tok),
            pl.BlockSpec((tm, SG_WIDTH), tok),
            pl.BlockSpec((1, 6, d), lambda i: (i // tiles_per_batch, 0, 0)),
        ] + [_weight_spec(a.shape) for a in consts],
        out_specs=[
            pl.BlockSpec((tm, d), tok),
            pl.BlockSpec((tm * ROW_TILE, LANES), tok),
            pl.BlockSpec((N_EXPERTS, tm), lambda i: (0, i)),
        ],
        out_shape=[
            jax.ShapeDtypeStruct((t, d), F32),
            jax.ShapeDtypeStruct((t * ROW_TILE, LANES), F32),
            jax.ShapeDtypeStruct((N_EXPERTS, t), F32),
        ],
        compiler_params=_params("arbitrary"),
        name="mix",
    )(x2, ydn, ysg, mod3, *consts)


def _first_argmax(vals, idx):
    m = jnp.max(vals, axis=0, keepdims=True)
    first = jnp.min(jnp.where(vals == m, idx, jnp.int32(2 ** 30)), axis=0, keepdims=True)
    return m, first


def _route_kernel(sct_ref, bias_ref, idx_ref, wtt_ref, rank_ref, cnt_ref, carry_ref, upper_ref):
    tm = sct_ref.shape[1]

    @pl.when(pl.program_id(0) == 0)
    def _():
        carry_ref[...] = jnp.zeros_like(carry_ref)
        ti = lax.broadcasted_iota(I32, (tm, tm), 0)
        tj = lax.broadcasted_iota(I32, (tm, tm), 1)
        upper_ref[...] = (ti < tj).astype(BF16)

    scores = sct_ref[...]
    sel = scores + bias_ref[...]
    erow = lax.broadcasted_iota(I32, (N_EXPERTS, tm), 0)
    grow = lax.broadcasted_iota(I32, (GROUP_SIZE, tm), 0)

    gs = []
    for gidx in range(N_EXPERT_GROUPS):
        sg = sel[gidx * GROUP_SIZE:(gidx + 1) * GROUP_SIZE]
        m1, first = _first_argmax(sg, grow)
        m2 = jnp.max(jnp.where(grow == first, NEG_INF, sg), axis=0, keepdims=True)
        gs.append(m1 + m2)
    gsc = jnp.concatenate(gs, axis=0)
    giota = lax.broadcasted_iota(I32, (N_EXPERT_GROUPS, tm), 0)
    gmask = jnp.zeros((N_EXPERT_GROUPS, tm), F32)
    cur = gsc
    for _ in range(TOPK_GROUPS):
        _, gi = _first_argmax(cur, giota)
        pick = giota == gi
        gmask = jnp.where(pick, 1.0, gmask)
        cur = jnp.where(pick, NEG_INF, cur)
    masked = jnp.concatenate(
        [jnp.where(gmask[gidx:gidx + 1, :] > 0.5, sel[gidx * GROUP_SIZE:(gidx + 1) * GROUP_SIZE], NEG_INF)
         for gidx in range(N_EXPERT_GROUPS)], axis=0)

    cur = masked
    idxs, wts = [], []
    onehot = jnp.zeros((N_EXPERTS, tm), F32)
    for _ in range(TOP_K):
        _, ei = _first_argmax(cur, erow)
        pick = erow == ei
        idxs.append(ei)
        wts.append(jnp.sum(jnp.where(pick, scores, 0.0), axis=0, keepdims=True))
        onehot = jnp.where(pick, 1.0, onehot)
        cur = jnp.where(pick, NEG_INF, cur)
    idx = jnp.concatenate(idxs, axis=0)
    wt = jnp.concatenate(wts, axis=0)
    wt = wt / jnp.sum(wt, axis=0, keepdims=True) * ROUTED_SCALE
    idx_ref[...] = idx
    wpad = jnp.concatenate([wt, jnp.zeros((LANES - TOP_K, tm), F32)], axis=0)
    wtt_ref[...] = wpad.T

    before = _dot(onehot.astype(BF16), upper_ref[...]) + carry_ref[...]
    rank_ref[...] = jnp.concatenate(
        [jnp.sum(jnp.where(erow == idxs[kk], before, 0.0), axis=0, keepdims=True) for kk in range(TOP_K)],
        axis=0).astype(I32)
    total = carry_ref[...] + jnp.sum(onehot, axis=1, keepdims=True)
    carry_ref[...] = total
    cnt_ref[...] = total


def _route(sct, bias_col, tm):
    e, t = sct.shape
    tile = lambda i: (0, i)
    return pl.pallas_call(
        _route_kernel,
        grid=(t // tm,),
        in_specs=[pl.BlockSpec((e, tm), tile), _const_spec((e, 1))],
        out_specs=[
            pl.BlockSpec((TOP_K, tm), tile),
            pl.BlockSpec((tm, LANES), lambda i: (i, 0)),
            pl.BlockSpec((TOP_K, tm), tile),
            _const_spec((e, 1)),
        ],
        out_shape=[
            jax.ShapeDtypeStruct((TOP_K, t), I32),
            jax.ShapeDtypeStruct((t, LANES), F32),
            jax.ShapeDtypeStruct((TOP_K, t), I32),
            jax.ShapeDtypeStruct((e, 1), F32),
        ],
        scratch_shapes=[pltpu.VMEM((e, 1), F32), pltpu.VMEM((tm, tm), BF16)],
        compiler_params=_params("arbitrary"),
        name="route",
    )(sct, bias_col)


def _dest_kernel(cnt_ref, idx_ref, rank_ref, dest_ref, meta_ref, pstart_ref):
    tm = idx_ref.shape[1]
    e = N_EXPERTS

    @pl.when(pl.program_id(0) == 0)
    def _():
        cnt = cnt_ref[...]
        padded = jnp.floor((cnt + (MOE_BLOCK - 1)) * (1.0 / MOE_BLOCK)) * MOE_BLOCK
        pw = jnp.broadcast_to(padded, (e, LANES))
        ri = lax.broadcasted_iota(I32, (e, e), 0)
        ci = lax.broadcasted_iota(I32, (e, e), 1)
        lower = (ri >= ci).astype(BF16)
        ph, pm, plo = _split3(pw)
        pends = _dot(lower, ph) + (_dot(lower, pm) + _dot(lower, plo))
        pstart = pends - pw
        pstart_ref[...] = pstart[:, 0:1]
        lane = lax.broadcasted_iota(I32, (e, LANES), 1)
        blocks = jnp.where(lane == 0, pstart, pw) * (1.0 / MOE_BLOCK)
        pads = jnp.where(lane == 2, pstart + cnt, pw - cnt)
        meta_ref[...] = jnp.where(lane < 2, blocks, pads).astype(I32)

    pstart_col = pstart_ref[...]
    erow = lax.broadcasted_iota(I32, (e, tm), 0)
    idx = idx_ref[...]
    rows = [jnp.sum(jnp.where(erow == idx[kk:kk + 1, :], pstart_col, 0.0), axis=0, keepdims=True)
            for kk in range(TOP_K)]
    dest_ref[...] = jnp.concatenate(rows, axis=0).astype(I32) + rank_ref[...]


def _dest(cnt, idx, rank, tm):
    e = cnt.shape[0]
    t = idx.shape[1]
    tile = lambda i: (0, i)
    return pl.pallas_call(
        _dest_kernel,
        grid=(t // tm,),
        in_specs=[_const_spec((e, 1)), pl.BlockSpec((TOP_K, tm), tile), pl.BlockSpec((TOP_K, tm), tile)],
        out_specs=[pl.BlockSpec((TOP_K, tm), tile), _const_spec((e, LANES))],
        out_shape=[jax.ShapeDtypeStruct((TOP_K, t), I32), jax.ShapeDtypeStruct((e, LANES), I32)],
        scratch_shapes=[pltpu.VMEM((e, 1), F32)],
        compiler_params=_params("arbitrary"),
        name="dest",
    )(cnt, idx, rank)


def _row_copy(src, dst, sem):
    return pltpu.make_async_copy(src, dst, sem)


def _rows(first, count=1):
    return pl.ds(pl.multiple_of(first * ROW_TILE, ROW_TILE), count * ROW_TILE)


PAD_PIECES = tuple(MOE_BLOCK >> s for s in range(1, MOE_BLOCK.bit_length()))


def _zero_unassigned_rows(pad_start_ref, pad_rows_ref, xs_out, zero_ref, sem):
    zero_ref[...] = jnp.zeros_like(zero_ref)
    ne = pad_start_ref.shape[0]
    total = xs_out.shape[0] // (MOE_BLOCK * ROW_TILE)

    def piece(rows, pos):
        return pltpu.make_async_copy(zero_ref.at[_rows(0, rows)], xs_out.at[_rows(pos, rows)], sem)

    def pads(wait):
        def per_expert(ex, carry):
            pos = pad_start_ref[ex]
            pad = pad_rows_ref[ex]
            for rows in PAD_PIECES:
                @pl.when((pad & rows) != 0)
                def _():
                    piece(rows, pos).wait() if wait else piece(rows, pos).start()
                pos = pos + (pad & rows)
            return carry
        lax.fori_loop(0, ne, per_expert, 0)

    def tail(wait):
        used = (pad_start_ref[ne - 1] + pad_rows_ref[ne - 1]) // MOE_BLOCK

        def per_block(blk, carry):
            cp = piece(MOE_BLOCK, blk * MOE_BLOCK)
            cp.wait() if wait else cp.start()
            return carry
        lax.fori_loop(used, total, per_block, 0)

    pads(False)
    tail(False)
    pads(True)
    tail(True)


def _scatter_kernel(pad_start_ref, pad_rows_ref, dest_hbm, hf_ref, wsg_ref, wsu_ref, wsd_ref, xs_out, sh_ref,
                    dest_smem, zero_ref, sem_idx, sem_rows, sem_zero):
    tm = hf_ref.shape[0] // ROW_TILE
    i = pl.program_id(0)

    @pl.when(i == 0)
    def _():
        _zero_unassigned_rows(pad_start_ref, pad_rows_ref, xs_out, zero_ref, sem_zero)

    idx_copy = pltpu.make_async_copy(dest_hbm.at[i], dest_smem, sem_idx)
    idx_copy.start()
    idx_copy.wait()

    def issue(tok, carry):
        for kk in range(TOP_K):
            slot = dest_smem[kk * tm + tok]
            _row_copy(hf_ref.at[_rows(tok)], xs_out.at[_rows(slot)], sem_rows).start(priority=kk % 2)
        return carry

    lax.fori_loop(0, tm, issue, 0)

    hb = jnp.concatenate([hf_ref[pl.ds(j, tm, stride=ROW_TILE), :] for j in range(ROW_TILE)], axis=1).astype(BF16)
    hid = _silu(_dot(hb, wsg_ref[...])) * _dot(hb, wsu_ref[...])
    sh_ref[...] = _dot(hid.astype(BF16), wsd_ref[...])

    for kk in range(TOP_K):
        _row_copy(hf_ref, xs_out.at[_rows(0, tm)], sem_rows).wait()


def _scatter(pad_start, pad_rows, dest_tiles, hf, wsg, wsu, wsd, n_slots, tm):
    t = hf.shape[0] // ROW_TILE
    d = wsg.shape[0]
    const = lambda a: pl.BlockSpec(a.shape, lambda i, ps, pr: (0, 0))
    n_tiles = t // tm
    return pl.pallas_call(
        _scatter_kernel,
        grid_spec=pltpu.PrefetchScalarGridSpec(
            num_scalar_prefetch=2,
            grid=(n_tiles,),
            in_specs=[pl.BlockSpec(memory_space=pl.ANY), pl.BlockSpec((tm * ROW_TILE, LANES), lambda i, ps, pr: (i, 0)),
                      const(wsg), const(wsu), const(wsd)],
            out_specs=[pl.BlockSpec(memory_space=pl.ANY), pl.BlockSpec((tm, d), lambda i, ps, pr: (i, 0))],
            scratch_shapes=[
                pltpu.SMEM((TOP_K * tm,), I32),
                pltpu.VMEM((MOE_BLOCK * ROW_TILE, LANES), hf.dtype),
                pltpu.SemaphoreType.DMA,
                pltpu.SemaphoreType.DMA,
                pltpu.SemaphoreType.DMA,
            ],
        ),
        out_shape=[jax.ShapeDtypeStruct((n_slots * ROW_TILE, LANES), hf.dtype), jax.ShapeDtypeStruct((t, d), F32)],
        compiler_params=_params("arbitrary"),
        name="scatter",
    )(pad_start, pad_rows, dest_tiles, hf, wsg, wsu, wsd)


MOE_BUFFERS = 8


def _moe_kernel(bstart_ref, nblk_ref, xs_hbm, wg_ref, wu_ref, wd_ref, ys_hbm,
                xbuf, ybuf, wgb_ref, wub_ref, wdb_ref, sem_in, sem_out):
    ex = pl.program_id(0)
    last = pl.num_programs(0) - 1
    n = nblk_ref[ex]
    b0 = bstart_ref[ex]
    used = bstart_ref[last] + nblk_ref[last]

    def in_copy(g):
        slot = lax.rem(g, MOE_BUFFERS)
        return pltpu.make_async_copy(xs_hbm.at[_rows(g * MOE_BLOCK, MOE_BLOCK)], xbuf.at[slot], sem_in.at[slot])

    def out_copy(g):
        slot = lax.rem(g, MOE_BUFFERS)
        return pltpu.make_async_copy(ybuf.at[slot], ys_hbm.at[_rows(g * MOE_BLOCK, MOE_BLOCK)], sem_out.at[slot])

    @pl.when(ex == 0)
    def _():
        for s in range(MOE_BUFFERS - 1):
            @pl.when(s < used)
            def _():
                in_copy(s).start()

    wgb_ref[...] = wg_ref[0].astype(BF16)
    wub_ref[...] = wu_ref[0].astype(BF16)
    wdb_ref[...] = wd_ref[0].astype(BF16)

    def body(g, carry):
        slot = lax.rem(g, MOE_BUFFERS)
        in_copy(g).wait()

        @pl.when(g + (MOE_BUFFERS - 1) < used)
        def _():
            in_copy(g + (MOE_BUFFERS - 1)).start()

        @pl.when(g >= MOE_BUFFERS)
        def _():
            out_copy(g - MOE_BUFFERS).wait()

        xb = jnp.concatenate([xbuf[slot, pl.ds(j, MOE_BLOCK, stride=ROW_TILE), :] for j in range(ROW_TILE)],
                             axis=1).astype(BF16)
        hid = _silu(_dot(xb, wgb_ref[...])) * _dot(xb, wub_ref[...])
        y = _dot(hid.astype(BF16), wdb_ref[...])
        for j in range(ROW_TILE):
            ybuf[slot, pl.ds(j, MOE_BLOCK, stride=ROW_TILE), :] = y[:, j * LANES:(j + 1) * LANES]
        out_copy(g).start()
        return carry

    lax.fori_loop(b0, b0 + n, body, 0)

    @pl.when(ex == last)
    def _():
        for s in range(MOE_BUFFERS):
            @pl.when(s < used)
            def _():
                out_copy(used - 1 - s).wait()

        total = ys_hbm.shape[0] // (MOE_BLOCK * ROW_TILE)
        ybuf[0] = jnp.zeros(ybuf.shape[1:], ybuf.dtype)

        def tail_copy(blk):
            return pltpu.make_async_copy(ybuf.at[0], ys_hbm.at[_rows(blk * MOE_BLOCK, MOE_BLOCK)], sem_out.at[0])

        def start_tail(blk, carry):
            tail_copy(blk).start()
            return carry

        def wait_tail(blk, carry):
            tail_copy(blk).wait()
            return carry

        lax.fori_loop(used, total, start_tail, 0)
        lax.fori_loop(used, total, wait_tail, 0)


def _moe(bstart, nblk_e, xs, w_gate, w_up, w_down):
    ne, d, eh = w_gate.shape
    wspec = lambda shape: pl.BlockSpec(shape, lambda ex, bs, nb: (ex, 0, 0))
    return pl.pallas_call(
        _moe_kernel,
        grid_spec=pltpu.PrefetchScalarGridSpec(
            num_scalar_prefetch=2,
            grid=(ne,),
            in_specs=[pl.BlockSpec(memory_space=pl.ANY), wspec((1, d, eh)), wspec((1, d, eh)), wspec((1, eh, d))],
            out_specs=pl.BlockSpec(memory_space=pl.ANY),
            scratch_shapes=[
                pltpu.VMEM((MOE_BUFFERS, MOE_BLOCK * ROW_TILE, LANES), F32),
                pltpu.VMEM((MOE_BUFFERS, MOE_BLOCK * ROW_TILE, LANES), F32),
                pltpu.VMEM((d, eh), BF16), pltpu.VMEM((d, eh), BF16), pltpu.VMEM((eh, d), BF16),
                pltpu.SemaphoreType.DMA((MOE_BUFFERS,)),
                pltpu.SemaphoreType.DMA((MOE_BUFFERS,)),
            ],
        ),
        out_shape=jax.ShapeDtypeStruct(xs.shape, F32),
        compiler_params=_params("arbitrary"),
        name="moe",
    )(bstart, nblk_e, xs, w_gate, w_up, w_down)


def _combine_kernel(dest_hbm, ys_hbm, wtt_ref, x1_ref, sh_ref, mod_ref, postn_ref, o_ref, dest_smem0, dest_smem1, buf_ref, sem_idx, sem_rows):
    tc = x1_ref.shape[0]
    i = pl.program_id(0)
    n = pl.num_programs(0)
    dest_smem = (dest_smem0, dest_smem1)

    def idx_copy(tile, slot):
        return pltpu.make_async_copy(dest_hbm.at[tile], dest_smem[slot], sem_idx.at[slot])

    def issue_rows(slot):
        def issue(tok, carry):
            for kk in range(TOP_K):
                row = dest_smem[slot][kk * tc + tok]
                _row_copy(ys_hbm.at[_rows(row)], buf_ref.at[slot, kk, _rows(tok)],
                          sem_rows.at[slot]).start(priority=kk % 2)
            return carry
        lax.fori_loop(0, tc, issue, 0)

    @pl.when(i == 0)
    def _():
        idx_copy(0, 0).start()
        idx_copy(0, 0).wait()

        @pl.when(n > 1)
        def _():
            idx_copy(1, 1).start()
        issue_rows(0)

    def step(cur):
        nxt = 1 - cur

        @pl.when(i + 1 < n)
        def _():
            idx_copy(i + 1, nxt).wait()

            @pl.when(i + 2 < n)
            def _():
                idx_copy(i + 2, cur).start()
            issue_rows(nxt)

        for kk in range(TOP_K):
            _row_copy(ys_hbm.at[_rows(0, tc)], buf_ref.at[cur, kk], sem_rows.at[cur]).wait()

        wtt = wtt_ref[...]
        wk = [jnp.broadcast_to(wtt[:, kk:kk + 1], (tc, LANES)) for kk in range(TOP_K)]
        cols = []
        for j in range(ROW_TILE):
            lane_group = pl.ds(j, tc, stride=ROW_TILE)
            acc = buf_ref[cur, 0, lane_group, :] * wk[0]
            for kk in range(1, TOP_K):
                acc = acc + buf_ref[cur, kk, lane_group, :] * wk[kk]
            cols.append(acc)
        y = jnp.concatenate(cols, axis=1) + sh_ref[...]
        mod = mod_ref[0]
        o_ref[...] = x1_ref[...] + mod[5:6, :] * (_rms(y) * postn_ref[...])

    parity = lax.rem(i, 2)
    for cur in range(2):
        @pl.when(parity == cur)
        def _():
            step(cur)


def _combine(dest_tiles, ys, wtt, x1, shared, mod3, ffn_post_norm, seq, tc):
    t, d = x1.shape
    tiles_per_batch = seq // tc
    tok = lambda i: (i, 0)
    return pl.pallas_call(
        _combine_kernel,
        grid=(t // tc,),
        in_specs=[
            pl.BlockSpec(memory_space=pl.ANY),
            pl.BlockSpec(memory_space=pl.ANY),
            pl.BlockSpec((tc, LANES), tok),
            pl.BlockSpec((tc, d), tok),
            pl.BlockSpec((tc, d), tok),
            pl.BlockSpec((1, 6, d), lambda i: (i // tiles_per_batch, 0, 0)),
            _const_spec((1, d)),
        ],
        out_specs=pl.BlockSpec((tc, d), tok),
        out_shape=jax.ShapeDtypeStruct((t, d), F32),
        scratch_shapes=[
            pltpu.SMEM((TOP_K * tc,), I32),
            pltpu.SMEM((TOP_K * tc,), I32),
            pltpu.VMEM((2, TOP_K, tc * ROW_TILE, LANES), F32),
            pltpu.SemaphoreType.DMA((2,)),
            pltpu.SemaphoreType.DMA((2,)),
        ],
        compiler_params=_params("arbitrary"),
        name="combine",
    )(dest_tiles, ys, wtt, x1, shared, mod3, ffn_post_norm)


def _tile(n, want):
    t = min(n, want)
    assert n % t == 0
    return t


def _dest_tiles(dest, tile):
    k, t = dest.shape
    return dest.reshape(k, t // tile, tile).transpose(1, 0, 2).reshape(t // tile, k * tile)


def kernel(x, c, ada_w, ada_b, mix_pre_norm, mix_post_norm, w_in, conv_w, a_log, dt_bias, dn_norm_w, sg_ln_w, sg_ln_b, sg_w, sg_b, w_branch_gate, b_branch_gate, w_proj_dn, w_proj_sg, w_out, ffn_pre_norm, ffn_post_norm, w_router, router_bias, w_exp_gate, w_exp_up, w_exp_down, w_sh_gate, w_sh_up, w_sh_down):
    nb, seq, d = x.shape
    depth = ada_w.shape[0]
    t = nb * seq
    tm = _tile(seq, DENSE_TILE)
    tr = _tile(t, ROUTE_TILE)
    tsc = _tile(t, SCATTER_TILE)
    tcm = _tile(seq, COMBINE_TILE)
    nblk = -(-t * TOP_K // MOE_BLOCK) + N_EXPERTS
    row = lambda v: v.reshape(1, -1)
    pad_lanes = lambda v: jnp.pad(v.astype(F32), (0, LANES - v.shape[0])).reshape(1, LANES)

    x2 = x.reshape(t, d)
    for l in range(depth):
        mod3 = _ada(c, ada_w[l], ada_b[l]).reshape(nb, 6, d)

        wi = w_in[l]
        qkvz, ab_cols, uv = wi[:, :4 * DN_WIDTH], wi[:, 4 * DN_WIDTH:4 * DN_WIDTH + 2 * DN_HEADS], wi[:, 4 * DN_WIDTH + 2 * DN_HEADS:]
        w1 = jnp.concatenate([qkvz, ab_cols, jnp.zeros((d, LANES - 2 * DN_HEADS), wi.dtype)], axis=1).astype(BF16)
        qkv, z, ab, ysg = _in_proj(x2, mod3, row(mix_pre_norm[l]), w1, uv.astype(BF16), row(sg_ln_w[l]), row(sg_ln_b[l]),
                                   sg_w[l], sg_b[l].T, seq, tm)

        ydn = _delta_net(qkv.reshape(nb, seq, -1), z.reshape(nb, seq, -1), ab.reshape(nb, seq, -1), conv_w[l],
                         pad_lanes(a_log[l]), pad_lanes(dt_bias[l]), row(dn_norm_w[l]))

        wr_t = w_router[l].T
        wrh = wr_t.astype(BF16)
        wrl = (wr_t - wrh.astype(F32)).astype(BF16)
        x1, hf, sct = _mix(
            x2, ydn.reshape(t, -1), ysg, mod3, row(mix_pre_norm[l]), row(mix_post_norm[l]), row(ffn_pre_norm[l]),
            w_branch_gate[l].astype(BF16), row(b_branch_gate[l]), w_proj_dn[l].astype(BF16), w_proj_sg[l].astype(BF16),
            w_out[l].astype(BF16), wrh, wrl, seq, tm)

        idx, wtt, rank, cnt = _route(sct, router_bias[l].reshape(-1, 1), tr)
        dest, meta = _dest(cnt, idx, rank, tr)
        xs, shared = _scatter(meta[:, 2], meta[:, 3], _dest_tiles(dest, tsc), hf, w_sh_gate[l].astype(BF16),
                              w_sh_up[l].astype(BF16), w_sh_down[l].astype(BF16), nblk * MOE_BLOCK, tsc)
        ys = _moe(meta[:, 0], meta[:, 1], xs, w_exp_gate[l], w_exp_up[l], w_exp_down[l])
        x2 = _combine(_dest_tiles(dest, tcm), ys, wtt, x1, shared, mod3, row(ffn_post_norm[l]), seq, tcm)
    return x2.reshape(nb, seq, d)
```
